```python
import jax
import jax.numpy as jnp
from jax import lax
import numpy as np

D_MODEL = 1024
BATCH = 8
SEQ = 2048
DEPTH = 4
DEC_BATCH = 128
DEC_SEQ = 1
PAST_LEN = 2048
PAGE_SIZE = 128

N_META = 16
N_EVEN = (DEPTH + 1) // 2
N_ODD = DEPTH // 2
A_HEADS = 4
A_DK = 128
A_DV = 128
A_WIDTH = A_HEADS * A_DV
A_CHUNK = 64
B_HEADS = 8
B_KV_HEADS = 2
B_HEAD_DIM = 64
B_WIDTH = B_HEADS * B_HEAD_DIM
B_KV_WIDTH = B_KV_HEADS * B_HEAD_DIM
IDX_HEADS = 8
IDX_DIM = 64
TOPK_MAX = 256
Q_BLOCK = 128
ROPE_THETA = 10000.0
C_WIDTH = 512
C_CONV = 31
D_WIDTH = 512
D_CONV = 4
D_BLOCKS = 8
D_BLOCK_W = D_WIDTH // D_BLOCKS
LRU_C = 8.0
N_GROUPS = 4
EXPERTS_PER_GROUP = 4
N_EXPERTS = N_GROUPS * EXPERTS_PER_GROUP
TOP_E = 2
EXPERT_FF = 256
DN_ALPHA = (2 * DEPTH) ** 0.25
DN_BETA = (8 * DEPTH) ** -0.25
LN_EPS = 1e-5
RMS_EPS = 1e-6
NEG_BIG = -1e30
LB_TINY = 1e-30
EVEN_PARTS = (A_HEADS * A_DK, A_HEADS * A_DK, A_WIDTH, A_WIDTH, B_WIDTH, B_KV_WIDTH, B_KV_WIDTH, IDX_HEADS * IDX_DIM, IDX_DIM, IDX_HEADS)
EVEN_INIT_SCALE = (1.0, 1.0, DN_BETA, 1.0, 1.0, 1.0, DN_BETA, 1.0, 1.0, 1.0)
EVEN_IN = sum(EVEN_PARTS)
MIX_EVEN = A_WIDTH + B_WIDTH
ODD_PARTS = (C_WIDTH, C_WIDTH, D_WIDTH, D_WIDTH)
ODD_IN = sum(ODD_PARTS)
MIX_ODD = C_WIDTH + D_WIDTH

kernel_name = 'hybrid_hgrn2_dsa_conformer_rglru_hmoe_step'

F32 = jnp.float32


def split_cols(h, parts):
    cuts = [int(c) for c in np.cumsum(parts)[:-1]]
    return jnp.split(h, cuts, axis=-1)


def layer_norm(x, g, b):
    xf = x.astype(F32)
    mu = jnp.mean(xf, axis=-1, keepdims=True)
    var = jnp.mean(jnp.square(xf - mu), axis=-1, keepdims=True)
    return ((xf - mu) * lax.rsqrt(var + LN_EPS) * g.astype(F32) + b.astype(F32)).astype(x.dtype)


def post_norm(x, sub, g, b):
    return layer_norm(DN_ALPHA * x + sub.astype(x.dtype), g, b)


def rope(x, pos):
    half = x.shape[-1] // 2
    inv = ROPE_THETA ** (-jnp.arange(half, dtype=F32) / half)
    ang = pos.astype(F32)[:, None] * inv[None, :]
    cos = jnp.cos(ang)[:, None, :]
    sin = jnp.sin(ang)[:, None, :]
    xf = x.astype(F32)
    x1, x2 = xf[..., :half], xf[..., half:]
    return jnp.concatenate([x1 * cos - x2 * sin, x2 * cos + x1 * sin], axis=-1).astype(x.dtype)


def causal_dwconv(x_ext, w, b):
    ch = x_ext.shape[-1]
    y = lax.conv_general_dilated(x_ext, w[:, None, :].astype(x_ext.dtype), window_strides=(1,), padding='VALID',
                                 dimension_numbers=('NWC', 'WIO', 'NWC'), feature_group_count=ch)
    return y + b.astype(y.dtype)


def even_project(x, w_in, lb, pos):
    bsz, t, _ = x.shape
    qa, fa, ia, ga, qb, kb, vb, qi, ki, wi = split_cols(x @ w_in, EVEN_PARTS)
    heads = lambda a, h: a.reshape(bsz, t, h, -1)
    lbh = lb.astype(F32).reshape(A_HEADS, A_DK)
    zf = heads(fa, A_HEADS).astype(F32)
    log_f = jnp.logaddexp(jnp.log(jnp.maximum(lbh, LB_TINY)), jnp.log1p(-lbh) + jax.nn.log_sigmoid(zf))
    k_a = (1.0 - lbh) * jax.nn.sigmoid(-zf)
    q_a = jax.nn.silu(heads(qa, A_HEADS).astype(F32))
    v_a = heads(ia, A_HEADS).astype(F32)
    q_b = rope(heads(qb, B_HEADS), pos)
    k_b = rope(heads(kb, B_KV_HEADS), pos)
    v_b = heads(vb, B_KV_HEADS)
    q_i = rope(heads(qi, IDX_HEADS), pos)
    k_i = rope(ki[:, :, None, :], pos)[:, :, 0]
    return (q_a, k_a, v_a, log_f), ga, (q_b, k_b, v_b, q_i, k_i, wi)


def hgrn_chunk(s, q, k, v, log_f):
    c = q.shape[1]
    b = jnp.cumsum(log_f, axis=1)
    causal = jnp.tril(jnp.ones((c, c), dtype=bool))[None, :, :, None, None]
    diff = b[:, :, None] - b[:, None, :]
    decay = jnp.where(causal, jnp.exp(jnp.where(causal, diff, 0.0)), 0.0)
    scores = jnp.einsum('bthk,bshk,btshk->bhts', q, k, decay)
    o_intra = jnp.einsum('bhts,bshv->bthv', scores, v)
    o_inter = jnp.einsum('bthk,bhkv->bthv', q * jnp.exp(b), s)
    b_last = b[:, -1]
    k_dec = k * jnp.exp(b_last[:, None] - b)
    s_new = jnp.exp(b_last)[..., None] * s + jnp.einsum('bshk,bshv->bhkv', k_dec, v)
    return s_new, o_intra + o_inter


def hgrn_prompt(q, k, v, log_f):
    bsz, t = q.shape[:2]
    s0 = jnp.zeros((bsz, A_HEADS, A_DK, A_DV), F32)
    s, o_meta = hgrn_chunk(s0, q[:, :N_META], k[:, :N_META], v[:, :N_META], log_f[:, :N_META])
    n_chunk = (t - N_META) // A_CHUNK
    def chunks(a):
        return jnp.moveaxis(a[:, N_META:].reshape((bsz, n_chunk, A_CHUNK) + a.shape[2:]), 1, 0)
    s, o_rest = lax.scan(lambda st, c: hgrn_chunk(st, *c), s, (chunks(q), chunks(k), chunks(v), chunks(log_f)))
    o_rest = jnp.moveaxis(o_rest, 0, 1).reshape(bsz, t - N_META, A_HEADS, A_DV)
    return jnp.concatenate([o_meta, o_rest], axis=1), s


def hgrn_sample(s0, q, k, v, log_f):
    def step(s, inp):
        qt, kt, vt, lft = inp
        s = jnp.exp(lft)[..., None] * s + kt[..., None] * vt[..., None, :]
        return s, jnp.einsum('bhk,bhkv->bhv', qt, s)
    tm = lambda a: jnp.moveaxis(a, 1, 0)
    s, o = lax.scan(step, s0.astype(F32), (tm(q), tm(k), tm(v), tm(log_f)))
    return jnp.moveaxis(o, 0, 1), s


def index_scores(qi, wi, ki):
    dots = jnp.einsum('bqhd,bsd->bqhs', qi.astype(F32), ki.astype(F32)) * (IDX_DIM ** -0.5)
    return jnp.einsum('bqh,bqhs->bqs', wi.astype(F32) * (IDX_HEADS ** -0.5), jax.nn.relu(dots))


def select_keys(score, visible, n_sel):
    _, idx = lax.top_k(jnp.where(visible, score, NEG_BIG), n_sel)
    return idx


def sparse_attend(q, k_sel, v_sel, valid):
    bsz, nq = q.shape[:2]
    qg = q.reshape(bsz, nq, B_KV_HEADS, B_HEADS // B_KV_HEADS, B_HEAD_DIM).astype(F32)
    logits = jnp.einsum('bqkgd,bqnkd->bqkgn', qg, k_sel.astype(F32)) * (B_HEAD_DIM ** -0.5)
    logits = jnp.where(valid[:, :, None, None, :], logits, NEG_BIG)
    p = jax.nn.softmax(logits, axis=-1)
    o = jnp.einsum('bqkgn,bqnkd->bqkgd', p, v_sel.astype(F32))
    return o.reshape(bsz, nq, B_WIDTH)


def dsa_prompt(q, k, v, qi, ki, wi):
    bsz, t = q.shape[:2]
    n_sel = min(TOPK_MAX, t // 4)
    n_blk = -(-t // Q_BLOCK)
    t_pad = n_blk * Q_BLOCK
    def blocks(a):
        a = jnp.pad(a, [(0, 0), (0, t_pad - t)] + [(0, 0)] * (a.ndim - 2))
        return jnp.moveaxis(a.reshape((bsz, n_blk, Q_BLOCK) + a.shape[2:]), 1, 0)
    key_pos = jnp.arange(t)
    b_idx = jnp.arange(bsz)[:, None, None]
    def one_block(blk):
        qb, qib, wib, qpos = blk
        score = index_scores(qib, wib, ki)
        sel = select_keys(score, key_pos[None, :] <= qpos[:, None], n_sel)
        valid = sel <= qpos[None, :, None]
        return sparse_attend(qb, k[b_idx, sel], v[b_idx, sel], valid)
    qpos_blocks = jnp.arange(t_pad).reshape(n_blk, Q_BLOCK)
    out = lax.map(one_block, (blocks(q), blocks(qi), blocks(wi), qpos_blocks))
    return jnp.moveaxis(out, 0, 1).reshape(bsz, t_pad, B_WIDTH)[:, :t]


def dsa_sample(q, k, v, qi, ki, wi, cache_k, cache_v, cache_kidx, page_table, layer):
    dbsz, dseq = q.shape[:2]
    past = page_table.shape[1] * PAGE_SIZE
    n_keys = past + dseq
    n_sel = min(TOPK_MAX, n_keys // 4)
    ki_past = cache_kidx[page_table, :, layer].reshape(dbsz, past, IDX_DIM)
    ki_all = jnp.concatenate([ki_past.astype(ki.dtype), ki], axis=1)
    score = index_scores(qi, wi, ki_all)
    qpos = past + jnp.arange(dseq)
    sel = select_keys(score, jnp.arange(n_keys)[None, :] <= qpos[:, None], n_sel)
    from_past = (sel < past)[..., None, None]
    sp = jnp.minimum(sel, past - 1)
    b_idx = jnp.arange(dbsz)[:, None, None]
    phys = page_table[b_idx, sp // PAGE_SIZE]
    off = sp % PAGE_SIZE
    sn = jnp.clip(sel - past, 0, dseq - 1)
    k_sel = jnp.where(from_past, cache_k[phys, off, layer].astype(k.dtype), k[b_idx, sn])
    v_sel = jnp.where(from_past, cache_v[phys, off, layer].astype(v.dtype), v[b_idx, sn])
    valid = sel <= qpos[None, :, None]
    return sparse_attend(q, k_sel, v_sel, valid)


def even_output(o_a, g_a, o_b, norm_g, w_out):
    bsz, t = o_a.shape[:2]
    o_a = o_a * lax.rsqrt(jnp.mean(jnp.square(o_a), axis=-1, keepdims=True) + RMS_EPS) * norm_g.astype(F32)
    o_a = o_a.reshape(bsz, t, A_WIDTH) * jax.nn.silu(g_a.astype(F32))
    mixed = jnp.concatenate([o_a, o_b.astype(F32)], axis=-1).astype(w_out.dtype)
    return mixed @ w_out


def lru_scan(h0, a, u):
    def step(h, au):
        h = au[0] * h + au[1]
        return h, h
    h, hs = lax.scan(step, h0, (jnp.moveaxis(a, 1, 0), jnp.moveaxis(u, 1, 0)))
    return jnp.moveaxis(hs, 0, 1), h


def odd_mixer(x, buf_c, buf_d, h0, w_in, w_out, cw, cb, lng, lnb, dw, db, wa, ba, wx, bx, lam):
    ca, cg, dx, dg = split_cols(x @ w_in, ODD_PARTS)
    bsz, t = x.shape[:2]
    u = ca * jax.nn.sigmoid(cg)
    ext_c = jnp.concatenate([buf_c.astype(u.dtype), u], axis=1)
    yc = jax.nn.silu(layer_norm(causal_dwconv(ext_c, cw, cb), lng, lnb).astype(F32))
    ext_d = jnp.concatenate([buf_d.astype(dx.dtype), dx], axis=1)
    xc = causal_dwconv(ext_d, dw, db).astype(F32)
    xb = xc.reshape(bsz, t, D_BLOCKS, D_BLOCK_W)
    r = jax.nn.sigmoid(jnp.einsum('btni,nij->btnj', xb, wa.astype(F32)).reshape(bsz, t, D_WIDTH) + ba.astype(F32))
    ig = jax.nn.sigmoid(jnp.einsum('btni,nij->btnj', xb, wx.astype(F32)).reshape(bsz, t, D_WIDTH) + bx.astype(F32))
    log_a = -LRU_C * r * jax.nn.softplus(-lam.astype(F32))
    a = jnp.exp(log_a)
    drive = jnp.sqrt(jnp.maximum(-jnp.expm1(2.0 * log_a), 0.0)) * ig * xc
    hs, h_last = lru_scan(h0.astype(F32), a, drive)
    yd = hs * jax.nn.gelu(dg.astype(F32))
    out = jnp.concatenate([yc, yd], axis=-1).astype(w_out.dtype) @ w_out
    return out, ext_c[:, -(C_CONV - 1):], ext_d[:, -(D_CONV - 1):], h_last


def hier_moe(x, rg_w, rg_b, re_w, re_b, w_gate, w_up, w_down):
    n = x.shape[0]
    xf = x.astype(F32)
    g_prob = jax.nn.softmax(xf @ rg_w.astype(F32) + rg_b.astype(F32), axis=-1)
    g_val, g_idx = lax.top_k(g_prob, 1)
    e_logits = (xf @ re_w.astype(F32) + re_b.astype(F32)).reshape(n, N_GROUPS, EXPERTS_PER_GROUP)
    e_logits = jnp.take_along_axis(e_logits, g_idx[:, :, None], axis=1)[:, 0]
    e_val, e_idx = lax.top_k(jax.nn.softmax(e_logits, axis=-1), TOP_E)
    e_val = e_val / jnp.sum(e_val, axis=-1, keepdims=True)
    expert = g_idx * EXPERTS_PER_GROUP + e_idx
    gates = jnp.sum(jax.nn.one_hot(expert, N_EXPERTS, dtype=F32) * (g_val * e_val)[..., None], axis=1)
    hg = jnp.einsum('nd,edf->nef', x, w_gate).astype(F32)
    hu = jnp.einsum('nd,edf->nef', x, w_up).astype(F32)
    act = (jax.nn.silu(hg) * hu * gates[..., None]).astype(x.dtype)
    return jnp.einsum('nef,efd->nd', act, w_down)


def setup_inputs(seed: int = 0) -> dict:
    key = jax.random.key(seed)
    keys = iter(jax.random.split(key, 48))
    def nrm(shape, scale):
        return jax.random.normal(next(keys), shape, jnp.float32) * scale
    n_pages = PAST_LEN // PAGE_SIZE
    n_phys = (DEC_BATCH * n_pages * 5 + 3) // 4
    x_prompt = nrm((BATCH, SEQ, D_MODEL), 1.0)
    x_sample = nrm((DEC_BATCH, DEC_SEQ, D_MODEL), 1.0)
    cache_k = nrm((n_phys, PAGE_SIZE, N_EVEN, B_KV_HEADS, B_HEAD_DIM), 1.0)
    cache_v = nrm((n_phys, PAGE_SIZE, N_EVEN, B_KV_HEADS, B_HEAD_DIM), DN_BETA)
    cache_kidx = nrm((n_phys, PAGE_SIZE, N_EVEN, IDX_DIM), 1.0)
    state_hgrn = nrm((DEC_BATCH, N_EVEN, A_HEADS, A_DK, A_DV), 0.1)
    state_conv_c = nrm((DEC_BATCH, N_ODD, C_CONV - 1, C_WIDTH), 0.5)
    state_conv_d = nrm((DEC_BATCH, N_ODD, D_CONV - 1, D_WIDTH), 1.0)
    state_lru = nrm((DEC_BATCH, N_ODD, D_WIDTH), 0.5)
    perm = jax.random.permutation(next(keys), n_phys)
    page_table = perm[:DEC_BATCH * n_pages].reshape(DEC_BATCH, n_pages).astype(jnp.int32)
    meta_tokens = nrm((N_META, D_MODEL), 1.0)
    even_scale = jnp.concatenate([jnp.full((p,), s, jnp.float32) for p, s in zip(EVEN_PARTS, EVEN_INIT_SCALE)])
    w_in_even = nrm((N_EVEN, D_MODEL, EVEN_IN), D_MODEL ** -0.5) * even_scale
    w_out_even = nrm((N_EVEN, MIX_EVEN, D_MODEL), MIX_EVEN ** -0.5) * DN_BETA
    hgrn_lb_logits = nrm((N_EVEN, A_HEADS * A_DK), 1.0)
    hgrn_norm_g = 1.0 + nrm((N_EVEN, A_DV), 0.02)
    w_in_odd = nrm((N_ODD, D_MODEL, ODD_IN), D_MODEL ** -0.5)
    w_out_odd = nrm((N_ODD, MIX_ODD, D_MODEL), MIX_ODD ** -0.5) * DN_BETA
    conv_c_w = nrm((N_ODD, C_CONV, C_WIDTH), C_CONV ** -0.5)
    conv_c_b = nrm((N_ODD, C_WIDTH), 0.02)
    conv_c_ln_g = 1.0 + nrm((N_ODD, C_WIDTH), 0.02)
    conv_c_ln_b = nrm((N_ODD, C_WIDTH), 0.02)
    conv_d_w = nrm((N_ODD, D_CONV, D_WIDTH), D_CONV ** -0.5)
    conv_d_b = nrm((N_ODD, D_WIDTH), 0.02)
    lru_wa = nrm((N_ODD, D_BLOCKS, D_BLOCK_W, D_BLOCK_W), D_BLOCK_W ** -0.5)
    lru_ba = nrm((N_ODD, D_WIDTH), 0.02)
    lru_wx = nrm((N_ODD, D_BLOCKS, D_BLOCK_W, D_BLOCK_W), D_BLOCK_W ** -0.5)
    lru_bx = nrm((N_ODD, D_WIDTH), 0.02)
    a_target = jax.random.uniform(next(keys), (N_ODD, D_WIDTH), jnp.float32, 0.9, 0.999) ** (1.0 / LRU_C)
    lru_lambda = jnp.log(a_target) - jnp.log1p(-a_target)
    ln1_g = 1.0 + nrm((DEPTH, D_MODEL), 0.02)
    ln1_b = nrm((DEPTH, D_MODEL), 0.02)
    ln2_g = 1.0 + nrm((DEPTH, D_MODEL), 0.02)
    ln2_b = nrm((DEPTH, D_MODEL), 0.02)
    router_g_w = nrm((DEPTH, D_MODEL, N_GROUPS), D_MODEL ** -0.5)
    router_g_b = nrm((DEPTH, N_GROUPS), 0.01)
    router_e_w = nrm((DEPTH, D_MODEL, N_EXPERTS), D_MODEL ** -0.5)
    router_e_b = nrm((DEPTH, N_EXPERTS), 0.01)
    w_gate = nrm((DEPTH, N_EXPERTS, D_MODEL, EXPERT_FF), D_MODEL ** -0.5)
    w_up = nrm((DEPTH, N_EXPERTS, D_MODEL, EXPERT_FF), D_MODEL ** -0.5) * DN_BETA
    w_down = nrm((DEPTH, N_EXPERTS, EXPERT_FF, D_MODEL), EXPERT_FF ** -0.5) * DN_BETA
    return {'x_prompt': x_prompt, 'x_sample': x_sample, 'cache_k': cache_k, 'cache_v': cache_v,
            'cache_kidx': cache_kidx, 'state_hgrn': state_hgrn, 'state_conv_c': state_conv_c,
            'state_conv_d': state_conv_d, 'state_lru': state_lru, 'page_table': page_table,
            'meta_tokens': meta_tokens, 'w_in_even': w_in_even, 'w_out_even': w_out_even,
            'hgrn_lb_logits': hgrn_lb_logits, 'hgrn_norm_g': hgrn_norm_g, 'w_in_odd': w_in_odd,
            'w_out_odd': w_out_odd, 'conv_c_w': conv_c_w, 'conv_c_b': conv_c_b, 'conv_c_ln_g': conv_c_ln_g,
            'conv_c_ln_b': conv_c_ln_b, 'conv_d_w': conv_d_w, 'conv_d_b': conv_d_b, 'lru_wa': lru_wa,
            'lru_ba': lru_ba, 'lru_wx': lru_wx, 'lru_bx': lru_bx, 'lru_lambda': lru_lambda,
            'ln1_g': ln1_g, 'ln1_b': ln1_b, 'ln2_g': ln2_g, 'ln2_b': ln2_b,
            'router_g_w': router_g_w, 'router_g_b': router_g_b, 'router_e_w': router_e_w,
            'router_e_b': router_e_b, 'w_gate': w_gate, 'w_up': w_up, 'w_down': w_down}


def reference(x_prompt, x_sample, cache_k, cache_v, cache_kidx, state_hgrn, state_conv_c, state_conv_d, state_lru,
              page_table, meta_tokens, w_in_even, w_out_even, hgrn_lb_logits, hgrn_norm_g, w_in_odd, w_out_odd,
              conv_c_w, conv_c_b, conv_c_ln_g, conv_c_ln_b, conv_d_w, conv_d_b, lru_wa, lru_ba, lru_wx, lru_bx,
              lru_lambda, ln1_g, ln1_b, ln2_g, ln2_b, router_g_w, router_g_b, router_e_w, router_e_b,
              w_gate, w_up, w_down):
    bsz, seq, _ = x_prompt.shape
    dbsz, dseq, _ = x_sample.shape
    past_len = page_table.shape[1] * PAGE_SIZE
    meta = jnp.broadcast_to(meta_tokens.astype(x_prompt.dtype)[None], (bsz, N_META, D_MODEL))
    xp = jnp.concatenate([meta, x_prompt], axis=1)
    xs = x_sample
    pos_p = jnp.arange(N_META + seq)
    pos_s = past_len + jnp.arange(dseq)
    sm = jax.nn.softmax(hgrn_lb_logits.astype(F32), axis=0)
    lower_bounds = jnp.cumsum(sm, axis=0) - sm[0]

    kp, vp, ip, hp, cp, dp, lp = [], [], [], [], [], [], []
    ks, vs, iks, hsm, csm, dsm, lsm = [], [], [], [], [], [], []
    for layer in range(DEPTH):
        li = layer // 2
        if layer % 2 == 0:
            hg_p, g_p, at_p = even_project(xp, w_in_even[li], lower_bounds[li], pos_p)
            oa_p, s_p = hgrn_prompt(*hg_p)
            ob_p = dsa_prompt(*at_p)
            mix_p = even_output(oa_p, g_p, ob_p, hgrn_norm_g[li], w_out_even[li])
            hg_s, g_s, at_s = even_project(xs, w_in_even[li], lower_bounds[li], pos_s)
            oa_s, s_s = hgrn_sample(state_hgrn[:, li], *hg_s)
            ob_s = dsa_sample(*at_s, cache_k, cache_v, cache_kidx, page_table, li)
            mix_s = even_output(oa_s, g_s, ob_s, hgrn_norm_g[li], w_out_even[li])
            kp.append(at_p[1]); vp.append(at_p[2]); ip.append(at_p[4]); hp.append(s_p)
            ks.append(at_s[1]); vs.append(at_s[2]); iks.append(at_s[4]); hsm.append(s_s)
        else:
            odd_w = (w_in_odd[li], w_out_odd[li], conv_c_w[li], conv_c_b[li], conv_c_ln_g[li], conv_c_ln_b[li],
                     conv_d_w[li], conv_d_b[li], lru_wa[li], lru_ba[li], lru_wx[li], lru_bx[li], lru_lambda[li])
            zc = jnp.zeros((bsz, C_CONV - 1, C_WIDTH), xp.dtype)
            zd = jnp.zeros((bsz, D_CONV - 1, D_WIDTH), xp.dtype)
            zh = jnp.zeros((bsz, D_WIDTH), F32)
            mix_p, bc_p, bd_p, h_p = odd_mixer(xp, zc, zd, zh, *odd_w)
            mix_s, bc_s, bd_s, h_s = odd_mixer(xs, state_conv_c[:, li], state_conv_d[:, li], state_lru[:, li], *odd_w)
            cp.append(bc_p); dp.append(bd_p); lp.append(h_p)
            csm.append(bc_s); dsm.append(bd_s); lsm.append(h_s)
        xp = post_norm(xp, mix_p, ln1_g[layer], ln1_b[layer])
        xs = post_norm(xs, mix_s, ln1_g[layer], ln1_b[layer])
        moe_w = (router_g_w[layer], router_g_b[layer], router_e_w[layer], router_e_b[layer],
                 w_gate[layer], w_up[layer], w_down[layer])
        xp = post_norm(xp, hier_moe(xp.reshape(-1, D_MODEL), *moe_w).reshape(xp.shape), ln2_g[layer], ln2_b[layer])
        xs = post_norm(xs, hier_moe(xs.reshape(-1, D_MODEL), *moe_w).reshape(xs.shape), ln2_g[layer], ln2_b[layer])

    y_prompt = xp[:, N_META:]
    y_sample = xs
    new_k_prompt = jnp.stack(kp, axis=2)
    new_v_prompt = jnp.stack(vp, axis=2)
    new_kidx_prompt = jnp.stack(ip, axis=2)
    new_hgrn_prompt = jnp.stack(hp, axis=1)
    new_conv_c_prompt = jnp.stack(cp, axis=1)
    new_conv_d_prompt = jnp.stack(dp, axis=1)
    new_lru_prompt = jnp.stack(lp, axis=1)
    new_k_sample = jnp.stack(ks, axis=2)
    new_v_sample = jnp.stack(vs, axis=2)
    new_kidx_sample = jnp.stack(iks, axis=2)
    new_hgrn_sample = jnp.stack(hsm, axis=1)
    new_conv_c_sample = jnp.stack(csm, axis=1)
    new_conv_d_sample = jnp.stack(dsm, axis=1)
    new_lru_sample = jnp.stack(lsm, axis=1)
    return (y_prompt, y_sample, new_k_prompt, new_v_prompt, new_kidx_prompt, new_hgrn_prompt, new_conv_c_prompt,
            new_conv_d_prompt, new_lru_prompt, new_k_sample, new_v_sample, new_kidx_sample, new_hgrn_sample,
            new_conv_c_sample, new_conv_d_sample, new_lru_sample)
```

```python
import functools
import math

import numpy as np
import jax
import jax.numpy as jnp
from jax import lax
from jax.experimental import pallas as pl
from jax.experimental.pallas import tpu as pltpu

F32 = jnp.float32
BF16 = jnp.bfloat16
I32 = jnp.int32

D_MODEL = 1024
DEPTH = 4
PAGE_SIZE = 128
N_META = 16
A_HEADS = 4
A_DK = 128
A_DV = 128
A_WIDTH = A_HEADS * A_DV
B_HEADS = 8
B_KV_HEADS = 2
B_HEAD_DIM = 64
B_WIDTH = B_HEADS * B_HEAD_DIM
B_KV_WIDTH = B_KV_HEADS * B_HEAD_DIM
IDX_HEADS = 8
IDX_DIM = 64
TOPK_MAX = 256
ROPE_THETA = 10000.0
C_WIDTH = 512
C_CONV = 31
D_WIDTH = 512
D_CONV = 4
D_BLOCKS = 8
D_BLOCK_W = D_WIDTH // D_BLOCKS
LRU_C = 8.0
N_GROUPS = 4
EXPERTS_PER_GROUP = 4
N_EXPERTS = N_GROUPS * EXPERTS_PER_GROUP
TOP_E = 2
EXPERT_FF = 256
DN_ALPHA = (2 * DEPTH) ** 0.25
LN_EPS = 1e-5
RMS_EPS = 1e-6
NEG_BIG = -1e30
LB_TINY = 1e-30

LANES = 128
SUBLANES = 8
ROW_TILE = 128
VMEM_LIMIT_BYTES = 56 * 1024 * 1024

EV_QA, EV_FA, EV_IA, EV_GA = 0, 512, 1024, 1536
EV_QB, EV_QI, EV_KB, EV_VB, EV_KI, EV_WI = 2048, 2560, 3072, 3200, 3328, 3456
EV_COLS = 3584
PAD_ROWS = ROW_TILE - N_META
INT_MIN = -2147483648


def _cparams(sem):
    return pltpu.CompilerParams(dimension_semantics=sem, vmem_limit_bytes=VMEM_LIMIT_BYTES)


def _sigmoid(x):
    return 1.0 / (1.0 + jnp.exp(-x))


def _log_sigmoid(x):
    return jnp.minimum(x, 0.0) - jnp.log1p(jnp.exp(-jnp.abs(x)))


def _layer_norm(y, g, b):
    mu = jnp.mean(y, axis=-1, keepdims=True)
    d = y - mu
    var = jnp.mean(d * d, axis=-1, keepdims=True)
    return d * lax.rsqrt(var + LN_EPS) * g + b


def _keep_rows(tile, tm, seq_pad):
    r = (tile * tm + lax.broadcasted_iota(I32, (tm, 1), 0)).astype(F32)
    pos = r - jnp.floor((r + 0.5) * (1.0 / seq_pad)) * seq_pad
    return jnp.where(pos >= PAD_ROWS, 1.0, 0.0)


def _finish_rows(y, g, b, tile, tm, seq_pad):
    out = _layer_norm(y, g, b)
    return out if seq_pad is None else out * _keep_rows(tile, tm, seq_pad)


def _mm_kernel(x_ref, w_ref, o_ref, *, tn):
    x = x_ref[...].astype(BF16)
    for c in range(0, w_ref.shape[1], tn):
        o_ref[:, c:c + tn] = jnp.dot(x, w_ref[:, c:c + tn], preferred_element_type=F32)


def _matmul(x, w, tm, tn):
    m, k = x.shape
    n = w.shape[1]
    assert m % tm == 0 and n % tn == 0
    return pl.pallas_call(
        functools.partial(_mm_kernel, tn=tn),
        grid=(m // tm,),
        in_specs=[pl.BlockSpec((tm, k), lambda i: (i, 0)),
                  pl.BlockSpec((k, n), lambda i: (0, 0))],
        out_specs=pl.BlockSpec((tm, n), lambda i: (i, 0)),
        out_shape=jax.ShapeDtypeStruct((m, n), F32),
        compiler_params=_cparams(("parallel",)),
        name="matmul",
    )(x, w)


HG_LEVELS = 7


def _hgrn_mid_rows(b_ref, level):
    half = 1 << level
    blk = half * 2
    if blk >= SUBLANES:
        pieces = []
        for start in range(0, ROW_TILE, blk):
            m = start + half - 1
            pieces.append(jnp.broadcast_to(b_ref[m:m + 1, :], (blk, LANES)))
        return pieces[0] if len(pieces) == 1 else jnp.concatenate(pieces, axis=0)
    sub = lax.broadcasted_iota(I32, (SUBLANES, LANES), 0)
    pieces = []
    for start in range(0, ROW_TILE, SUBLANES):
        acc = None
        for off in range(SUBLANES - blk, -1, -blk):
            m = start + off + half - 1
            row = jnp.broadcast_to(b_ref[m:m + 1, :], (SUBLANES, LANES))
            acc = row if acc is None else jnp.where(sub < off + blk, row, acc)
        pieces.append(acc)
    return jnp.concatenate(pieces, axis=0)


def _hgrn_kernel(qa_ref, fa_ref, ia_ref, lb_ref, o_ref, s_ref, st_ref, b_ref):
    c = pl.program_id(2)

    @pl.when(c == 0)
    def _():
        st_ref[...] = jnp.zeros_like(st_ref)

    lb = lb_ref[...]
    z = fa_ref[...]
    qa = qa_ref[...]
    v = ia_ref[...]
    la = jnp.log(jnp.maximum(lb, LB_TINY))
    lc = jnp.log1p(-lb) + _log_sigmoid(z)
    log_f = jnp.maximum(la, lc) + jnp.log1p(jnp.exp(-jnp.abs(la - lc)))
    k = (1.0 - lb) * _sigmoid(-z)
    q = qa * _sigmoid(qa)

    rows = lax.broadcasted_iota(I32, (ROW_TILE, 1), 0)
    cols = lax.broadcasted_iota(I32, (1, ROW_TILE), 1)

    b = log_f
    for lv in range(HG_LEVELS):
        d = 1 << lv
        b = b + jnp.where(rows >= d, pltpu.roll(b, d, axis=0), 0.0)
    b_ref[...] = b

    scores = jnp.zeros((ROW_TILE, ROW_TILE), F32)
    nt = (((1,), (1,)), ((), ()))
    for lv in range(HG_LEVELS):
        e = jnp.exp(-jnp.abs(b - _hgrn_mid_rows(b_ref, lv)))
        upper = (rows & (1 << lv)) != 0
        qd = jnp.where(upper, q * e, 0.0).astype(BF16)
        kd = jnp.where(upper, 0.0, k * e).astype(BF16)
        s_l = lax.dot_general(qd, kd, nt, preferred_element_type=F32)
        if lv + 1 < HG_LEVELS:
            same = (rows >> (lv + 1)) == (cols >> (lv + 1))
            scores = scores + jnp.where(same, s_l, 0.0)
        else:
            scores = scores + s_l
    diag = jnp.sum(q * k, axis=1, keepdims=True)
    scores = jnp.where(rows == cols, diag, scores)

    st = st_ref[...]
    v_bf = v.astype(BF16)
    o = jnp.dot(scores.astype(BF16), v_bf, preferred_element_type=F32)
    o = o + lax.dot_general((q * jnp.exp(b)).astype(BF16), st.astype(BF16), nt,
                            preferred_element_type=F32)
    o_ref[...] = o

    b_last = b_ref[ROW_TILE - 1:ROW_TILE, :]
    kdl = (k * jnp.exp(b_last - b)).astype(BF16)
    st_new = st * jnp.exp(b_last) + lax.dot_general(v_bf, kdl, (((0,), (0,)), ((), ())),
                                                    preferred_element_type=F32)
    st_ref[...] = st_new

    @pl.when(c == pl.num_programs(2) - 1)
    def _():
        s_ref[0, 0] = st_new.T


def _hgrn_prompt(h, lb, n_batch, n_chunks):
    m = h.shape[0]
    col = lambda base: (lambda b, hh, c: (b * n_chunks + c, base // LANES + hh))
    return pl.pallas_call(
        _hgrn_kernel,
        grid=(n_batch, A_HEADS, n_chunks),
        in_specs=[pl.BlockSpec((ROW_TILE, LANES), col(EV_QA)),
                  pl.BlockSpec((ROW_TILE, LANES), col(EV_FA)),
                  pl.BlockSpec((ROW_TILE, LANES), col(EV_IA)),
                  pl.BlockSpec((1, LANES), lambda b, hh, c: (0, hh))],
        out_specs=[pl.BlockSpec((ROW_TILE, LANES), lambda b, hh, c: (b * n_chunks + c, hh)),
                   pl.BlockSpec((1, 1, A_DK, A_DV), lambda b, hh, c: (b, hh, 0, 0))],
        out_shape=[jax.ShapeDtypeStruct((m, A_WIDTH), F32),
                   jax.ShapeDtypeStruct((n_batch, A_HEADS, A_DK, A_DV), F32)],
        scratch_shapes=[pltpu.VMEM((A_DV, A_DK), F32), pltpu.VMEM((ROW_TILE, LANES), F32)],
        compiler_params=_cparams(("parallel", "parallel", "arbitrary")),
        name="hgrn_prompt",
    )(h, h, h, lb)


HS_TB = 8


def _hgrn_sample_kernel(qa_ref, fa_ref, ia_ref, lb_ref, s_ref, o_ref, so_ref):
    lb = lb_ref[...]
    z = fa_ref[...]
    la = jnp.log(jnp.maximum(lb, LB_TINY))
    lc = jnp.log1p(-lb) + _log_sigmoid(z)
    f = jnp.exp(jnp.maximum(la, lc) + jnp.log1p(jnp.exp(-jnp.abs(la - lc))))
    k = (1.0 - lb) * _sigmoid(-z)
    qa = qa_ref[...]
    q = qa * _sigmoid(qa)
    v = ia_ref[...]
    pad = jnp.zeros((LANES - 3 * HS_TB, LANES), F32)
    for h in range(A_HEADS):
        sl = slice(h * A_DK, (h + 1) * A_DK)
        cols = jnp.concatenate([q[:, sl], k[:, sl], f[:, sl], pad], axis=0).T
        for b in range(HS_TB):
            qc = cols[:, b:b + 1]
            kc = cols[:, HS_TB + b:HS_TB + b + 1]
            fc = cols[:, 2 * HS_TB + b:2 * HS_TB + b + 1]
            s_new = fc * s_ref[b, 0, h] + kc * v[b:b + 1, sl]
            so_ref[b, h] = s_new
            o_ref[b:b + 1, sl] = jnp.sum(qc * s_new, axis=0, keepdims=True)


def _hgrn_sample(h, lb, state, layer):
    db = h.shape[0]
    row = lambda cb: pl.BlockSpec((HS_TB, A_WIDTH), lambda i: (i, cb))
    return pl.pallas_call(
        _hgrn_sample_kernel,
        grid=(db // HS_TB,),
        in_specs=[row(EV_QA // A_WIDTH), row(EV_FA // A_WIDTH), row(EV_IA // A_WIDTH),
                  pl.BlockSpec((1, A_WIDTH), lambda i: (0, 0)),
                  pl.BlockSpec((HS_TB, 1, A_HEADS, A_DK, A_DV), lambda i: (i, layer, 0, 0, 0))],
        out_specs=[row(0), pl.BlockSpec((HS_TB, A_HEADS, A_DK, A_DV), lambda i: (i, 0, 0, 0))],
        out_shape=[jax.ShapeDtypeStruct((db, A_WIDTH), F32),
                   jax.ShapeDtypeStruct((db, A_HEADS, A_DK, A_DV), F32)],
        compiler_params=_cparams(("parallel",)),
        name="hgrn_sample",
    )(h, h, h, lb, state)


def _pack_even_weight(w):
    parts = (A_HEADS * A_DK, A_HEADS * A_DK, A_WIDTH, A_WIDTH, B_WIDTH, B_KV_WIDTH, B_KV_WIDTH,
             IDX_HEADS * IDX_DIM, IDX_DIM, IDX_HEADS)
    cuts = [int(c) for c in np.cumsum(parts)[:-1]]
    qa, fa, ia, ga, qb, kb, vb, qi, ki, wi = jnp.split(w, cuts, axis=1)
    zeros = lambda n: jnp.zeros((w.shape[0], n), w.dtype)
    out = jnp.concatenate([qa, fa, ia, ga, qb, qi, kb, vb, ki, zeros(LANES - IDX_DIM),
                           wi, zeros(LANES - IDX_HEADS)], axis=1)
    assert out.shape[1] == EV_COLS
    return out.astype(BF16)

def _rope_tables(pos):
    half = B_HEAD_DIM // 2
    lane = np.arange(LANES)
    inv = ROPE_THETA ** (-(lane % half).astype(np.float64) / half)
    ang = np.asarray(pos, np.float64)[:, None] * inv[None, :]
    sign = np.where((lane % B_HEAD_DIM) < half, -1.0, 1.0)
    return np.cos(ang).astype(np.float32), (np.sin(ang) * sign[None, :]).astype(np.float32)


def _rope128(x, cos, sin, first_half):
    rot = jnp.where(first_half, pltpu.roll(x, LANES - 32, axis=1), pltpu.roll(x, 32, axis=1))
    return x * cos + rot * sin


def _even_post_kernel(qb_ref, qi_ref, kv_ref, ki_ref, cos_ref, sin_ref,
                      qbx_ref, qix_ref, kvf_ref, kvb_ref, kif_ref, kib_ref):
    cos = cos_ref[...]
    sin = sin_ref[...]
    lane = lax.broadcasted_iota(I32, (1, LANES), 1)
    first_half = (lane % B_HEAD_DIM) < (B_HEAD_DIM // 2)
    low = lane < B_HEAD_DIM
    scale = B_HEAD_DIM ** -0.5
    for pair in range(B_HEADS // 2):
        sl = slice(pair * LANES, (pair + 1) * LANES)
        qb = _rope128(qb_ref[:, sl], cos, sin, first_half) * scale
        qi = _rope128(qi_ref[:, sl], cos, sin, first_half) * scale
        qb_sw = pltpu.roll(qb, B_HEAD_DIM, axis=1)
        group = (2 * pair) // (B_HEADS // B_KV_HEADS)
        for sub in range(2):
            h = 2 * pair + sub
            src = qb if sub == group else qb_sw
            keep = low if group == 0 else jnp.logical_not(low)
            qbx_ref[:, h * LANES:(h + 1) * LANES] = jnp.where(keep, src, 0.0).astype(BF16)
            keep_i = low if sub == 0 else jnp.logical_not(low)
            qix_ref[:, h * LANES:(h + 1) * LANES] = jnp.where(keep_i, qi, 0.0).astype(BF16)
    k = _rope128(kv_ref[:, :LANES], cos, sin, first_half)
    v = kv_ref[:, LANES:]
    kvf_ref[:, :LANES] = k
    kvf_ref[:, LANES:] = v
    kvb_ref[:, :LANES] = k.astype(BF16)
    kvb_ref[:, LANES:] = v.astype(BF16)
    ki = _rope128(ki_ref[...], cos, sin, first_half)
    kif_ref[...] = ki
    kib_ref[...] = (ki + pltpu.roll(ki, B_HEAD_DIM, axis=1)).astype(BF16)


def _even_post(h, cos, sin, tm):
    m = h.shape[0]
    row = lambda w, cb: pl.BlockSpec((tm, w), lambda i: (i, cb))
    return pl.pallas_call(
        _even_post_kernel,
        grid=(m // tm,),
        in_specs=[row(512, EV_QB // 512), row(512, EV_QI // 512), row(256, EV_KB // 256),
                  row(LANES, EV_KI // LANES), row(LANES, 0), row(LANES, 0)],
        out_specs=[row(1024, 0), row(1024, 0), row(256, 0), row(256, 0), row(LANES, 0), row(LANES, 0)],
        out_shape=[jax.ShapeDtypeStruct((m, 1024), BF16), jax.ShapeDtypeStruct((m, 1024), BF16),
                   jax.ShapeDtypeStruct((m, 256), F32), jax.ShapeDtypeStruct((m, 256), BF16),
                   jax.ShapeDtypeStruct((m, LANES), F32), jax.ShapeDtypeStruct((m, LANES), BF16)],
        compiler_params=_cparams(("parallel",)),
        name="even_post",
    )(h, h, h, h, cos, sin)


def _ordinal_to_f32(k):
    return pltpu.bitcast(jnp.where(k < 0, k ^ 0x7FFFFFFF, k), F32)


def _kth_largest(count_ge, n_sel, rows):
    n_f = float(n_sel)
    base = jnp.where(count_ge(jnp.zeros((rows, 1), F32)) >= n_f, 0, INT_MIN).astype(I32)

    def bit_step(i, base):
        cand = base | (jnp.int32(1) << (30 - i))
        return jnp.where(count_ge(_ordinal_to_f32(cand)) >= n_f, cand, base)

    return _ordinal_to_f32(lax.fori_loop(0, 31, bit_step, base))


def _tie_cut(count_eq_before, need, n_cols_log2):
    def step(i, c):
        cand = c + (jnp.int32(1) << (n_cols_log2 - 1 - i))
        return jnp.where(count_eq_before(cand) < need, cand, c)
    return lax.fori_loop(0, n_cols_log2, step, jnp.zeros(need.shape, I32))


def _topk_select(score, cols, n_sel, n_cols_log2):
    rows = score.shape[0]
    count = lambda m: jnp.sum(m, axis=1, keepdims=True)
    thr = _kth_largest(lambda t: count(jnp.where(score >= t, 1.0, 0.0)), n_sel, rows)
    gt = jnp.where(score > thr, 1.0, 0.0)
    eq = jnp.where(score == thr, 1.0, 0.0)
    need = float(n_sel) - count(gt)
    c_all = jnp.full((rows, 1), (1 << n_cols_log2) - 1, I32)
    c_star = lax.cond(jnp.max(count(eq) - need) > 0.0,
                      lambda _: _tie_cut(lambda c: count(jnp.where(cols < c, eq, 0.0)), need, n_cols_log2),
                      lambda _: c_all, 0)
    return gt + jnp.where(cols <= c_star, eq, 0.0)


def _merge_head_pair(a, b, group, low):
    if group == 0:
        return jnp.where(low, a, pltpu.roll(b, B_HEAD_DIM, axis=1))
    return jnp.where(low, pltpu.roll(a, B_HEAD_DIM, axis=1), b)


DSA_KEY_BLOCK = 512


def _dsa_prompt_kernel(qb_ref, qi_ref, wi_ref, kv_ref, ki_ref, o_ref, s_ref, acc_ref, *, n_seq, n_sel, n_cols_log2):
    kb_w = DSA_KEY_BLOCK
    rows = n_seq * ROW_TILE
    j = pl.program_id(1)
    nk = ((j + 1) * ROW_TILE + kb_w - 1) // kb_w
    nt = (((1,), (1,)), ((), ()))
    per = B_HEADS // B_KV_HEADS
    lane = lax.broadcasted_iota(I32, (1, kb_w), 1)
    qpos1 = j * ROW_TILE + lax.broadcasted_iota(I32, (ROW_TILE, 1), 0) - PAD_ROWS
    qpos = jnp.concatenate([qpos1] * n_seq, axis=0)
    lane_sum = lambda a: jnp.sum(a, axis=1, keepdims=True)

    def key_rows(kb):
        return pl.ds(pl.multiple_of(kb * kb_w, kb_w), kb_w)

    w_cols = []
    for b in range(n_seq):
        wi = wi_ref[b] * (IDX_HEADS ** -0.5)
        w_cols.append([jnp.broadcast_to(wi[:, h:h + 1], (ROW_TILE, LANES)) for h in range(IDX_HEADS)])

    def score_block(kb, carry):
        kpos = kb * kb_w + lane - PAD_ROWS
        for b in range(n_seq):
            ki = ki_ref[b, key_rows(kb), :]
            acc = [jnp.zeros((ROW_TILE, LANES), F32) for _ in range(kb_w // LANES)]
            for h in range(IDX_HEADS):
                d = lax.dot_general(qi_ref[b, :, h * LANES:(h + 1) * LANES], ki, nt, preferred_element_type=F32)
                for c in range(kb_w // LANES):
                    acc[c] = acc[c] + w_cols[b][h] * jnp.maximum(d[:, c * LANES:(c + 1) * LANES], 0.0)
            s = jnp.where(kpos <= qpos1, jnp.concatenate(acc, axis=1), NEG_BIG)
            s_ref[kb, b * ROW_TILE:(b + 1) * ROW_TILE, :] = jnp.where(kpos >= 0, s, -jnp.inf)
        return carry
    lax.fori_loop(0, nk, score_block, 0)

    def over_blocks(fn):
        def body(kb, a):
            f = fn(kb, s_ref[kb])
            for c in range(kb_w // LANES):
                a = a + f[:, c * LANES:(c + 1) * LANES]
            return a
        return lane_sum(lax.fori_loop(0, nk, body, jnp.zeros((rows, LANES), F32)))

    thr = _kth_largest(lambda t: over_blocks(lambda kb, s: jnp.where(s >= t, 1.0, 0.0)), n_sel, rows)
    need = float(n_sel) - over_blocks(lambda kb, s: jnp.where(s > thr, 1.0, 0.0))
    n_eq = over_blocks(lambda kb, s: jnp.where(s == thr, 1.0, 0.0))
    takes_all = qpos < n_sel
    c_all = jnp.full((rows, 1), (1 << n_cols_log2) - 1, I32)

    def eq_before(c):
        return over_blocks(lambda kb, s: jnp.where(s == thr, jnp.where(kb * kb_w + lane < c, 1.0, 0.0), 0.0))

    c_star = lax.cond(jnp.max(jnp.where(takes_all, 0.0, n_eq - need)) > 0.0,
                      lambda _: _tie_cut(eq_before, need, n_cols_log2), lambda _: c_all, 0)

    n_att = n_seq * B_KV_HEADS
    qs = [jnp.concatenate([qb_ref[a // B_KV_HEADS, :, h * LANES:(h + 1) * LANES]
                           for h in range((a % B_KV_HEADS) * per, (a % B_KV_HEADS + 1) * per)], axis=0)
          for a in range(n_att)]
    acc_ref[...] = jnp.zeros_like(acc_ref)

    def attend_block(kb, carry):
        s = s_ref[kb]
        col = kb * kb_w + lane
        kpos = col - PAD_ROWS
        picked = jnp.where(s > thr, 1.0, jnp.where(s == thr, jnp.where(col <= c_star, 1.0, 0.0), 0.0))
        picked = jnp.where(takes_all, 1.0, picked)
        valid = jnp.where(kpos >= 0, jnp.where(kpos <= qpos, picked, 0.0), 0.0)
        out = []
        for a in range(n_att):
            b = a // B_KV_HEADS
            m_old, l_old = carry[2 * a], carry[2 * a + 1]
            k_blk = kv_ref[b, key_rows(kb), :LANES]
            v_blk = kv_ref[b, key_rows(kb), LANES:]
            ok = valid[b * ROW_TILE:(b + 1) * ROW_TILE][None] > 0.5
            logits = lax.dot_general(qs[a], k_blk, nt, preferred_element_type=F32).reshape(per, ROW_TILE, kb_w)
            logits = jnp.where(ok, logits, NEG_BIG)
            m_new = jnp.maximum(m_old, jnp.max(logits, axis=-1, keepdims=True))
            alpha = jnp.exp(m_old - m_new)
            p = jnp.exp(logits - m_new)
            l_new = alpha * l_old + jnp.sum(p, axis=-1, keepdims=True)
            pv = jnp.dot(p.reshape(per * ROW_TILE, kb_w).astype(BF16), v_blk, preferred_element_type=F32)
            acc_ref[a] = acc_ref[a] * alpha.reshape(per * ROW_TILE, 1) + pv
            out += [m_new, l_new]
        return tuple(out)

    init = []
    for a in range(n_att):
        init += [jnp.full((per, ROW_TILE, 1), -jnp.inf, F32), jnp.zeros((per, ROW_TILE, 1), F32)]
    stats = lax.fori_loop(0, nk, attend_block, tuple(init))

    low = lane[:, :LANES] < B_HEAD_DIM
    for a in range(n_att):
        b, g = a // B_KV_HEADS, a % B_KV_HEADS
        o = acc_ref[a].reshape(per, ROW_TILE, LANES) / stats[2 * a + 1]
        for pair in range(per // 2):
            c = (g * per) // 2 + pair
            o_ref[b, :, c * LANES:(c + 1) * LANES] = _merge_head_pair(o[2 * pair], o[2 * pair + 1], g, low)


def _dsa_prompt(qbx, qix, h, kvb, kib, n_batch, n_chunks):
    t = n_chunks * ROW_TILE
    n_seq = 2 if n_batch % 2 == 0 else 1
    n_kb = -(-t // DSA_KEY_BLOCK)
    tk = n_kb * DSA_KEY_BLOCK
    seq3 = lambda a: a.reshape(n_batch, t, a.shape[-1])
    pad_keys = lambda a: jnp.pad(seq3(a), ((0, 0), (0, tk - t), (0, 0)))
    rowblk = lambda w, cb: pl.BlockSpec((n_seq, ROW_TILE, w), lambda b, j: (b, j, cb))
    out = pl.pallas_call(
        functools.partial(_dsa_prompt_kernel, n_seq=n_seq, n_sel=min(TOPK_MAX, (t - PAD_ROWS) // 4),
                          n_cols_log2=math.ceil(math.log2(tk))),
        grid=(n_batch // n_seq, n_chunks),
        in_specs=[rowblk(1024, 0), rowblk(1024, 0), rowblk(LANES, EV_WI // LANES),
                  pl.BlockSpec((n_seq, tk, 256), lambda b, j: (b, 0, 0)),
                  pl.BlockSpec((n_seq, tk, LANES), lambda b, j: (b, 0, 0))],
        out_specs=rowblk(512, 0),
        out_shape=jax.ShapeDtypeStruct((n_batch, t, B_WIDTH), F32),
        scratch_shapes=[pltpu.VMEM((n_kb, n_seq * ROW_TILE, DSA_KEY_BLOCK), F32),
                        pltpu.VMEM((n_seq * B_KV_HEADS, (B_HEADS // B_KV_HEADS) * ROW_TILE, LANES), F32)],
        compiler_params=_cparams(("parallel", "arbitrary")),
        name="dsa_prompt",
    )(seq3(qbx), seq3(qix), seq3(h), pad_keys(kvb), pad_keys(kib))
    return out.reshape(n_batch * t, B_WIDTH)


def _page_specs(n_pages, width, col_block):
    return [pl.BlockSpec((1, PAGE_SIZE, width), functools.partial(lambda i, pt, p: (pt[i, p], 0, col_block), p=p))
            for p in range(n_pages)]


def _dsa_sample_score_kernel(pt_ref, q_ref, w_ref, qx_ref, kx_ref, *refs, n_pages, n_cols):
    del pt_ref
    page_refs, o_ref = refs[:n_pages], refs[n_pages]
    nt = (((1,), (1,)), ((), ()))
    q = q_ref[0]
    w = w_ref[0] * (IDX_HEADS ** -0.5)
    pieces = []
    for p in range(n_pages):
        d = lax.dot_general(q, page_refs[p][0].astype(BF16), nt, preferred_element_type=F32)
        pieces.append(jnp.sum(w * jnp.maximum(d, 0.0), axis=0, keepdims=True))
    d_self = jnp.sum(qx_ref[0].astype(F32) * kx_ref[0].astype(F32), axis=1, keepdims=True)
    s_self = jnp.sum(w[:, :1] * jnp.maximum(d_self, 0.0), axis=0, keepdims=True)
    lane = lax.broadcasted_iota(I32, (1, LANES), 1)
    pieces.append(jnp.where(lane == 0, s_self, 0.0))
    pad = n_cols - (n_pages + 1) * LANES
    if pad:
        pieces.append(jnp.zeros((1, pad), F32))
    o_ref[0] = jnp.concatenate(pieces, axis=1)


def _dsa_sample_scores(page_table, q_is, w_ib, q_ix, k_ib, cache_kidx2, n_cols):
    db, n_pages = page_table.shape
    per_seq = lambda shape: pl.BlockSpec((1,) + shape, lambda i, pt: (i, 0, 0))
    grid_spec = pltpu.PrefetchScalarGridSpec(
        num_scalar_prefetch=1, grid=(db,),
        in_specs=[per_seq((IDX_HEADS, LANES)), per_seq((IDX_HEADS, LANES)), per_seq((IDX_HEADS, LANES)),
                  per_seq((1, LANES))] + _page_specs(n_pages, LANES, 0),
        out_specs=per_seq((1, n_cols)))
    return pl.pallas_call(
        functools.partial(_dsa_sample_score_kernel, n_pages=n_pages, n_cols=n_cols),
        grid_spec=grid_spec,
        out_shape=jax.ShapeDtypeStruct((db, 1, n_cols), F32),
        compiler_params=_cparams(("arbitrary",)),
        name="dsa_sample_scores",
    )(page_table, q_is, w_ib, q_ix, k_ib, *([cache_kidx2] * n_pages))


def _dsa_sample_select_kernel(s_ref, o_ref, *, n_keys, n_sel, n_cols_log2):
    cols = lax.broadcasted_iota(I32, (1, s_ref.shape[1]), 1)
    score = jnp.where(cols < n_keys, s_ref[...], -jnp.inf)
    o_ref[...] = _topk_select(score, cols, n_sel, n_cols_log2)


def _dsa_sample_select(scores, n_keys):
    db, n_cols = scores.shape
    return pl.pallas_call(
        functools.partial(_dsa_sample_select_kernel, n_keys=n_keys, n_sel=min(TOPK_MAX, n_keys // 4),
                          n_cols_log2=math.ceil(math.log2(n_cols))),
        out_shape=jax.ShapeDtypeStruct((db, n_cols), F32),
        compiler_params=pltpu.CompilerParams(vmem_limit_bytes=VMEM_LIMIT_BYTES),
        name="dsa_sample_select",
    )(scores)


def _dsa_sample_attend_kernel(pt_ref, q_ref, m_ref, kn_ref, *refs, n_pages):
    del pt_ref
    k_refs, v_refs, o_ref = refs[:n_pages], refs[n_pages:2 * n_pages], refs[2 * n_pages]
    nt = (((1,), (1,)), ((), ()))
    q = q_ref[0]
    mask = m_ref[0]
    logits = []
    for p in range(n_pages):
        l_p = lax.dot_general(q, k_refs[p][0].astype(BF16), nt, preferred_element_type=F32)
        logits.append(jnp.where(mask[:, p * LANES:(p + 1) * LANES] > 0.5, l_p, NEG_BIG))
    kn = kn_ref[0]
    k_new = kn[:, :LANES].astype(BF16).astype(F32)
    v_new = kn[:, LANES:].astype(BF16).astype(F32)
    l_self = jnp.sum(q.astype(F32) * k_new, axis=1, keepdims=True)
    l_self = jnp.where(mask[:, n_pages * LANES:n_pages * LANES + 1] > 0.5, l_self, NEG_BIG)
    mx = l_self
    for l_p in logits:
        mx = jnp.maximum(mx, jnp.max(l_p, axis=1, keepdims=True))
    p_self = jnp.exp(l_self - mx)
    den = p_self
    acc = p_self * v_new
    for p in range(n_pages):
        w_p = jnp.exp(logits[p] - mx)
        den = den + jnp.sum(w_p, axis=1, keepdims=True)
        acc = acc + jnp.dot(w_p.astype(BF16), v_refs[p][0].astype(BF16), preferred_element_type=F32)
    o_ref[0] = acc / den


def _dsa_sample_attend(page_table, q8, mask, kv_new, cache_k2, cache_v2, layer):
    db, n_pages = page_table.shape
    n_cols = mask.shape[-1]
    per_seq = lambda shape: pl.BlockSpec((1,) + shape, lambda i, pt: (i, 0, 0))
    grid_spec = pltpu.PrefetchScalarGridSpec(
        num_scalar_prefetch=1, grid=(db,),
        in_specs=[per_seq((B_HEADS, LANES)), per_seq((1, n_cols)), per_seq((1, 2 * LANES))]
        + _page_specs(n_pages, LANES, layer) + _page_specs(n_pages, LANES, layer),
        out_specs=per_seq((B_HEADS, LANES)))
    return pl.pallas_call(
        functools.partial(_dsa_sample_attend_kernel, n_pages=n_pages),
        grid_spec=grid_spec,
        out_shape=jax.ShapeDtypeStruct((db, B_HEADS, LANES), F32),
        compiler_params=_cparams(("arbitrary",)),
        name="dsa_sample_attend",
    )(page_table, q8, mask, kv_new, *([cache_k2] * n_pages), *([cache_v2] * n_pages))


def _even_out_kernel(oa_ref, ga_ref, ob_ref, x_ref, w_ref, ng_ref, lg_ref, lb_ref, o_ref, *, tm, seq_pad):
    ng = ng_ref[...]
    acc = jnp.dot(ob_ref[...].astype(BF16), w_ref[A_WIDTH:, :], preferred_element_type=F32)
    for h in range(A_HEADS):
        sl = slice(h * A_DV, (h + 1) * A_DV)
        oa = oa_ref[:, sl]
        oa = oa * lax.rsqrt(jnp.mean(oa * oa, axis=-1, keepdims=True) + RMS_EPS) * ng
        ga = ga_ref[:, sl]
        oa = oa * (ga * _sigmoid(ga))
        acc = acc + jnp.dot(oa.astype(BF16), w_ref[sl, :], preferred_element_type=F32)
    y = DN_ALPHA * x_ref[...] + acc
    o_ref[...] = _finish_rows(y, lg_ref[...], lb_ref[...], pl.program_id(0), tm, seq_pad)


def _even_out(oa, h, ob, x, w, ng, lg, lb, tm, seq_pad):
    m = x.shape[0]
    row = lambda w_, cb: pl.BlockSpec((tm, w_), lambda i: (i, cb))
    full = lambda a: pl.BlockSpec(a.shape, lambda i: (0,) * a.ndim)
    return pl.pallas_call(
        functools.partial(_even_out_kernel, tm=tm, seq_pad=seq_pad),
        grid=(m // tm,),
        in_specs=[row(A_WIDTH, 0), row(A_WIDTH, EV_GA // A_WIDTH), row(B_WIDTH, 0), row(D_MODEL, 0),
                  full(w), full(ng), full(lg), full(lb)],
        out_specs=row(D_MODEL, 0),
        out_shape=jax.ShapeDtypeStruct((m, D_MODEL), F32),
        compiler_params=_cparams(("parallel",)),
        name="even_out",
    )(oa, h, ob, x, w, ng, lg, lb)


RT_EXPERT0 = N_GROUPS


def _lane_argmax(v, lane):
    mx = jnp.max(v, axis=-1, keepdims=True)
    idx = jnp.min(jnp.where(v == mx, lane, float(LANES)), axis=-1, keepdims=True)
    return mx, idx


def _router_gates(x, wrh_ref, wrl_ref, rb_ref):
    xh = x.astype(BF16)
    xl = (x - xh.astype(F32)).astype(BF16)
    logits = (jnp.dot(xh, wrh_ref[...], preferred_element_type=F32)
              + jnp.dot(xl, wrh_ref[...], preferred_element_type=F32)
              + jnp.dot(xh, wrl_ref[...], preferred_element_type=F32)) + rb_ref[...]
    lane_i = lax.broadcasted_iota(I32, logits.shape, 1)
    lane = lane_i.astype(F32)
    neg_inf = -jnp.inf
    g_logits = jnp.where(lane_i < N_GROUPS, logits, neg_inf)
    g_max, g_idx = _lane_argmax(g_logits, lane)
    g_val = 1.0 / jnp.sum(jnp.exp(g_logits - g_max), axis=-1, keepdims=True)
    e_lane = lane_i - RT_EXPERT0
    lane_group = jnp.where(e_lane >= 0, e_lane >> 2, -1)
    lane_group = jnp.where(lane_i < RT_EXPERT0 + N_EXPERTS, lane_group, -1).astype(F32)
    e_logits = jnp.where(lane_group == g_idx, logits, neg_inf)
    e_max, first = _lane_argmax(e_logits, lane)
    p = jnp.exp(e_logits - e_max)
    p = p / jnp.sum(p, axis=-1, keepdims=True)
    p1 = jnp.sum(jnp.where(lane == first, p, 0.0), axis=-1, keepdims=True)
    rest = jnp.where(lane == first, neg_inf, jnp.where(lane_group == g_idx, p, neg_inf))
    p2, second = _lane_argmax(rest, lane)
    scale = g_val / (p1 + p2)
    return jnp.where(lane == first, p1 * scale, jnp.where(lane == second, p2 * scale, 0.0))


def _moe_kernel(x_ref, wrh_ref, wrl_ref, rb_ref, wgu_ref, wd_ref, lg_ref, lb_ref, o_ref, acc_ref, gate_ref,
                *, tm, seq_pad):
    e = pl.program_id(1)

    @pl.when(e == 0)
    def _():
        gate_ref[...] = _router_gates(x_ref[...], wrh_ref, wrl_ref, rb_ref)
        acc_ref[...] = jnp.zeros_like(acc_ref)

    lane = lax.broadcasted_iota(I32, (1, LANES), 1)
    gate = jnp.sum(jnp.where(lane == e + RT_EXPERT0, gate_ref[...], 0.0), axis=-1, keepdims=True)
    hgu = jnp.dot(x_ref[...].astype(BF16), wgu_ref[0], preferred_element_type=F32)
    hg = hgu[:, :EXPERT_FF]
    act = (hg * _sigmoid(hg)) * hgu[:, EXPERT_FF:] * gate
    acc_ref[...] += jnp.dot(act.astype(BF16), wd_ref[0], preferred_element_type=F32)

    @pl.when(e == N_EXPERTS - 1)
    def _():
        y = DN_ALPHA * x_ref[...] + acc_ref[...]
        o_ref[...] = _finish_rows(y, lg_ref[...], lb_ref[...], pl.program_id(0), tm, seq_pad)


def _moe(x, p, lg, lb, tm, seq_pad):
    m = x.shape[0]
    full = lambda a: pl.BlockSpec(a.shape, lambda i, e: (0,) * a.ndim)
    return pl.pallas_call(
        functools.partial(_moe_kernel, tm=tm, seq_pad=seq_pad),
        grid=(m // tm, N_EXPERTS),
        in_specs=[pl.BlockSpec((tm, D_MODEL), lambda i, e: (i, 0)),
                  full(p["wrh"]), full(p["wrl"]), full(p["rb"]),
                  pl.BlockSpec((1, D_MODEL, 2 * EXPERT_FF), lambda i, e: (e, 0, 0)),
                  pl.BlockSpec((1, EXPERT_FF, D_MODEL), lambda i, e: (e, 0, 0)),
                  full(lg), full(lb)],
        out_specs=pl.BlockSpec((tm, D_MODEL), lambda i, e: (i, 0)),
        out_shape=jax.ShapeDtypeStruct((m, D_MODEL), F32),
        scratch_shapes=[pltpu.VMEM((tm, D_MODEL), F32), pltpu.VMEM((tm, LANES), F32)],
        compiler_params=_cparams(("parallel", "arbitrary")),
        name="moe",
    )(x, p["wrh"], p["wrl"], p["rb"], p["wgu"], p["wd"], lg, lb)


def _pack_moe_params(rg_w, rg_b, re_w, re_b, w_gate, w_up, w_down):
    d = rg_w.shape[0]
    wr = jnp.concatenate([rg_w, re_w, jnp.zeros((d, LANES - N_GROUPS - N_EXPERTS), F32)], axis=1).astype(F32)
    wrh = wr.astype(BF16)
    wrl = (wr - wrh.astype(F32)).astype(BF16)
    rb = jnp.concatenate([rg_b, re_b, jnp.zeros((LANES - N_GROUPS - N_EXPERTS,), F32)]).reshape(1, LANES)
    return dict(wrh=wrh, wrl=wrl, rb=rb.astype(F32),
                wgu=jnp.concatenate([w_gate, w_up], axis=2).astype(BF16), wd=w_down.astype(BF16))


C_HIST = 32
D_HIST = 8


def _softplus(x):
    return jnp.maximum(x, 0.0) + jnp.log1p(jnp.exp(-jnp.abs(x)))


def _gelu_tanh(x):
    return 0.5 * x * (1.0 + jnp.tanh(math.sqrt(2.0 / math.pi) * (x + 0.044715 * (x * x * x))))


def _lru_gates(xc, wab_ref, ba, bx, lam):
    proj = jnp.dot(xc.astype(BF16), wab_ref[...], preferred_element_type=F32)
    r = _sigmoid(proj[:, :D_WIDTH] + ba)
    ig = _sigmoid(proj[:, D_WIDTH:] + bx)
    log_a = -LRU_C * r * _softplus(-lam)
    a = jnp.exp(log_a)
    th = jnp.tanh(log_a)
    drive = jnp.sqrt(jnp.maximum(-2.0 * th / (1.0 - th), 0.0)) * ig * xc
    return a, drive


def _odd_seq_kernel(ca_ref, cg_ref, dx_ref, dg_ref, cw_ref, cb_ref, lng_ref, lnb_ref, dw_ref, db_ref,
                    wab_ref, ba_ref, bx_ref, lam_ref,
                    y_ref, cst_ref, dst_ref, hst_ref, uext, dext, hc):
    c = pl.program_id(1)

    @pl.when(c == 0)
    def _():
        uext[:C_HIST, :] = jnp.zeros((C_HIST, C_WIDTH), F32)
        dext[:D_HIST, :] = jnp.zeros((D_HIST, D_WIDTH), F32)
        hc[...] = jnp.zeros_like(hc)

    uext[C_HIST:, :] = ca_ref[...] * _sigmoid(cg_ref[...])
    acc = jnp.zeros((ROW_TILE, C_WIDTH), F32)
    for j in range(C_CONV):
        off = C_HIST - (C_CONV - 1) + j
        acc = acc + cw_ref[j:j + 1, :] * uext[off:off + ROW_TILE, :]
    yc = _layer_norm(acc + cb_ref[...], lng_ref[...], lnb_ref[...])
    y_ref[:, :C_WIDTH] = yc * _sigmoid(yc)

    dext[D_HIST:, :] = dx_ref[...]
    xc = jnp.zeros((ROW_TILE, D_WIDTH), F32)
    for j in range(D_CONV):
        off = D_HIST - (D_CONV - 1) + j
        xc = xc + dw_ref[j:j + 1, :] * dext[off:off + ROW_TILE, :]
    xc = xc + db_ref[...]
    a, u = _lru_gates(xc, wab_ref, ba_ref[...], bx_ref[...], lam_ref[...])
    rows = lax.broadcasted_iota(I32, (ROW_TILE, 1), 0)
    u = jnp.where(jnp.logical_and(c == 0, rows < PAD_ROWS), 0.0, u)
    d = 1
    while d < ROW_TILE:
        head = rows < d
        a_prev = jnp.where(head, 1.0, pltpu.roll(a, d, axis=0))
        u_prev = jnp.where(head, 0.0, pltpu.roll(u, d, axis=0))
        u = u + a * u_prev
        a = a * a_prev
        d *= 2
    hs = a * hc[...] + u
    y_ref[:, C_WIDTH:] = hs * _gelu_tanh(dg_ref[...])

    hc[...] = hs[ROW_TILE - 1:ROW_TILE, :]
    uext[:C_HIST, :] = uext[ROW_TILE:ROW_TILE + C_HIST, :]
    dext[:D_HIST, :] = dext[ROW_TILE:ROW_TILE + D_HIST, :]

    @pl.when(c == pl.num_programs(1) - 1)
    def _():
        cst_ref[0] = uext[C_HIST + ROW_TILE - (C_CONV - 1):C_HIST + ROW_TILE, :]
        dst_ref[0] = dext[D_HIST + ROW_TILE - (D_CONV - 1):D_HIST + ROW_TILE, :]
        hst_ref[0] = hs[ROW_TILE - 1:ROW_TILE, :]


def _odd_seq(h, p, n_batch, n_chunks):
    m = h.shape[0]
    blk = lambda cb: pl.BlockSpec((ROW_TILE, 512), lambda b, c: (b * n_chunks + c, cb))
    full = lambda a: pl.BlockSpec(a.shape, lambda b, c: (0,) * a.ndim)
    params = [p["cw"], p["cb"], p["lng"], p["lnb"], p["dw"], p["db"], p["wab"], p["ba"], p["bx"], p["lam"]]
    state = lambda r: pl.BlockSpec((1, r, 512), lambda b, c: (b, 0, 0))
    return pl.pallas_call(
        _odd_seq_kernel,
        grid=(n_batch, n_chunks),
        in_specs=[blk(0), blk(1), blk(2), blk(3)] + [full(a) for a in params],
        out_specs=[pl.BlockSpec((ROW_TILE, 1024), lambda b, c: (b * n_chunks + c, 0)),
                   state(C_CONV - 1), state(D_CONV - 1), state(1)],
        out_shape=[jax.ShapeDtypeStruct((m, C_WIDTH + D_WIDTH), F32),
                   jax.ShapeDtypeStruct((n_batch, C_CONV - 1, C_WIDTH), F32),
                   jax.ShapeDtypeStruct((n_batch, D_CONV - 1, D_WIDTH), F32),
                   jax.ShapeDtypeStruct((n_batch, 1, D_WIDTH), F32)],
        scratch_shapes=[pltpu.VMEM((C_HIST + ROW_TILE, C_WIDTH), F32),
                        pltpu.VMEM((D_HIST + ROW_TILE, D_WIDTH), F32),
                        pltpu.VMEM((1, D_WIDTH), F32)],
        compiler_params=_cparams(("parallel", "arbitrary")),
        name="odd_seq",
    )(h, h, h, h, *params)


def _pack_odd_params(cw, cb, lng, lnb, dw, db, wa, ba, wx, bx, lam):
    def block_diag(w):
        out = jnp.zeros((D_WIDTH, D_WIDTH), w.dtype)
        for n in range(D_BLOCKS):
            out = out.at[n * D_BLOCK_W:(n + 1) * D_BLOCK_W, n * D_BLOCK_W:(n + 1) * D_BLOCK_W].set(w[n])
        return out
    row = lambda v: v.reshape(1, -1).astype(F32)
    return dict(cw=cw.astype(F32), cb=row(cb), lng=row(lng), lnb=row(lnb), dw=dw.astype(F32), db=row(db),
                wab=jnp.concatenate([block_diag(wa), block_diag(wx)], axis=1).astype(BF16),
                ba=row(ba), bx=row(bx), lam=row(lam))


def _odd_sample_kernel(ca_ref, cg_ref, dx_ref, dg_ref, cs_ref, ds_ref, h0_ref,
                       cw_ref, cb_ref, lng_ref, lnb_ref, dw_ref, db_ref, wab_ref, ba_ref, bx_ref, lam_ref,
                       y_ref, cso_ref, dso_ref, ho_ref):
    u = ca_ref[...] * _sigmoid(cg_ref[...])
    acc = cw_ref[C_CONV - 1:C_CONV, :] * u
    for j in range(C_CONV - 1):
        acc = acc + cw_ref[j:j + 1, :] * cs_ref[j]
        if j > 0:
            cso_ref[j - 1] = cs_ref[j]
    cso_ref[C_CONV - 2] = u
    yc = _layer_norm(acc + cb_ref[...], lng_ref[...], lnb_ref[...])
    y_ref[:, :C_WIDTH] = yc * _sigmoid(yc)

    dx = dx_ref[...]
    xc = dw_ref[D_CONV - 1:D_CONV, :] * dx
    for j in range(D_CONV - 1):
        xc = xc + dw_ref[j:j + 1, :] * ds_ref[j]
        if j > 0:
            dso_ref[j - 1] = ds_ref[j]
    dso_ref[D_CONV - 2] = dx
    xc = xc + db_ref[...]
    a, drive = _lru_gates(xc, wab_ref, ba_ref[...], bx_ref[...], lam_ref[...])
    h = a * h0_ref[...] + drive
    ho_ref[...] = h
    y_ref[:, C_WIDTH:] = h * _gelu_tanh(dg_ref[...])


def _odd_sample(h, cs_t, ds_t, h0, p):
    db = h.shape[0]
    params = [p["cw"], p["cb"], p["lng"], p["lnb"], p["dw"], p["db"], p["wab"], p["ba"], p["bx"], p["lam"]]
    full = lambda a: pl.BlockSpec(a.shape, lambda i: (0,) * a.ndim)
    blk = lambda cb: pl.BlockSpec((db, 512), lambda i: (0, cb))
    return pl.pallas_call(
        _odd_sample_kernel,
        grid=(1,),
        in_specs=[blk(0), blk(1), blk(2), blk(3), full(cs_t), full(ds_t), full(h0)] + [full(a) for a in params],
        out_specs=[pl.BlockSpec((db, 1024), lambda i: (0, 0)), full(cs_t), full(ds_t), full(h0)],
        out_shape=[jax.ShapeDtypeStruct((db, C_WIDTH + D_WIDTH), F32),
                   jax.ShapeDtypeStruct(cs_t.shape, F32), jax.ShapeDtypeStruct(ds_t.shape, F32),
                   jax.ShapeDtypeStruct(h0.shape, F32)],
        compiler_params=_cparams(("arbitrary",)),
        name="odd_sample",
    )(h, h, h, h, cs_t, ds_t, h0, *params)


def _mm_postnorm_kernel(a_ref, x_ref, w_ref, lg_ref, lb_ref, o_ref, *, tm, seq_pad):
    acc = jnp.dot(a_ref[...].astype(BF16), w_ref[...], preferred_element_type=F32)
    y = DN_ALPHA * x_ref[...] + acc
    o_ref[...] = _finish_rows(y, lg_ref[...], lb_ref[...], pl.program_id(0), tm, seq_pad)


def _mm_postnorm(a, x, w, lg, lb, tm, seq_pad):
    m = x.shape[0]
    row = lambda w_: pl.BlockSpec((tm, w_), lambda i: (i, 0))
    full = lambda arr: pl.BlockSpec(arr.shape, lambda i: (0,) * arr.ndim)
    return pl.pallas_call(
        functools.partial(_mm_postnorm_kernel, tm=tm, seq_pad=seq_pad),
        grid=(m // tm,),
        in_specs=[row(a.shape[1]), row(D_MODEL), full(w), full(lg), full(lb)],
        out_specs=row(D_MODEL),
        out_shape=jax.ShapeDtypeStruct((m, D_MODEL), F32),
        compiler_params=_cparams(("parallel",)),
        name="mm_postnorm",
    )(a, x, w, lg, lb)


def _row_tile(m):
    for tm in (512, 256, ROW_TILE):
        if m % tm == 0:
            return tm
    return m


def _moe_tile(m):
    for tm in (1024, 512, 256, ROW_TILE):
        if m % tm == 0:
            return tm
    return m


def _even_sample_attention(hs, qbx, qix, kvf, kib, cache_k2, cache_v2, cache_kidx2, page_table, layer):
    db, n_pages = page_table.shape
    past = n_pages * PAGE_SIZE
    n_keys = past + 1
    n_cols = -(-(n_keys) // LANES) * LANES
    qi = qix.reshape(db, IDX_HEADS, 2, IDX_DIM)
    qi = jnp.stack([qi[:, h, h % 2] for h in range(IDX_HEADS)], axis=1)
    zeros = jnp.zeros_like(qi)
    q_is = jnp.concatenate([qi, zeros] if layer == 0 else [zeros, qi], axis=-1)
    w_ib = jnp.broadcast_to(hs[:, EV_WI:EV_WI + IDX_HEADS, None], (db, IDX_HEADS, LANES))
    scores = _dsa_sample_scores(page_table, q_is, w_ib, qix.reshape(db, IDX_HEADS, LANES),
                                kib.reshape(db, 1, LANES), cache_kidx2, n_cols)
    mask = _dsa_sample_select(scores.reshape(db, n_cols), n_keys)
    o8 = _dsa_sample_attend(page_table, qbx.reshape(db, B_HEADS, LANES), mask.reshape(db, 1, n_cols),
                            kvf.reshape(db, 1, 2 * LANES), cache_k2, cache_v2, layer)
    per = B_HEADS // B_KV_HEADS
    halves = [o8[:, h, (h // per) * B_HEAD_DIM:(h // per + 1) * B_HEAD_DIM] for h in range(B_HEADS)]
    return jnp.concatenate(halves, axis=-1)


def kernel(x_prompt, x_sample, cache_k, cache_v, cache_kidx, state_hgrn, state_conv_c, state_conv_d, state_lru,
           page_table, meta_tokens, w_in_even, w_out_even, hgrn_lb_logits, hgrn_norm_g, w_in_odd, w_out_odd,
           conv_c_w, conv_c_b, conv_c_ln_g, conv_c_ln_b, conv_d_w, conv_d_b, lru_wa, lru_ba, lru_wx, lru_bx,
           lru_lambda, ln1_g, ln1_b, ln2_g, ln2_b, router_g_w, router_g_b, router_e_w, router_e_b,
           w_gate, w_up, w_down):
    bsz, seq, _ = x_prompt.shape
    dbsz, dseq, _ = x_sample.shape
    assert dseq == 1 and seq % ROW_TILE == 0 and dbsz % HS_TB == 0
    t_real = N_META + seq
    n_chunks = (PAD_ROWS + t_real) // ROW_TILE
    t_pad = n_chunks * ROW_TILE
    n_phys = cache_k.shape[0]
    past_len = page_table.shape[1] * PAGE_SIZE
    n_even = cache_k.shape[2]

    meta = jnp.broadcast_to(meta_tokens.astype(F32)[None], (bsz, N_META, D_MODEL))
    xp = jnp.concatenate([jnp.zeros((bsz, PAD_ROWS, D_MODEL), F32), meta, x_prompt.astype(F32)], axis=1)
    xp = xp.reshape(bsz * t_pad, D_MODEL)
    xs = x_sample.reshape(dbsz, D_MODEL).astype(F32)
    tm_p, tm_s = _row_tile(bsz * t_pad), _row_tile(dbsz)

    cos_p, sin_p = _rope_tables(np.maximum(np.arange(t_pad) - PAD_ROWS, 0))
    cos_p, sin_p = jnp.tile(jnp.asarray(cos_p), (bsz, 1)), jnp.tile(jnp.asarray(sin_p), (bsz, 1))
    cos_s, sin_s = _rope_tables(np.full((dbsz,), past_len))
    cos_s, sin_s = jnp.asarray(cos_s), jnp.asarray(sin_s)

    sm = jax.nn.softmax(hgrn_lb_logits.astype(F32), axis=0)
    lower_bounds = jnp.cumsum(sm, axis=0) - sm[0]

    cache_k2 = cache_k.reshape(n_phys, PAGE_SIZE, n_even * B_KV_WIDTH)
    cache_v2 = cache_v.reshape(n_phys, PAGE_SIZE, n_even * B_KV_WIDTH)
    cache_kidx2 = cache_kidx.reshape(n_phys, PAGE_SIZE, n_even * IDX_DIM)
    assert n_even * IDX_DIM == LANES and B_KV_WIDTH == LANES

    row2 = lambda v: v.reshape(1, -1).astype(F32)
    unpad = lambda a, w: a.reshape(bsz, t_pad, w)[:, PAD_ROWS:]
    kp, vp, ip, hp, cp, dp, lp = [], [], [], [], [], [], []
    ks, vs, iks, hsm, csm, dsm, lsm = [], [], [], [], [], [], []
    for layer in range(DEPTH):
        li = layer // 2
        lg1, lb1 = row2(ln1_g[layer]), row2(ln1_b[layer])
        if layer % 2 == 0:
            w_in = _pack_even_weight(w_in_even[li])
            w_out = w_out_even[li].astype(BF16)
            lb = lower_bounds[li].reshape(1, A_WIDTH)
            ng = row2(hgrn_norm_g[li])
            h = _matmul(xp, w_in, tm_p, 512)
            qbx, qix, kvf, kvb, kif, kib = _even_post(h, cos_p, sin_p, tm_p)
            oa, s_p = _hgrn_prompt(h, lb, bsz, n_chunks)
            ob = _dsa_prompt(qbx, qix, h, kvb, kib, bsz, n_chunks)
            xp = _even_out(oa, h, ob, xp, w_out, ng, lg1, lb1, tm_p, t_pad)
            kp.append(unpad(kvf[:, :LANES], LANES).reshape(bsz, t_real, B_KV_HEADS, B_HEAD_DIM))
            vp.append(unpad(kvf[:, LANES:], LANES).reshape(bsz, t_real, B_KV_HEADS, B_HEAD_DIM))
            ip.append(unpad(kif[:, :IDX_DIM], IDX_DIM))
            hp.append(s_p)
            h = _matmul(xs, w_in, tm_s, 512)
            qbx, qix, kvf, kvb, kif, kib = _even_post(h, cos_s, sin_s, tm_s)
            oa, s_s = _hgrn_sample(h, lb, state_hgrn, li)
            ob = _even_sample_attention(h, qbx, qix, kvf, kib, cache_k2, cache_v2, cache_kidx2, page_table, li)
            xs = _even_out(oa, h, ob, xs, w_out, ng, lg1, lb1, tm_s, None)
            ks.append(kvf[:, :LANES].reshape(dbsz, 1, B_KV_HEADS, B_HEAD_DIM))
            vs.append(kvf[:, LANES:].reshape(dbsz, 1, B_KV_HEADS, B_HEAD_DIM))
            iks.append(kif[:, :IDX_DIM].reshape(dbsz, 1, IDX_DIM))
            hsm.append(s_s)
        else:
            w_in = w_in_odd[li].astype(BF16)
            w_out = w_out_odd[li].astype(BF16)
            p = _pack_odd_params(conv_c_w[li], conv_c_b[li], conv_c_ln_g[li], conv_c_ln_b[li], conv_d_w[li],
                                 conv_d_b[li], lru_wa[li], lru_ba[li], lru_wx[li], lru_bx[li], lru_lambda[li])
            h = _matmul(xp, w_in, tm_p, 512)
            y, c_p, d_p, h_p = _odd_seq(h, p, bsz, n_chunks)
            xp = _mm_postnorm(y, xp, w_out, lg1, lb1, tm_p, t_pad)
            cp.append(c_p); dp.append(d_p); lp.append(h_p[:, 0])
            h = _matmul(xs, w_in, tm_s, 512)
            y, c_s, d_s, h_s = _odd_sample(h, jnp.swapaxes(state_conv_c[:, li], 0, 1).astype(F32),
                                           jnp.swapaxes(state_conv_d[:, li], 0, 1).astype(F32),
                                           state_lru[:, li].astype(F32), p)
            xs = _mm_postnorm(y, xs, w_out, lg1, lb1, tm_s, None)
            csm.append(jnp.swapaxes(c_s, 0, 1)); dsm.append(jnp.swapaxes(d_s, 0, 1)); lsm.append(h_s)
        mp = _pack_moe_params(router_g_w[layer], router_g_b[layer], router_e_w[layer], router_e_b[layer],
                              w_gate[layer], w_up[layer], w_down[layer])
        lg2, lb2 = row2(ln2_g[layer]), row2(ln2_b[layer])
        xp = _moe(xp, mp, lg2, lb2, _moe_tile(bsz * t_pad), t_pad)
        xs = _moe(xs, mp, lg2, lb2, _moe_tile(dbsz), None)

    y_prompt = xp.reshape(bsz, t_pad, D_MODEL)[:, PAD_ROWS + N_META:]
    y_sample = xs.reshape(dbsz, 1, D_MODEL)
    return (y_prompt, y_sample, jnp.stack(kp, axis=2), jnp.stack(vp, axis=2), jnp.stack(ip, axis=2),
            jnp.stack(hp, axis=1), jnp.stack(cp, axis=1), jnp.stack(dp, axis=1), jnp.stack(lp, axis=1),
            jnp.stack(ks, axis=2), jnp.stack(vs, axis=2), jnp.stack(iks, axis=2), jnp.stack(hsm, axis=1),
            jnp.stack(csm, axis=1), jnp.stack(dsm, axis=1), jnp.stack(lsm, axis=1))
```

```python
import functools
import math

import numpy as np
import jax
import jax.numpy as jnp
from jax import lax
from jax.experimental import pallas as pl
from jax.experimental.pallas import tpu as pltpu

F32 = jnp.float32
BF16 = jnp.bfloat16
I32 = jnp.int32

D_MODEL = 1024
DEPTH = 4
PAGE_SIZE = 128
N_META = 16
A_HEADS = 4
A_DK = 128
A_DV = 128
A_WIDTH = A_HEADS * A_DV
B_HEADS = 8
B_KV_HEADS = 2
B_HEAD_DIM = 64
B_WIDTH = B_HEADS * B_HEAD_DIM
B_KV_WIDTH = B_KV_HEADS * B_HEAD_DIM
IDX_HEADS = 8
IDX_DIM = 64
TOPK_MAX = 256
ROPE_THETA = 10000.0
C_WIDTH = 512
C_CONV = 31
D_WIDTH = 512
D_CONV = 4
D_BLOCKS = 8
D_BLOCK_W = D_WIDTH // D_BLOCKS
LRU_C = 8.0
N_GROUPS = 4
EXPERTS_PER_GROUP = 4
N_EXPERTS = N_GROUPS * EXPERTS_PER_GROUP
TOP_E = 2
EXPERT_FF = 256
DN_ALPHA = (2 * DEPTH) ** 0.25
LN_EPS = 1e-5
RMS_EPS = 1e-6
NEG_BIG = -1e30
LB_TINY = 1e-30

LANES = 128
SUBLANES = 8
ROW_TILE = 128
VMEM_LIMIT_BYTES = 56 * 1024 * 1024

EV_QA, EV_FA, EV_IA, EV_GA = 0, 512, 1024, 1536
EV_QB, EV_QI, EV_KB, EV_VB, EV_KI, EV_WI = 2048, 2560, 3072, 3200, 3328, 3456
EV_COLS = 3584
PAD_ROWS = ROW_TILE - N_META
INT_MIN = -2147483648


def _cparams(sem):
    return pltpu.CompilerParams(dimension_semantics=sem, vmem_limit_bytes=VMEM_LIMIT_BYTES)


def _sigmoid(x):
    return 1.0 / (1.0 + jnp.exp(-x))


def _log_sigmoid(x):
    return jnp.minimum(x, 0.0) - jnp.log1p(jnp.exp(-jnp.abs(x)))


def _layer_norm(y, g, b):
    mu = jnp.mean(y, axis=-1, keepdims=True)
    d = y - mu
    var = jnp.mean(d * d, axis=-1, keepdims=True)
    return d * lax.rsqrt(var + LN_EPS) * g + b


def _keep_rows(tile, tm, seq_pad):
    r = (tile * tm + lax.broadcasted_iota(I32, (tm, 1), 0)).astype(F32)
    pos = r - jnp.floor((r + 0.5) * (1.0 / seq_pad)) * seq_pad
    return jnp.where(pos >= PAD_ROWS, 1.0, 0.0)


def _finish_rows(y, g, b, tile, tm, seq_pad):
    out = _layer_norm(y, g, b)
    return out if seq_pad is None else out * _keep_rows(tile, tm, seq_pad)


def _mm_kernel(x_ref, w_ref, o_ref, *, tn):
    x = x_ref[...].astype(BF16)
    for c in range(0, w_ref.shape[1], tn):
        o_ref[:, c:c + tn] = jnp.dot(x, w_ref[:, c:c + tn], preferred_element_type=F32)


def _matmul(x, w, tm, tn):
    m, k = x.shape
    n = w.shape[1]
    assert m % tm == 0 and n % tn == 0
    return pl.pallas_call(
        functools.partial(_mm_kernel, tn=tn),
        grid=(m // tm,),
        in_specs=[pl.BlockSpec((tm, k), lambda i: (i, 0)),
                  pl.BlockSpec((k, n), lambda i: (0, 0))],
        out_specs=pl.BlockSpec((tm, n), lambda i: (i, 0)),
        out_shape=jax.ShapeDtypeStruct((m, n), F32),
        compiler_params=_cparams(("parallel",)),
        name="matmul",
    )(x, w)


HG_LEVELS = 7


def _hgrn_mid_rows(b_ref, level):
    half = 1 << level
    blk = half * 2
    if blk >= SUBLANES:
        pieces = []
        for start in range(0, ROW_TILE, blk):
            m = start + half - 1
            pieces.append(jnp.broadcast_to(b_ref[m:m + 1, :], (blk, LANES)))
        return pieces[0] if len(pieces) == 1 else jnp.concatenate(pieces, axis=0)
    sub = lax.broadcasted_iota(I32, (SUBLANES, LANES), 0)
    pieces = []
    for start in range(0, ROW_TILE, SUBLANES):
        acc = None
        for off in range(SUBLANES - blk, -1, -blk):
            m = start + off + half - 1
            row = jnp.broadcast_to(b_ref[m:m + 1, :], (SUBLANES, LANES))
            acc = row if acc is None else jnp.where(sub < off + blk, row, acc)
        pieces.append(acc)
    return jnp.concatenate(pieces, axis=0)


def _hgrn_kernel(qa_ref, fa_ref, ia_ref, lb_ref, o_ref, s_ref, st_ref, b_ref):
    c = pl.program_id(2)

    @pl.when(c == 0)
    def _():
        st_ref[...] = jnp.zeros_like(st_ref)

    lb = lb_ref[...]
    z = fa_ref[...]
    qa = qa_ref[...]
    v = ia_ref[...]
    la = jnp.log(jnp.maximum(lb, LB_TINY))
    lc = jnp.log1p(-lb) + _log_sigmoid(z)
    log_f = jnp.maximum(la, lc) + jnp.log1p(jnp.exp(-jnp.abs(la - lc)))
    k = (1.0 - lb) * _sigmoid(-z)
    q = qa * _sigmoid(qa)

    rows = lax.broadcasted_iota(I32, (ROW_TILE, 1), 0)
    cols = lax.broadcasted_iota(I32, (1, ROW_TILE), 1)

    b = log_f
    for lv in range(HG_LEVELS):
        d = 1 << lv
        b = b + jnp.where(rows >= d, pltpu.roll(b, d, axis=0), 0.0)
    b_ref[...] = b

    scores = jnp.zeros((ROW_TILE, ROW_TILE), F32)
    nt = (((1,), (1,)), ((), ()))
    for lv in range(HG_LEVELS):
        e = jnp.exp(-jnp.abs(b - _hgrn_mid_rows(b_ref, lv)))
        upper = (rows & (1 << lv)) != 0
        qd = jnp.where(upper, q * e, 0.0).astype(BF16)
        kd = jnp.where(upper, 0.0, k * e).astype(BF16)
        s_l = lax.dot_general(qd, kd, nt, preferred_element_type=F32)
        if lv + 1 < HG_LEVELS:
            same = (rows >> (lv + 1)) == (cols >> (lv + 1))
            scores = scores + jnp.where(same, s_l, 0.0)
        else:
            scores = scores + s_l
    diag = jnp.sum(q * k, axis=1, keepdims=True)
    scores = jnp.where(rows == cols, diag, scores)

    st = st_ref[...]
    v_bf = v.astype(BF16)
    o = jnp.dot(scores.astype(BF16), v_bf, preferred_element_type=F32)
    o = o + lax.dot_general((q * jnp.exp(b)).astype(BF16), st.astype(BF16), nt,
                            preferred_element_type=F32)
    o_ref[...] = o

    b_last = b_ref[ROW_TILE - 1:ROW_TILE, :]
    kdl = (k * jnp.exp(b_last - b)).astype(BF16)
    st_new = st * jnp.exp(b_last) + lax.dot_general(v_bf, kdl, (((0,), (0,)), ((), ())),
                                                    preferred_element_type=F32)
    st_ref[...] = st_new

    @pl.when(c == pl.num_programs(2) - 1)
    def _():
        s_ref[0, 0] = st_new.T


def _hgrn_prompt(h, lb, n_batch, n_chunks):
    m = h.shape[0]
    col = lambda base: (lambda b, hh, c: (b * n_chunks + c, base // LANES + hh))
    return pl.pallas_call(
        _hgrn_kernel,
        grid=(n_batch, A_HEADS, n_chunks),
        in_specs=[pl.BlockSpec((ROW_TILE, LANES), col(EV_QA)),
                  pl.BlockSpec((ROW_TILE, LANES), col(EV_FA)),
                  pl.BlockSpec((ROW_TILE, LANES), col(EV_IA)),
                  pl.BlockSpec((1, LANES), lambda b, hh, c: (0, hh))],
        out_specs=[pl.BlockSpec((ROW_TILE, LANES), lambda b, hh, c: (b * n_chunks + c, hh)),
                   pl.BlockSpec((1, 1, A_DK, A_DV), lambda b, hh, c: (b, hh, 0, 0))],
        out_shape=[jax.ShapeDtypeStruct((m, A_WIDTH), F32),
                   jax.ShapeDtypeStruct((n_batch, A_HEADS, A_DK, A_DV), F32)],
        scratch_shapes=[pltpu.VMEM((A_DV, A_DK), F32), pltpu.VMEM((ROW_TILE, LANES), F32)],
        compiler_params=_cparams(("parallel", "parallel", "arbitrary")),
        name="hgrn_prompt",
    )(h, h, h, lb)


HS_TB = 8


def _hgrn_sample_kernel(qa_ref, fa_ref, ia_ref, lb_ref, s_ref, o_ref, so_ref):
    lb = lb_ref[...]
    z = fa_ref[...]
    la = jnp.log(jnp.maximum(lb, LB_TINY))
    lc = jnp.log1p(-lb) + _log_sigmoid(z)
    f = jnp.exp(jnp.maximum(la, lc) + jnp.log1p(jnp.exp(-jnp.abs(la - lc))))
    k = (1.0 - lb) * _sigmoid(-z)
    qa = qa_ref[...]
    q = qa * _sigmoid(qa)
    v = ia_ref[...]
    pad = jnp.zeros((LANES - 3 * HS_TB, LANES), F32)
    for h in range(A_HEADS):
        sl = slice(h * A_DK, (h + 1) * A_DK)
        cols = jnp.concatenate([q[:, sl], k[:, sl], f[:, sl], pad], axis=0).T
        for b in range(HS_TB):
            qc = cols[:, b:b + 1]
            kc = cols[:, HS_TB + b:HS_TB + b + 1]
            fc = cols[:, 2 * HS_TB + b:2 * HS_TB + b + 1]
            s_new = fc * s_ref[b, 0, h] + kc * v[b:b + 1, sl]
            so_ref[b, h] = s_new
            o_ref[b:b + 1, sl] = jnp.sum(qc * s_new, axis=0, keepdims=True)


def _hgrn_sample(h, lb, state, layer):
    db = h.shape[0]
    row = lambda cb: pl.BlockSpec((HS_TB, A_WIDTH), lambda i: (i, cb))
    return pl.pallas_call(
        _hgrn_sample_kernel,
        grid=(db // HS_TB,),
        in_specs=[row(EV_QA // A_WIDTH), row(EV_FA // A_WIDTH), row(EV_IA // A_WIDTH),
                  pl.BlockSpec((1, A_WIDTH), lambda i: (0, 0)),
                  pl.BlockSpec((HS_TB, 1, A_HEADS, A_DK, A_DV), lambda i: (i, layer, 0, 0, 0))],
        out_specs=[row(0), pl.BlockSpec((HS_TB, A_HEADS, A_DK, A_DV), lambda i: (i, 0, 0, 0))],
        out_shape=[jax.ShapeDtypeStruct((db, A_WIDTH), F32),
                   jax.ShapeDtypeStruct((db, A_HEADS, A_DK, A_DV), F32)],
        compiler_params=_cparams(("parallel",)),
        name="hgrn_sample",
    )(h, h, h, lb, state)


def _pack_even_weight(w):
    parts = (A_HEADS * A_DK, A_HEADS * A_DK, A_WIDTH, A_WIDTH, B_WIDTH, B_KV_WIDTH, B_KV_WIDTH,
             IDX_HEADS * IDX_DIM, IDX_DIM, IDX_HEADS)
    cuts = [int(c) for c in np.cumsum(parts)[:-1]]
    qa, fa, ia, ga, qb, kb, vb, qi, ki, wi = jnp.split(w, cuts, axis=1)
    zeros = lambda n: jnp.zeros((w.shape[0], n), w.dtype)
    out = jnp.concatenate([qa, fa, ia, ga, qb, qi, kb, vb, ki, zeros(LANES - IDX_DIM),
                           wi, zeros(LANES - IDX_HEADS)], axis=1)
    assert out.shape[1] == EV_COLS
    return out.astype(BF16)

def _rope_tables(pos):
    half = B_HEAD_DIM // 2
    lane = np.arange(LANES)
    inv = ROPE_THETA ** (-(lane % half).astype(np.float64) / half)
    ang = np.asarray(pos, np.float64)[:, None] * inv[None, :]
    sign = np.where((lane % B_HEAD_DIM) < half, -1.0, 1.0)
    return np.cos(ang).astype(np.float32), (np.sin(ang) * sign[None, :]).astype(np.float32)


def _rope128(x, cos, sin, first_half):
    rot = jnp.where(first_half, pltpu.roll(x, LANES - 32, axis=1), pltpu.roll(x, 32, axis=1))
    return x * cos + rot * sin


def _even_post_kernel(qb_ref, qi_ref, kv_ref, ki_ref, cos_ref, sin_ref,
                      qbx_ref, qix_ref, kvf_ref, kvb_ref, kif_ref, kib_ref):
    cos = cos_ref[...]
    sin = sin_ref[...]
    lane = lax.broadcasted_iota(I32, (1, LANES), 1)
    first_half = (lane % B_HEAD_DIM) < (B_HEAD_DIM // 2)
    low = lane < B_HEAD_DIM
    scale = B_HEAD_DIM ** -0.5
    for pair in range(B_HEADS // 2):
        sl = slice(pair * LANES, (pair + 1) * LANES)
        qb = _rope128(qb_ref[:, sl], cos, sin, first_half) * scale
        qi = _rope128(qi_ref[:, sl], cos, sin, first_half) * scale
        qb_sw = pltpu.roll(qb, B_HEAD_DIM, axis=1)
        group = (2 * pair) // (B_HEADS // B_KV_HEADS)
        for sub in range(2):
            h = 2 * pair + sub
            src = qb if sub == group else qb_sw
            keep = low if group == 0 else jnp.logical_not(low)
            qbx_ref[:, h * LANES:(h + 1) * LANES] = jnp.where(keep, src, 0.0).astype(BF16)
            keep_i = low if sub == 0 else jnp.logical_not(low)
            qix_ref[:, h * LANES:(h + 1) * LANES] = jnp.where(keep_i, qi, 0.0).astype(BF16)
    k = _rope128(kv_ref[:, :LANES], cos, sin, first_half)
    v = kv_ref[:, LANES:]
    kvf_ref[:, :LANES] = k
    kvf_ref[:, LANES:] = v
    kvb_ref[:, :LANES] = k.astype(BF16)
    kvb_ref[:, LANES:] = v.astype(BF16)
    ki = _rope128(ki_ref[...], cos, sin, first_half)
    kif_ref[...] = ki
    kib_ref[...] = (ki + pltpu.roll(ki, B_HEAD_DIM, axis=1)).astype(BF16)


def _even_post(h, cos, sin, tm):
    m = h.shape[0]
    row = lambda w, cb: pl.BlockSpec((tm, w), lambda i: (i, cb))
    return pl.pallas_call(
        _even_post_kernel,
        grid=(m // tm,),
        in_specs=[row(512, EV_QB // 512), row(512, EV_QI // 512), row(256, EV_KB // 256),
                  row(LANES, EV_KI // LANES), row(LANES, 0), row(LANES, 0)],
        out_specs=[row(1024, 0), row(1024, 0), row(256, 0), row(256, 0), row(LANES, 0), row(LANES, 0)],
        out_shape=[jax.ShapeDtypeStruct((m, 1024), BF16), jax.ShapeDtypeStruct((m, 1024), BF16),
                   jax.ShapeDtypeStruct((m, 256), F32), jax.ShapeDtypeStruct((m, 256), BF16),
                   jax.ShapeDtypeStruct((m, LANES), F32), jax.ShapeDtypeStruct((m, LANES), BF16)],
        compiler_params=_cparams(("parallel",)),
        name="even_post",
    )(h, h, h, h, cos, sin)


def _ordinal_to_f32(k):
    return pltpu.bitcast(jnp.where(k < 0, k ^ 0x7FFFFFFF, k), F32)


def _kth_largest(count_ge, n_sel, shape):
    n_f = float(n_sel)
    base = jnp.where(count_ge(jnp.zeros(shape, F32)) >= n_f, 0, INT_MIN).astype(I32)

    def bit_step(i, base):
        cand = base | (jnp.int32(1) << (30 - i))
        return jnp.where(count_ge(_ordinal_to_f32(cand)) >= n_f, cand, base)

    return _ordinal_to_f32(lax.fori_loop(0, 31, bit_step, base))


def _tie_cut(count_eq_before, need, n_cols_log2):
    def step(i, c):
        cand = c + (jnp.int32(1) << (n_cols_log2 - 1 - i))
        return jnp.where(count_eq_before(cand) < need, cand, c)
    return lax.fori_loop(0, n_cols_log2, step, jnp.zeros(need.shape, I32))


def _topk_select(score, cols, n_sel, n_cols_log2):
    rows = score.shape[0]
    count = lambda m: jnp.sum(m, axis=1, keepdims=True)
    thr = _kth_largest(lambda t: count(jnp.where(score >= t, 1.0, 0.0)), n_sel, (rows, 1))
    gt = jnp.where(score > thr, 1.0, 0.0)
    eq = jnp.where(score == thr, 1.0, 0.0)
    need = float(n_sel) - count(gt)
    c_all = jnp.full((rows, 1), (1 << n_cols_log2) - 1, I32)
    c_star = lax.cond(jnp.max(count(eq) - need) > 0.0,
                      lambda _: _tie_cut(lambda c: count(jnp.where(cols < c, eq, 0.0)), need, n_cols_log2),
                      lambda _: c_all, 0)
    return gt + jnp.where(cols <= c_star, eq, 0.0)


def _merge_head_pair(a, b, group, low):
    if group == 0:
        return jnp.where(low, a, pltpu.roll(b, B_HEAD_DIM, axis=1))
    return jnp.where(low, pltpu.roll(a, B_HEAD_DIM, axis=1), b)


DSA_KEY_BLOCK = 512
DSA_ATT_BLOCK = 512


def _transpose_bf16(x):
    return x.astype(F32).T.astype(BF16)


def _fold_keys(x, reduce):
    keys, nq = x.shape
    return reduce(reduce(x.reshape(keys // 64, 64, nq), axis=0), axis=0, keepdims=True)


def _dsa_prompt_kernel(qb_ref, qi_ref, wi_ref, kv_ref, ki_ref, o_ref, s_ref, acc_ref, qit_ref, qbt_ref,
                       *, n_seq, n_sel, n_cols_log2):
    kb_w = DSA_KEY_BLOCK
    nq = n_seq * ROW_TILE
    j = pl.program_id(1)
    nk = ((j + 1) * ROW_TILE + kb_w - 1) // kb_w
    per = B_HEADS // B_KV_HEADS
    key_in_block = lax.broadcasted_iota(I32, (kb_w, 1), 0)
    qpos1 = j * ROW_TILE + lax.broadcasted_iota(I32, (1, ROW_TILE), 1) - PAD_ROWS
    qpos = jnp.concatenate([qpos1] * n_seq, axis=1)
    key_sum = lambda a: jnp.sum(a, axis=0, keepdims=True)

    def key_rows(kb):
        return pl.ds(pl.multiple_of(kb * kb_w, kb_w), kb_w)

    w_rows = []
    for b in range(n_seq):
        w_rows.append(wi_ref[b].T * (IDX_HEADS ** -0.5))
        for h in range(IDX_HEADS):
            qit_ref[b, h * LANES:(h + 1) * LANES, :] = _transpose_bf16(qi_ref[b, :, h * LANES:(h + 1) * LANES])
        for h in range(B_HEADS):
            a, i = b * B_KV_HEADS + h // per, h % per
            qbt_ref[a, :, i * ROW_TILE:(i + 1) * ROW_TILE] = _transpose_bf16(qb_ref[b, :, h * LANES:(h + 1) * LANES])

    def score_block(kb, carry):
        kpos = kb * kb_w + key_in_block - PAD_ROWS
        for b in range(n_seq):
            ki = ki_ref[b, key_rows(kb), :]
            acc = jnp.zeros((kb_w, ROW_TILE), F32)
            for h in range(IDX_HEADS):
                d = jnp.dot(ki, qit_ref[b, h * LANES:(h + 1) * LANES, :], preferred_element_type=F32)
                acc = acc + w_rows[b][h:h + 1, :] * jnp.maximum(d, 0.0)
            s = jnp.where(kpos <= qpos1, acc, NEG_BIG)
            s_ref[kb, :, b * ROW_TILE:(b + 1) * ROW_TILE] = jnp.where(kpos >= 0, s, -jnp.inf)
        return carry
    lax.fori_loop(0, nk, score_block, 0)

    n_chain = 8

    def over_blocks(fn):
        def body(kb, a):
            f = fn(kb, s_ref[kb]).reshape(kb_w // (n_chain * SUBLANES), n_chain * SUBLANES, nq)
            return a + jnp.sum(f, axis=0)
        return key_sum(lax.fori_loop(0, nk, body, jnp.zeros((n_chain * SUBLANES, nq), F32)))

    thr = _kth_largest(lambda t: over_blocks(lambda kb, s: jnp.where(s >= t, 1.0, 0.0)), n_sel, (1, nq))
    need = float(n_sel) - over_blocks(lambda kb, s: jnp.where(s > thr, 1.0, 0.0))
    n_eq = over_blocks(lambda kb, s: jnp.where(s == thr, 1.0, 0.0))
    takes_all = qpos < n_sel
    c_all = jnp.full((1, nq), (1 << n_cols_log2) - 1, I32)

    def eq_before(c):
        return over_blocks(lambda kb, s: jnp.where(s == thr, jnp.where(kb * kb_w + key_in_block < c, 1.0, 0.0), 0.0))

    c_star = lax.cond(jnp.max(jnp.where(takes_all, 0.0, n_eq - need)) > 0.0,
                      lambda _: _tie_cut(eq_before, need, n_cols_log2), lambda _: c_all, 0)

    n_att = n_seq * B_KV_HEADS
    acc_ref[...] = jnp.zeros_like(acc_ref)
    sub_w = DSA_ATT_BLOCK
    key_in_sub = key_in_block[:sub_w]

    def attend_block(kb, carry):
        carry = list(carry)
        for sub in range(kb_w // sub_w):
            lo = sub * sub_w
            s = s_ref[kb, lo:lo + sub_w, :]
            kidx = kb * kb_w + lo + key_in_sub
            kpos = kidx - PAD_ROWS
            picked = jnp.where(s > thr, 1.0, jnp.where(s == thr, jnp.where(kidx <= c_star, 1.0, 0.0), 0.0))
            picked = jnp.where(takes_all, 1.0, picked)
            valid = jnp.where(kpos >= 0, jnp.where(kpos <= qpos, picked, 0.0), 0.0)
            sub_rows = pl.ds(pl.multiple_of(kb * kb_w + lo, sub_w), sub_w)
            for b in range(n_seq):
                ok = jnp.concatenate([valid[:, b * ROW_TILE:(b + 1) * ROW_TILE]] * per, axis=1) > 0.5
                v_t = _transpose_bf16(kv_ref[b, sub_rows, LANES:])
                k_blk = kv_ref[b, sub_rows, :LANES]
                for g in range(B_KV_HEADS):
                    a = b * B_KV_HEADS + g
                    m_old, l_old = carry[2 * a], carry[2 * a + 1]
                    logits = jnp.dot(k_blk, qbt_ref[a], preferred_element_type=F32)
                    logits = jnp.where(ok, logits, NEG_BIG)
                    m_new = jnp.maximum(m_old, _fold_keys(logits, jnp.max))
                    alpha = jnp.exp(m_old - m_new)
                    p = jnp.exp(logits - m_new)
                    carry[2 * a] = m_new
                    carry[2 * a + 1] = alpha * l_old + _fold_keys(p, jnp.sum)
                    acc_ref[a] = acc_ref[a] * alpha + jnp.dot(v_t, p.astype(BF16), preferred_element_type=F32)
        return tuple(carry)

    init = []
    for a in range(n_att):
        init += [jnp.full((1, per * ROW_TILE), -jnp.inf, F32), jnp.zeros((1, per * ROW_TILE), F32)]
    stats = lax.fori_loop(0, nk, attend_block, tuple(init))

    low = lax.broadcasted_iota(I32, (1, LANES), 1) < B_HEAD_DIM
    for a in range(n_att):
        b, g = a // B_KV_HEADS, a % B_KV_HEADS
        o_t = acc_ref[a] / stats[2 * a + 1]
        o = [o_t[:, i * ROW_TILE:(i + 1) * ROW_TILE].T for i in range(per)]
        for pair in range(per // 2):
            c = (g * per) // 2 + pair
            o_ref[b, :, c * LANES:(c + 1) * LANES] = _merge_head_pair(o[2 * pair], o[2 * pair + 1], g, low)


def _dsa_prompt(qbx, qix, h, kvb, kib, n_batch, n_chunks):
    t = n_chunks * ROW_TILE
    n_seq = 2 if n_batch % 2 == 0 else 1
    n_kb = -(-t // DSA_KEY_BLOCK)
    tk = n_kb * DSA_KEY_BLOCK
    seq3 = lambda a: a.reshape(n_batch, t, a.shape[-1])
    pad_keys = lambda a: jnp.pad(seq3(a), ((0, 0), (0, tk - t), (0, 0)))
    rowblk = lambda w, cb: pl.BlockSpec((n_seq, ROW_TILE, w), lambda b, j: (b, j, cb))
    out = pl.pallas_call(
        functools.partial(_dsa_prompt_kernel, n_seq=n_seq, n_sel=min(TOPK_MAX, (t - PAD_ROWS) // 4),
                          n_cols_log2=math.ceil(math.log2(tk))),
        grid=(n_batch // n_seq, n_chunks),
        in_specs=[rowblk(1024, 0), rowblk(1024, 0), rowblk(LANES, EV_WI // LANES),
                  pl.BlockSpec((n_seq, tk, 256), lambda b, j: (b, 0, 0)),
                  pl.BlockSpec((n_seq, tk, LANES), lambda b, j: (b, 0, 0))],
        out_specs=rowblk(512, 0),
        out_shape=jax.ShapeDtypeStruct((n_batch, t, B_WIDTH), F32),
        scratch_shapes=[pltpu.VMEM((n_kb, DSA_KEY_BLOCK, n_seq * ROW_TILE), F32),
                        pltpu.VMEM((n_seq * B_KV_HEADS, LANES, (B_HEADS // B_KV_HEADS) * ROW_TILE), F32),
                        pltpu.VMEM((n_seq, IDX_HEADS * LANES, ROW_TILE), BF16),
                        pltpu.VMEM((n_seq * B_KV_HEADS, LANES, (B_HEADS // B_KV_HEADS) * ROW_TILE), BF16)],
        compiler_params=_cparams(("parallel", "arbitrary")),
        name="dsa_prompt",
    )(seq3(qbx), seq3(qix), seq3(h), pad_keys(kvb), pad_keys(kib))
    return out.reshape(n_batch * t, B_WIDTH)


def _page_specs(n_pages, width, col_block):
    return [pl.BlockSpec((1, PAGE_SIZE, width), functools.partial(lambda i, pt, p: (pt[i, p], 0, col_block), p=p))
            for p in range(n_pages)]


def _dsa_sample_score_kernel(pt_ref, q_ref, w_ref, qx_ref, kx_ref, *refs, n_pages, n_cols):
    del pt_ref
    page_refs, o_ref = refs[:n_pages], refs[n_pages]
    nt = (((1,), (1,)), ((), ()))
    q = q_ref[0]
    w = w_ref[0] * (IDX_HEADS ** -0.5)
    pieces = []
    for p in range(n_pages):
        d = lax.dot_general(q, page_refs[p][0].astype(BF16), nt, preferred_element_type=F32)
        pieces.append(jnp.sum(w * jnp.maximum(d, 0.0), axis=0, keepdims=True))
    d_self = jnp.sum(qx_ref[0].astype(F32) * kx_ref[0].astype(F32), axis=1, keepdims=True)
    s_self = jnp.sum(w[:, :1] * jnp.maximum(d_self, 0.0), axis=0, keepdims=True)
    lane = lax.broadcasted_iota(I32, (1, LANES), 1)
    pieces.append(jnp.where(lane == 0, s_self, 0.0))
    pad = n_cols - (n_pages + 1) * LANES
    if pad:
        pieces.append(jnp.zeros((1, pad), F32))
    o_ref[0] = jnp.concatenate(pieces, axis=1)


def _dsa_sample_scores(page_table, q_is, w_ib, q_ix, k_ib, cache_kidx2, n_cols):
    db, n_pages = page_table.shape
    per_seq = lambda shape: pl.BlockSpec((1,) + shape, lambda i, pt: (i, 0, 0))
    grid_spec = pltpu.PrefetchScalarGridSpec(
        num_scalar_prefetch=1, grid=(db,),
        in_specs=[per_seq((IDX_HEADS, LANES)), per_seq((IDX_HEADS, LANES)), per_seq((IDX_HEADS, LANES)),
                  per_seq((1, LANES))] + _page_specs(n_pages, LANES, 0),
        out_specs=per_seq((1, n_cols)))
    return pl.pallas_call(
        functools.partial(_dsa_sample_score_kernel, n_pages=n_pages, n_cols=n_cols),
        grid_spec=grid_spec,
        out_shape=jax.ShapeDtypeStruct((db, 1, n_cols), F32),
        compiler_params=_cparams(("arbitrary",)),
        name="dsa_sample_scores",
    )(page_table, q_is, w_ib, q_ix, k_ib, *([cache_kidx2] * n_pages))


def _dsa_sample_select_kernel(s_ref, o_ref, *, n_keys, n_sel, n_cols_log2):
    cols = lax.broadcasted_iota(I32, (1, s_ref.shape[1]), 1)
    score = jnp.where(cols < n_keys, s_ref[...], -jnp.inf)
    o_ref[...] = _topk_select(score, cols, n_sel, n_cols_log2)


def _dsa_sample_select(scores, n_keys):
    db, n_cols = scores.shape
    return pl.pallas_call(
        functools.partial(_dsa_sample_select_kernel, n_keys=n_keys, n_sel=min(TOPK_MAX, n_keys // 4),
                          n_cols_log2=math.ceil(math.log2(n_cols))),
        out_shape=jax.ShapeDtypeStruct((db, n_cols), F32),
        compiler_params=pltpu.CompilerParams(vmem_limit_bytes=VMEM_LIMIT_BYTES),
        name="dsa_sample_select",
    )(scores)


def _dsa_sample_attend_kernel(pt_ref, q_ref, m_ref, kn_ref, *refs, n_pages):
    del pt_ref
    k_refs, v_refs, o_ref = refs[:n_pages], refs[n_pages:2 * n_pages], refs[2 * n_pages]
    nt = (((1,), (1,)), ((), ()))
    q = q_ref[0]
    mask = m_ref[0]
    logits = []
    for p in range(n_pages):
        l_p = lax.dot_general(q, k_refs[p][0].astype(BF16), nt, preferred_element_type=F32)
        logits.append(jnp.where(mask[:, p * LANES:(p + 1) * LANES] > 0.5, l_p, NEG_BIG))
    kn = kn_ref[0]
    k_new = kn[:, :LANES].astype(BF16).astype(F32)
    v_new = kn[:, LANES:].astype(BF16).astype(F32)
    l_self = jnp.sum(q.astype(F32) * k_new, axis=1, keepdims=True)
    l_self = jnp.where(mask[:, n_pages * LANES:n_pages * LANES + 1] > 0.5, l_self, NEG_BIG)
    mx = l_self
    for l_p in logits:
        mx = jnp.maximum(mx, jnp.max(l_p, axis=1, keepdims=True))
    p_self = jnp.exp(l_self - mx)
    den = p_self
    acc = p_self * v_new
    for p in range(n_pages):
        w_p = jnp.exp(logits[p] - mx)
        den = den + jnp.sum(w_p, axis=1, keepdims=True)
        acc = acc + jnp.dot(w_p.astype(BF16), v_refs[p][0].astype(BF16), preferred_element_type=F32)
    o_ref[0] = acc / den


def _dsa_sample_attend(page_table, q8, mask, kv_new, cache_k2, cache_v2, layer):
    db, n_pages = page_table.shape
    n_cols = mask.shape[-1]
    per_seq = lambda shape: pl.BlockSpec((1,) + shape, lambda i, pt: (i, 0, 0))
    grid_spec = pltpu.PrefetchScalarGridSpec(
        num_scalar_prefetch=1, grid=(db,),
        in_specs=[per_seq((B_HEADS, LANES)), per_seq((1, n_cols)), per_seq((1, 2 * LANES))]
        + _page_specs(n_pages, LANES, layer) + _page_specs(n_pages, LANES, layer),
        out_specs=per_seq((B_HEADS, LANES)))
    return pl.pallas_call(
        functools.partial(_dsa_sample_attend_kernel, n_pages=n_pages),
        grid_spec=grid_spec,
        out_shape=jax.ShapeDtypeStruct((db, B_HEADS, LANES), F32),
        compiler_params=_cparams(("arbitrary",)),
        name="dsa_sample_attend",
    )(page_table, q8, mask, kv_new, *([cache_k2] * n_pages), *([cache_v2] * n_pages))


def _even_out_kernel(oa_ref, ga_ref, ob_ref, x_ref, w_ref, ng_ref, lg_ref, lb_ref, o_ref, *, tm, seq_pad):
    ng = ng_ref[...]
    acc = jnp.dot(ob_ref[...].astype(BF16), w_ref[A_WIDTH:, :], preferred_element_type=F32)
    for h in range(A_HEADS):
        sl = slice(h * A_DV, (h + 1) * A_DV)
        oa = oa_ref[:, sl]
        oa = oa * lax.rsqrt(jnp.mean(oa * oa, axis=-1, keepdims=True) + RMS_EPS) * ng
        ga = ga_ref[:, sl]
        oa = oa * (ga * _sigmoid(ga))
        acc = acc + jnp.dot(oa.astype(BF16), w_ref[sl, :], preferred_element_type=F32)
    y = DN_ALPHA * x_ref[...] + acc
    o_ref[...] = _finish_rows(y, lg_ref[...], lb_ref[...], pl.program_id(0), tm, seq_pad)


def _even_out(oa, h, ob, x, w, ng, lg, lb, tm, seq_pad):
    m = x.shape[0]
    row = lambda w_, cb: pl.BlockSpec((tm, w_), lambda i: (i, cb))
    full = lambda a: pl.BlockSpec(a.shape, lambda i: (0,) * a.ndim)
    return pl.pallas_call(
        functools.partial(_even_out_kernel, tm=tm, seq_pad=seq_pad),
        grid=(m // tm,),
        in_specs=[row(A_WIDTH, 0), row(A_WIDTH, EV_GA // A_WIDTH), row(B_WIDTH, 0), row(D_MODEL, 0),
                  full(w), full(ng), full(lg), full(lb)],
        out_specs=row(D_MODEL, 0),
        out_shape=jax.ShapeDtypeStruct((m, D_MODEL), F32),
        compiler_params=_cparams(("parallel",)),
        name="even_out",
    )(oa, h, ob, x, w, ng, lg, lb)


RT_EXPERT0 = N_GROUPS


def _lane_argmax(v, lane):
    mx = jnp.max(v, axis=-1, keepdims=True)
    idx = jnp.min(jnp.where(v == mx, lane, float(LANES)), axis=-1, keepdims=True)
    return mx, idx


def _router_gates(x, wrh_ref, wrl_ref, rb_ref):
    xh = x.astype(BF16)
    xl = (x - xh.astype(F32)).astype(BF16)
    logits = (jnp.dot(xh, wrh_ref[...], preferred_element_type=F32)
              + jnp.dot(xl, wrh_ref[...], preferred_element_type=F32)
              + jnp.dot(xh, wrl_ref[...], preferred_element_type=F32)) + rb_ref[...]
    lane_i = lax.broadcasted_iota(I32, logits.shape, 1)
    lane = lane_i.astype(F32)
    neg_inf = -jnp.inf
    g_logits = jnp.where(lane_i < N_GROUPS, logits, neg_inf)
    g_max, g_idx = _lane_argmax(g_logits, lane)
    g_val = 1.0 / jnp.sum(jnp.exp(g_logits - g_max), axis=-1, keepdims=True)
    e_lane = lane_i - RT_EXPERT0
    lane_group = jnp.where(e_lane >= 0, e_lane >> 2, -1)
    lane_group = jnp.where(lane_i < RT_EXPERT0 + N_EXPERTS, lane_group, -1).astype(F32)
    e_logits = jnp.where(lane_group == g_idx, logits, neg_inf)
    e_max, first = _lane_argmax(e_logits, lane)
    p = jnp.exp(e_logits - e_max)
    p = p / jnp.sum(p, axis=-1, keepdims=True)
    p1 = jnp.sum(jnp.where(lane == first, p, 0.0), axis=-1, keepdims=True)
    rest = jnp.where(lane == first, neg_inf, jnp.where(lane_group == g_idx, p, neg_inf))
    p2, second = _lane_argmax(rest, lane)
    scale = g_val / (p1 + p2)
    return jnp.where(lane == first, p1 * scale, jnp.where(lane == second, p2 * scale, 0.0))


def _moe_kernel(x_ref, wrh_ref, wrl_ref, rb_ref, wgu_ref, wd_ref, lg_ref, lb_ref, o_ref, acc_ref, gate_ref,
                *, tm, seq_pad):
    e = pl.program_id(1)

    @pl.when(e == 0)
    def _():
        gate_ref[...] = _router_gates(x_ref[...], wrh_ref, wrl_ref, rb_ref)
        acc_ref[...] = jnp.zeros_like(acc_ref)

    lane = lax.broadcasted_iota(I32, (1, LANES), 1)
    gate = jnp.sum(jnp.where(lane == e + RT_EXPERT0, gate_ref[...], 0.0), axis=-1, keepdims=True)
    hgu = jnp.dot(x_ref[...].astype(BF16), wgu_ref[0], preferred_element_type=F32)
    hg = hgu[:, :EXPERT_FF]
    act = (hg * _sigmoid(hg)) * hgu[:, EXPERT_FF:] * gate
    acc_ref[...] += jnp.dot(act.astype(BF16), wd_ref[0], preferred_element_type=F32)

    @pl.when(e == N_EXPERTS - 1)
    def _():
        y = DN_ALPHA * x_ref[...] + acc_ref[...]
        o_ref[...] = _finish_rows(y, lg_ref[...], lb_ref[...], pl.program_id(0), tm, seq_pad)


def _moe(x, p, lg, lb, tm, seq_pad):
    m = x.shape[0]
    full = lambda a: pl.BlockSpec(a.shape, lambda i, e: (0,) * a.ndim)
    return pl.pallas_call(
        functools.partial(_moe_kernel, tm=tm, seq_pad=seq_pad),
        grid=(m // tm, N_EXPERTS),
        in_specs=[pl.BlockSpec((tm, D_MODEL), lambda i, e: (i, 0)),
                  full(p["wrh"]), full(p["wrl"]), full(p["rb"]),
                  pl.BlockSpec((1, D_MODEL, 2 * EXPERT_FF), lambda i, e: (e, 0, 0)),
                  pl.BlockSpec((1, EXPERT_FF, D_MODEL), lambda i, e: (e, 0, 0)),
                  full(lg), full(lb)],
        out_specs=pl.BlockSpec((tm, D_MODEL), lambda i, e: (i, 0)),
        out_shape=jax.ShapeDtypeStruct((m, D_MODEL), F32),
        scratch_shapes=[pltpu.VMEM((tm, D_MODEL), F32), pltpu.VMEM((tm, LANES), F32)],
        compiler_params=_cparams(("parallel", "arbitrary")),
        name="moe",
    )(x, p["wrh"], p["wrl"], p["rb"], p["wgu"], p["wd"], lg, lb)


def _pack_moe_params(rg_w, rg_b, re_w, re_b, w_gate, w_up, w_down):
    d = rg_w.shape[0]
    wr = jnp.concatenate([rg_w, re_w, jnp.zeros((d, LANES - N_GROUPS - N_EXPERTS), F32)], axis=1).astype(F32)
    wrh = wr.astype(BF16)
    wrl = (wr - wrh.astype(F32)).astype(BF16)
    rb = jnp.concatenate([rg_b, re_b, jnp.zeros((LANES - N_GROUPS - N_EXPERTS,), F32)]).reshape(1, LANES)
    return dict(wrh=wrh, wrl=wrl, rb=rb.astype(F32),
                wgu=jnp.concatenate([w_gate, w_up], axis=2).astype(BF16), wd=w_down.astype(BF16))


C_HIST = 32
D_HIST = 8


def _softplus(x):
    return jnp.maximum(x, 0.0) + jnp.log1p(jnp.exp(-jnp.abs(x)))


def _gelu_tanh(x):
    return 0.5 * x * (1.0 + jnp.tanh(math.sqrt(2.0 / math.pi) * (x + 0.044715 * (x * x * x))))


def _lru_gates(xc, wab_ref, ba, bx, lam):
    proj = jnp.dot(xc.astype(BF16), wab_ref[...], preferred_element_type=F32)
    r = _sigmoid(proj[:, :D_WIDTH] + ba)
    ig = _sigmoid(proj[:, D_WIDTH:] + bx)
    log_a = -LRU_C * r * _softplus(-lam)
    a = jnp.exp(log_a)
    th = jnp.tanh(log_a)
    drive = jnp.sqrt(jnp.maximum(-2.0 * th / (1.0 - th), 0.0)) * ig * xc
    return a, drive


def _odd_seq_kernel(ca_ref, cg_ref, dx_ref, dg_ref, cw_ref, cb_ref, lng_ref, lnb_ref, dw_ref, db_ref,
                    wab_ref, ba_ref, bx_ref, lam_ref,
                    y_ref, cst_ref, dst_ref, hst_ref, uext, dext, hc):
    c = pl.program_id(1)

    @pl.when(c == 0)
    def _():
        uext[:C_HIST, :] = jnp.zeros((C_HIST, C_WIDTH), F32)
        dext[:D_HIST, :] = jnp.zeros((D_HIST, D_WIDTH), F32)
        hc[...] = jnp.zeros_like(hc)

    uext[C_HIST:, :] = ca_ref[...] * _sigmoid(cg_ref[...])
    acc = jnp.zeros((ROW_TILE, C_WIDTH), F32)
    for j in range(C_CONV):
        off = C_HIST - (C_CONV - 1) + j
        acc = acc + cw_ref[j:j + 1, :] * uext[off:off + ROW_TILE, :]
    yc = _layer_norm(acc + cb_ref[...], lng_ref[...], lnb_ref[...])
    y_ref[:, :C_WIDTH] = yc * _sigmoid(yc)

    dext[D_HIST:, :] = dx_ref[...]
    xc = jnp.zeros((ROW_TILE, D_WIDTH), F32)
    for j in range(D_CONV):
        off = D_HIST - (D_CONV - 1) + j
        xc = xc + dw_ref[j:j + 1, :] * dext[off:off + ROW_TILE, :]
    xc = xc + db_ref[...]
    a, u = _lru_gates(xc, wab_ref, ba_ref[...], bx_ref[...], lam_ref[...])
    rows = lax.broadcasted_iota(I32, (ROW_TILE, 1), 0)
    u = jnp.where(jnp.logical_and(c == 0, rows < PAD_ROWS), 0.0, u)
    d = 1
    while d < ROW_TILE:
        head = rows < d
        a_prev = jnp.where(head, 1.0, pltpu.roll(a, d, axis=0))
        u_prev = jnp.where(head, 0.0, pltpu.roll(u, d, axis=0))
        u = u + a * u_prev
        a = a * a_prev
        d *= 2
    hs = a * hc[...] + u
    y_ref[:, C_WIDTH:] = hs * _gelu_tanh(dg_ref[...])

    hc[...] = hs[ROW_TILE - 1:ROW_TILE, :]
    uext[:C_HIST, :] = uext[ROW_TILE:ROW_TILE + C_HIST, :]
    dext[:D_HIST, :] = dext[ROW_TILE:ROW_TILE + D_HIST, :]

    @pl.when(c == pl.num_programs(1) - 1)
    def _():
        cst_ref[0] = uext[C_HIST + ROW_TILE - (C_CONV - 1):C_HIST + ROW_TILE, :]
        dst_ref[0] = dext[D_HIST + ROW_TILE - (D_CONV - 1):D_HIST + ROW_TILE, :]
        hst_ref[0] = hs[ROW_TILE - 1:ROW_TILE, :]


def _odd_seq(h, p, n_batch, n_chunks):
    m = h.shape[0]
    blk = lambda cb: pl.BlockSpec((ROW_TILE, 512), lambda b, c: (b * n_chunks + c, cb))
    full = lambda a: pl.BlockSpec(a.shape, lambda b, c: (0,) * a.ndim)
    params = [p["cw"], p["cb"], p["lng"], p["lnb"], p["dw"], p["db"], p["wab"], p["ba"], p["bx"], p["lam"]]
    state = lambda r: pl.BlockSpec((1, r, 512), lambda b, c: (b, 0, 0))
    return pl.pallas_call(
        _odd_seq_kernel,
        grid=(n_batch, n_chunks),
        in_specs=[blk(0), blk(1), blk(2), blk(3)] + [full(a) for a in params],
        out_specs=[pl.BlockSpec((ROW_TILE, 1024), lambda b, c: (b * n_chunks + c, 0)),
                   state(C_CONV - 1), state(D_CONV - 1), state(1)],
        out_shape=[jax.ShapeDtypeStruct((m, C_WIDTH + D_WIDTH), F32),
                   jax.ShapeDtypeStruct((n_batch, C_CONV - 1, C_WIDTH), F32),
                   jax.ShapeDtypeStruct((n_batch, D_CONV - 1, D_WIDTH), F32),
                   jax.ShapeDtypeStruct((n_batch, 1, D_WIDTH), F32)],
        scratch_shapes=[pltpu.VMEM((C_HIST + ROW_TILE, C_WIDTH), F32),
                        pltpu.VMEM((D_HIST + ROW_TILE, D_WIDTH), F32),
                        pltpu.VMEM((1, D_WIDTH), F32)],
        compiler_params=_cparams(("parallel", "arbitrary")),
        name="odd_seq",
    )(h, h, h, h, *params)


def _pack_odd_params(cw, cb, lng, lnb, dw, db, wa, ba, wx, bx, lam):
    def block_diag(w):
        out = jnp.zeros((D_WIDTH, D_WIDTH), w.dtype)
        for n in range(D_BLOCKS):
            out = out.at[n * D_BLOCK_W:(n + 1) * D_BLOCK_W, n * D_BLOCK_W:(n + 1) * D_BLOCK_W].set(w[n])
        return out
    row = lambda v: v.reshape(1, -1).astype(F32)
    return dict(cw=cw.astype(F32), cb=row(cb), lng=row(lng), lnb=row(lnb), dw=dw.astype(F32), db=row(db),
                wab=jnp.concatenate([block_diag(wa), block_diag(wx)], axis=1).astype(BF16),
                ba=row(ba), bx=row(bx), lam=row(lam))


def _odd_sample_kernel(ca_ref, cg_ref, dx_ref, dg_ref, cs_ref, ds_ref, h0_ref,
                       cw_ref, cb_ref, lng_ref, lnb_ref, dw_ref, db_ref, wab_ref, ba_ref, bx_ref, lam_ref,
                       y_ref, cso_ref, dso_ref, ho_ref):
    u = ca_ref[...] * _sigmoid(cg_ref[...])
    acc = cw_ref[C_CONV - 1:C_CONV, :] * u
    for j in range(C_CONV - 1):
        acc = acc + cw_ref[j:j + 1, :] * cs_ref[j]
        if j > 0:
            cso_ref[j - 1] = cs_ref[j]
    cso_ref[C_CONV - 2] = u
    yc = _layer_norm(acc + cb_ref[...], lng_ref[...], lnb_ref[...])
    y_ref[:, :C_WIDTH] = yc * _sigmoid(yc)

    dx = dx_ref[...]
    xc = dw_ref[D_CONV - 1:D_CONV, :] * dx
    for j in range(D_CONV - 1):
        xc = xc + dw_ref[j:j + 1, :] * ds_ref[j]
        if j > 0:
            dso_ref[j - 1] = ds_ref[j]
    dso_ref[D_CONV - 2] = dx
    xc = xc + db_ref[...]
    a, drive = _lru_gates(xc, wab_ref, ba_ref[...], bx_ref[...], lam_ref[...])
    h = a * h0_ref[...] + drive
    ho_ref[...] = h
    y_ref[:, C_WIDTH:] = h * _gelu_tanh(dg_ref[...])


def _odd_sample(h, cs_t, ds_t, h0, p):
    db = h.shape[0]
    params = [p["cw"], p["cb"], p["lng"], p["lnb"], p["dw"], p["db"], p["wab"], p["ba"], p["bx"], p["lam"]]
    full = lambda a: pl.BlockSpec(a.shape, lambda i: (0,) * a.ndim)
    blk = lambda cb: pl.BlockSpec((db, 512), lambda i: (0, cb))
    return pl.pallas_call(
        _odd_sample_kernel,
        grid=(1,),
        in_specs=[blk(0), blk(1), blk(2), blk(3), full(cs_t), full(ds_t), full(h0)] + [full(a) for a in params],
        out_specs=[pl.BlockSpec((db, 1024), lambda i: (0, 0)), full(cs_t), full(ds_t), full(h0)],
        out_shape=[jax.ShapeDtypeStruct((db, C_WIDTH + D_WIDTH), F32),
                   jax.ShapeDtypeStruct(cs_t.shape, F32), jax.ShapeDtypeStruct(ds_t.shape, F32),
                   jax.ShapeDtypeStruct(h0.shape, F32)],
        compiler_params=_cparams(("arbitrary",)),
        name="odd_sample",
    )(h, h, h, h, cs_t, ds_t, h0, *params)


def _mm_postnorm_kernel(a_ref, x_ref, w_ref, lg_ref, lb_ref, o_ref, *, tm, seq_pad):
    acc = jnp.dot(a_ref[...].astype(BF16), w_ref[...], preferred_element_type=F32)
    y = DN_ALPHA * x_ref[...] + acc
    o_ref[...] = _finish_rows(y, lg_ref[...], lb_ref[...], pl.program_id(0), tm, seq_pad)


def _mm_postnorm(a, x, w, lg, lb, tm, seq_pad):
    m = x.shape[0]
    row = lambda w_: pl.BlockSpec((tm, w_), lambda i: (i, 0))
    full = lambda arr: pl.BlockSpec(arr.shape, lambda i: (0,) * arr.ndim)
    return pl.pallas_call(
        functools.partial(_mm_postnorm_kernel, tm=tm, seq_pad=seq_pad),
        grid=(m // tm,),
        in_specs=[row(a.shape[1]), row(D_MODEL), full(w), full(lg), full(lb)],
        out_specs=row(D_MODEL),
        out_shape=jax.ShapeDtypeStruct((m, D_MODEL), F32),
        compiler_params=_cparams(("parallel",)),
        name="mm_postnorm",
    )(a, x, w, lg, lb)


def _row_tile(m):
    for tm in (512, 256, ROW_TILE):
        if m % tm == 0:
            return tm
    return m


def _moe_tile(m):
    for tm in (1024, 512, 256, ROW_TILE):
        if m % tm == 0:
            return tm
    return m


def _even_sample_attention(hs, qbx, qix, kvf, kib, cache_k2, cache_v2, cache_kidx2, page_table, layer):
    db, n_pages = page_table.shape
    past = n_pages * PAGE_SIZE
    n_keys = past + 1
    n_cols = -(-(n_keys) // LANES) * LANES
    qi = qix.reshape(db, IDX_HEADS, 2, IDX_DIM)
    qi = jnp.stack([qi[:, h, h % 2] for h in range(IDX_HEADS)], axis=1)
    zeros = jnp.zeros_like(qi)
    q_is = jnp.concatenate([qi, zeros] if layer == 0 else [zeros, qi], axis=-1)
    w_ib = jnp.broadcast_to(hs[:, EV_WI:EV_WI + IDX_HEADS, None], (db, IDX_HEADS, LANES))
    scores = _dsa_sample_scores(page_table, q_is, w_ib, qix.reshape(db, IDX_HEADS, LANES),
                                kib.reshape(db, 1, LANES), cache_kidx2, n_cols)
    mask = _dsa_sample_select(scores.reshape(db, n_cols), n_keys)
    o8 = _dsa_sample_attend(page_table, qbx.reshape(db, B_HEADS, LANES), mask.reshape(db, 1, n_cols),
                            kvf.reshape(db, 1, 2 * LANES), cache_k2, cache_v2, layer)
    per = B_HEADS // B_KV_HEADS
    halves = [o8[:, h, (h // per) * B_HEAD_DIM:(h // per + 1) * B_HEAD_DIM] for h in range(B_HEADS)]
    return jnp.concatenate(halves, axis=-1)


def kernel(x_prompt, x_sample, cache_k, cache_v, cache_kidx, state_hgrn, state_conv_c, state_conv_d, state_lru,
           page_table, meta_tokens, w_in_even, w_out_even, hgrn_lb_logits, hgrn_norm_g, w_in_odd, w_out_odd,
           conv_c_w, conv_c_b, conv_c_ln_g, conv_c_ln_b, conv_d_w, conv_d_b, lru_wa, lru_ba, lru_wx, lru_bx,
           lru_lambda, ln1_g, ln1_b, ln2_g, ln2_b, router_g_w, router_g_b, router_e_w, router_e_b,
           w_gate, w_up, w_down):
    bsz, seq, _ = x_prompt.shape
    dbsz, dseq, _ = x_sample.shape
    assert dseq == 1 and seq % ROW_TILE == 0 and dbsz % HS_TB == 0
    t_real = N_META + seq
    n_chunks = (PAD_ROWS + t_real) // ROW_TILE
    t_pad = n_chunks * ROW_TILE
    n_phys = cache_k.shape[0]
    past_len = page_table.shape[1] * PAGE_SIZE
    n_even = cache_k.shape[2]

    meta = jnp.broadcast_to(meta_tokens.astype(F32)[None], (bsz, N_META, D_MODEL))
    xp = jnp.concatenate([jnp.zeros((bsz, PAD_ROWS, D_MODEL), F32), meta, x_prompt.astype(F32)], axis=1)
    xp = xp.reshape(bsz * t_pad, D_MODEL)
    xs = x_sample.reshape(dbsz, D_MODEL).astype(F32)
    tm_p, tm_s = _row_tile(bsz * t_pad), _row_tile(dbsz)

    cos_p, sin_p = _rope_tables(np.maximum(np.arange(t_pad) - PAD_ROWS, 0))
    cos_p, sin_p = jnp.tile(jnp.asarray(cos_p), (bsz, 1)), jnp.tile(jnp.asarray(sin_p), (bsz, 1))
    cos_s, sin_s = _rope_tables(np.full((dbsz,), past_len))
    cos_s, sin_s = jnp.asarray(cos_s), jnp.asarray(sin_s)

    sm = jax.nn.softmax(hgrn_lb_logits.astype(F32), axis=0)
    lower_bounds = jnp.cumsum(sm, axis=0) - sm[0]

    cache_k2 = cache_k.reshape(n_phys, PAGE_SIZE, n_even * B_KV_WIDTH)
    cache_v2 = cache_v.reshape(n_phys, PAGE_SIZE, n_even * B_KV_WIDTH)
    cache_kidx2 = cache_kidx.reshape(n_phys, PAGE_SIZE, n_even * IDX_DIM)
    assert n_even * IDX_DIM == LANES and B_KV_WIDTH == LANES

    row2 = lambda v: v.reshape(1, -1).astype(F32)
    unpad = lambda a, w: a.reshape(bsz, t_pad, w)[:, PAD_ROWS:]
    kp, vp, ip, hp, cp, dp, lp = [], [], [], [], [], [], []
    ks, vs, iks, hsm, csm, dsm, lsm = [], [], [], [], [], [], []
    for layer in range(DEPTH):
        li = layer // 2
        lg1, lb1 = row2(ln1_g[layer]), row2(ln1_b[layer])
        if layer % 2 == 0:
            w_in = _pack_even_weight(w_in_even[li])
            w_out = w_out_even[li].astype(BF16)
            lb = lower_bounds[li].reshape(1, A_WIDTH)
            ng = row2(hgrn_norm_g[li])
            h = _matmul(xp, w_in, tm_p, 512)
            qbx, qix, kvf, kvb, kif, kib = _even_post(h, cos_p, sin_p, tm_p)
            oa, s_p = _hgrn_prompt(h, lb, bsz, n_chunks)
            ob = _dsa_prompt(qbx, qix, h, kvb, kib, bsz, n_chunks)
            xp = _even_out(oa, h, ob, xp, w_out, ng, lg1, lb1, tm_p, t_pad)
            kp.append(unpad(kvf[:, :LANES], LANES).reshape(bsz, t_real, B_KV_HEADS, B_HEAD_DIM))
            vp.append(unpad(kvf[:, LANES:], LANES).reshape(bsz, t_real, B_KV_HEADS, B_HEAD_DIM))
            ip.append(unpad(kif[:, :IDX_DIM], IDX_DIM))
            hp.append(s_p)
            h = _matmul(xs, w_in, tm_s, 512)
            qbx, qix, kvf, kvb, kif, kib = _even_post(h, cos_s, sin_s, tm_s)
            oa, s_s = _hgrn_sample(h, lb, state_hgrn, li)
            ob = _even_sample_attention(h, qbx, qix, kvf, kib, cache_k2, cache_v2, cache_kidx2, page_table, li)
            xs = _even_out(oa, h, ob, xs, w_out, ng, lg1, lb1, tm_s, None)
            ks.append(kvf[:, :LANES].reshape(dbsz, 1, B_KV_HEADS, B_HEAD_DIM))
            vs.append(kvf[:, LANES:].reshape(dbsz, 1, B_KV_HEADS, B_HEAD_DIM))
            iks.append(kif[:, :IDX_DIM].reshape(dbsz, 1, IDX_DIM))
            hsm.append(s_s)
        else:
            w_in = w_in_odd[li].astype(BF16)
            w_out = w_out_odd[li].astype(BF16)
            p = _pack_odd_params(conv_c_w[li], conv_c_b[li], conv_c_ln_g[li], conv_c_ln_b[li], conv_d_w[li],
                                 conv_d_b[li], lru_wa[li], lru_ba[li], lru_wx[li], lru_bx[li], lru_lambda[li])
            h = _matmul(xp, w_in, tm_p, 512)
            y, c_p, d_p, h_p = _odd_seq(h, p, bsz, n_chunks)
            xp = _mm_postnorm(y, xp, w_out, lg1, lb1, tm_p, t_pad)
            cp.append(c_p); dp.append(d_p); lp.append(h_p[:, 0])
            h = _matmul(xs, w_in, tm_s, 512)
            y, c_s, d_s, h_s = _odd_sample(h, jnp.swapaxes(state_conv_c[:, li], 0, 1).astype(F32),
                                           jnp.swapaxes(state_conv_d[:, li], 0, 1).astype(F32),
                                           state_lru[:, li].astype(F32), p)
            xs = _mm_postnorm(y, xs, w_out, lg1, lb1, tm_s, None)
            csm.append(jnp.swapaxes(c_s, 0, 1)); dsm.append(jnp.swapaxes(d_s, 0, 1)); lsm.append(h_s)
        mp = _pack_moe_params(router_g_w[layer], router_g_b[layer], router_e_w[layer], router_e_b[layer],
                              w_gate[layer], w_up[layer], w_down[layer])
        lg2, lb2 = row2(ln2_g[layer]), row2(ln2_b[layer])
        xp = _moe(xp, mp, lg2, lb2, _moe_tile(bsz * t_pad), t_pad)
        xs = _moe(xs, mp, lg2, lb2, _moe_tile(dbsz), None)

    y_prompt = xp.reshape(bsz, t_pad, D_MODEL)[:, PAD_ROWS + N_META:]
    y_sample = xs.reshape(dbsz, 1, D_MODEL)
    return (y_prompt, y_sample, jnp.stack(kp, axis=2), jnp.stack(vp, axis=2), jnp.stack(ip, axis=2),
            jnp.stack(hp, axis=1), jnp.stack(cp, axis=1), jnp.stack(dp, axis=1), jnp.stack(lp, axis=1),
            jnp.stack(ks, axis=2), jnp.stack(vs, axis=2), jnp.stack(iks, axis=2), jnp.stack(hsm, axis=1),
            jnp.stack(csm, axis=1), jnp.stack(dsm, axis=1), jnp.stack(lsm, axis=1))
```

```python
import functools
import math

import numpy as np
import jax
import jax.numpy as jnp
from jax import lax
from jax.experimental import pallas as pl
from jax.experimental.pallas import tpu as pltpu

F32 = jnp.float32
BF16 = jnp.bfloat16
I32 = jnp.int32

D_MODEL = 1024
DEPTH = 4
PAGE_SIZE = 128
N_META = 16
A_HEADS = 4
A_DK = 128
A_DV = 128
A_WIDTH = A_HEADS * A_DV
B_HEADS = 8
B_KV_HEADS = 2
B_HEAD_DIM = 64
B_WIDTH = B_HEADS * B_HEAD_DIM
B_KV_WIDTH = B_KV_HEADS * B_HEAD_DIM
IDX_HEADS = 8
IDX_DIM = 64
TOPK_MAX = 256
ROPE_THETA = 10000.0
C_WIDTH = 512
C_CONV = 31
D_WIDTH = 512
D_CONV = 4
D_BLOCKS = 8
D_BLOCK_W = D_WIDTH // D_BLOCKS
LRU_C = 8.0
N_GROUPS = 4
EXPERTS_PER_GROUP = 4
N_EXPERTS = N_GROUPS * EXPERTS_PER_GROUP
TOP_E = 2
EXPERT_FF = 256
DN_ALPHA = (2 * DEPTH) ** 0.25
LN_EPS = 1e-5
RMS_EPS = 1e-6
NEG_BIG = -1e30
LB_TINY = 1e-30

LANES = 128
SUBLANES = 8
ROW_TILE = 128
VMEM_LIMIT_BYTES = 56 * 1024 * 1024

EV_QA, EV_FA, EV_IA, EV_GA = 0, 512, 1024, 1536
EV_QB, EV_QI, EV_KB, EV_VB, EV_KI, EV_WI = 2048, 2560, 3072, 3200, 3328, 3456
EV_COLS = 3584
PAD_ROWS = ROW_TILE - N_META
INT_MIN = -2147483648


def _cparams(sem):
    return pltpu.CompilerParams(dimension_semantics=sem, vmem_limit_bytes=VMEM_LIMIT_BYTES)


def _sigmoid(x):
    return 1.0 / (1.0 + jnp.exp(-x))


def _log_sigmoid(x):
    return jnp.minimum(x, 0.0) - jnp.log1p(jnp.exp(-jnp.abs(x)))


def _layer_norm(y, g, b):
    mu = jnp.mean(y, axis=-1, keepdims=True)
    d = y - mu
    var = jnp.mean(d * d, axis=-1, keepdims=True)
    return d * lax.rsqrt(var + LN_EPS) * g + b


def _keep_rows(tile, tm, seq_pad):
    r = (tile * tm + lax.broadcasted_iota(I32, (tm, 1), 0)).astype(F32)
    pos = r - jnp.floor((r + 0.5) * (1.0 / seq_pad)) * seq_pad
    return jnp.where(pos >= PAD_ROWS, 1.0, 0.0)


def _finish_rows(y, g, b, tile, tm, seq_pad):
    out = _layer_norm(y, g, b)
    return out if seq_pad is None else out * _keep_rows(tile, tm, seq_pad)


def _mm_kernel(x_ref, w_ref, o_ref, *, tn):
    x = x_ref[...].astype(BF16)
    for c in range(0, w_ref.shape[1], tn):
        o_ref[:, c:c + tn] = jnp.dot(x, w_ref[:, c:c + tn], preferred_element_type=F32)


def _matmul(x, w, tm, tn):
    m, k = x.shape
    n = w.shape[1]
    assert m % tm == 0 and n % tn == 0
    return pl.pallas_call(
        functools.partial(_mm_kernel, tn=tn),
        grid=(m // tm,),
        in_specs=[pl.BlockSpec((tm, k), lambda i: (i, 0)),
                  pl.BlockSpec((k, n), lambda i: (0, 0))],
        out_specs=pl.BlockSpec((tm, n), lambda i: (i, 0)),
        out_shape=jax.ShapeDtypeStruct((m, n), F32),
        compiler_params=_cparams(("parallel",)),
        name="matmul",
    )(x, w)


HG_LEVELS = 7


def _hgrn_mid_rows(b_ref, level):
    half = 1 << level
    blk = half * 2
    if blk >= SUBLANES:
        pieces = []
        for start in range(0, ROW_TILE, blk):
            m = start + half - 1
            pieces.append(jnp.broadcast_to(b_ref[m:m + 1, :], (blk, LANES)))
        return pieces[0] if len(pieces) == 1 else jnp.concatenate(pieces, axis=0)
    sub = lax.broadcasted_iota(I32, (SUBLANES, LANES), 0)
    pieces = []
    for start in range(0, ROW_TILE, SUBLANES):
        acc = None
        for off in range(SUBLANES - blk, -1, -blk):
            m = start + off + half - 1
            row = jnp.broadcast_to(b_ref[m:m + 1, :], (SUBLANES, LANES))
            acc = row if acc is None else jnp.where(sub < off + blk, row, acc)
        pieces.append(acc)
    return jnp.concatenate(pieces, axis=0)


def _hgrn_kernel(qa_ref, fa_ref, ia_ref, lb_ref, o_ref, s_ref, st_ref, b_ref):
    c = pl.program_id(2)

    @pl.when(c == 0)
    def _():
        st_ref[...] = jnp.zeros_like(st_ref)

    lb = lb_ref[...]
    z = fa_ref[...]
    qa = qa_ref[...]
    v = ia_ref[...]
    la = jnp.log(jnp.maximum(lb, LB_TINY))
    lc = jnp.log1p(-lb) + _log_sigmoid(z)
    log_f = jnp.maximum(la, lc) + jnp.log1p(jnp.exp(-jnp.abs(la - lc)))
    k = (1.0 - lb) * _sigmoid(-z)
    q = qa * _sigmoid(qa)

    rows = lax.broadcasted_iota(I32, (ROW_TILE, 1), 0)
    cols = lax.broadcasted_iota(I32, (1, ROW_TILE), 1)

    b = log_f
    for lv in range(HG_LEVELS):
        d = 1 << lv
        b = b + jnp.where(rows >= d, pltpu.roll(b, d, axis=0), 0.0)
    b_ref[...] = b

    scores = jnp.zeros((ROW_TILE, ROW_TILE), F32)
    nt = (((1,), (1,)), ((), ()))
    for lv in range(HG_LEVELS):
        e = jnp.exp(-jnp.abs(b - _hgrn_mid_rows(b_ref, lv)))
        upper = (rows & (1 << lv)) != 0
        qd = jnp.where(upper, q * e, 0.0).astype(BF16)
        kd = jnp.where(upper, 0.0, k * e).astype(BF16)
        s_l = lax.dot_general(qd, kd, nt, preferred_element_type=F32)
        if lv + 1 < HG_LEVELS:
            same = (rows >> (lv + 1)) == (cols >> (lv + 1))
            scores = scores + jnp.where(same, s_l, 0.0)
        else:
            scores = scores + s_l
    diag = jnp.sum(q * k, axis=1, keepdims=True)
    scores = jnp.where(rows == cols, diag, scores)

    st = st_ref[...]
    v_bf = v.astype(BF16)
    o = jnp.dot(scores.astype(BF16), v_bf, preferred_element_type=F32)
    o = o + lax.dot_general((q * jnp.exp(b)).astype(BF16), st.astype(BF16), nt,
                            preferred_element_type=F32)
    o_ref[...] = o

    b_last = b_ref[ROW_TILE - 1:ROW_TILE, :]
    kdl = (k * jnp.exp(b_last - b)).astype(BF16)
    st_new = st * jnp.exp(b_last) + lax.dot_general(v_bf, kdl, (((0,), (0,)), ((), ())),
                                                    preferred_element_type=F32)
    st_ref[...] = st_new

    @pl.when(c == pl.num_programs(2) - 1)
    def _():
        s_ref[0, 0] = st_new.T


def _hgrn_prompt(h, lb, n_batch, n_chunks):
    m = h.shape[0]
    col = lambda base: (lambda b, hh, c: (b * n_chunks + c, base // LANES + hh))
    return pl.pallas_call(
        _hgrn_kernel,
        grid=(n_batch, A_HEADS, n_chunks),
        in_specs=[pl.BlockSpec((ROW_TILE, LANES), col(EV_QA)),
                  pl.BlockSpec((ROW_TILE, LANES), col(EV_FA)),
                  pl.BlockSpec((ROW_TILE, LANES), col(EV_IA)),
                  pl.BlockSpec((1, LANES), lambda b, hh, c: (0, hh))],
        out_specs=[pl.BlockSpec((ROW_TILE, LANES), lambda b, hh, c: (b * n_chunks + c, hh)),
                   pl.BlockSpec((1, 1, A_DK, A_DV), lambda b, hh, c: (b, hh, 0, 0))],
        out_shape=[jax.ShapeDtypeStruct((m, A_WIDTH), F32),
                   jax.ShapeDtypeStruct((n_batch, A_HEADS, A_DK, A_DV), F32)],
        scratch_shapes=[pltpu.VMEM((A_DV, A_DK), F32), pltpu.VMEM((ROW_TILE, LANES), F32)],
        compiler_params=_cparams(("parallel", "parallel", "arbitrary")),
        name="hgrn_prompt",
    )(h, h, h, lb)


HS_TB = 8


def _hgrn_sample_kernel(qa_ref, fa_ref, ia_ref, lb_ref, s_ref, o_ref, so_ref):
    lb = lb_ref[...]
    z = fa_ref[...]
    la = jnp.log(jnp.maximum(lb, LB_TINY))
    lc = jnp.log1p(-lb) + _log_sigmoid(z)
    f = jnp.exp(jnp.maximum(la, lc) + jnp.log1p(jnp.exp(-jnp.abs(la - lc))))
    k = (1.0 - lb) * _sigmoid(-z)
    qa = qa_ref[...]
    q = qa * _sigmoid(qa)
    v = ia_ref[...]
    pad = jnp.zeros((LANES - 3 * HS_TB, LANES), F32)
    for h in range(A_HEADS):
        sl = slice(h * A_DK, (h + 1) * A_DK)
        cols = jnp.concatenate([q[:, sl], k[:, sl], f[:, sl], pad], axis=0).T
        for b in range(HS_TB):
            qc = cols[:, b:b + 1]
            kc = cols[:, HS_TB + b:HS_TB + b + 1]
            fc = cols[:, 2 * HS_TB + b:2 * HS_TB + b + 1]
            s_new = fc * s_ref[b, 0, h] + kc * v[b:b + 1, sl]
            so_ref[b, h] = s_new
            o_ref[b:b + 1, sl] = jnp.sum(qc * s_new, axis=0, keepdims=True)


def _hgrn_sample(h, lb, state, layer):
    db = h.shape[0]
    row = lambda cb: pl.BlockSpec((HS_TB, A_WIDTH), lambda i: (i, cb))
    return pl.pallas_call(
        _hgrn_sample_kernel,
        grid=(db // HS_TB,),
        in_specs=[row(EV_QA // A_WIDTH), row(EV_FA // A_WIDTH), row(EV_IA // A_WIDTH),
                  pl.BlockSpec((1, A_WIDTH), lambda i: (0, 0)),
                  pl.BlockSpec((HS_TB, 1, A_HEADS, A_DK, A_DV), lambda i: (i, layer, 0, 0, 0))],
        out_specs=[row(0), pl.BlockSpec((HS_TB, A_HEADS, A_DK, A_DV), lambda i: (i, 0, 0, 0))],
        out_shape=[jax.ShapeDtypeStruct((db, A_WIDTH), F32),
                   jax.ShapeDtypeStruct((db, A_HEADS, A_DK, A_DV), F32)],
        compiler_params=_cparams(("parallel",)),
        name="hgrn_sample",
    )(h, h, h, lb, state)


def _pack_even_weight(w):
    parts = (A_HEADS * A_DK, A_HEADS * A_DK, A_WIDTH, A_WIDTH, B_WIDTH, B_KV_WIDTH, B_KV_WIDTH,
             IDX_HEADS * IDX_DIM, IDX_DIM, IDX_HEADS)
    cuts = [int(c) for c in np.cumsum(parts)[:-1]]
    qa, fa, ia, ga, qb, kb, vb, qi, ki, wi = jnp.split(w, cuts, axis=1)
    zeros = lambda n: jnp.zeros((w.shape[0], n), w.dtype)
    out = jnp.concatenate([qa, fa, ia, ga, qb, qi, kb, vb, ki, zeros(LANES - IDX_DIM),
                           wi, zeros(LANES - IDX_HEADS)], axis=1)
    assert out.shape[1] == EV_COLS
    return out.astype(BF16)

def _rope_tables(pos):
    half = B_HEAD_DIM // 2
    lane = np.arange(LANES)
    inv = ROPE_THETA ** (-(lane % half).astype(np.float64) / half)
    ang = np.asarray(pos, np.float64)[:, None] * inv[None, :]
    sign = np.where((lane % B_HEAD_DIM) < half, -1.0, 1.0)
    return np.cos(ang).astype(np.float32), (np.sin(ang) * sign[None, :]).astype(np.float32)


def _rope128(x, cos, sin, first_half):
    rot = jnp.where(first_half, pltpu.roll(x, LANES - 32, axis=1), pltpu.roll(x, 32, axis=1))
    return x * cos + rot * sin


def _even_post_kernel(qb_ref, qi_ref, kv_ref, ki_ref, cos_ref, sin_ref,
                      qbx_ref, qix_ref, kvf_ref, kvb_ref, kif_ref, kib_ref):
    cos = cos_ref[...]
    sin = sin_ref[...]
    lane = lax.broadcasted_iota(I32, (1, LANES), 1)
    first_half = (lane % B_HEAD_DIM) < (B_HEAD_DIM // 2)
    low = lane < B_HEAD_DIM
    scale = B_HEAD_DIM ** -0.5
    for pair in range(B_HEADS // 2):
        sl = slice(pair * LANES, (pair + 1) * LANES)
        qb = _rope128(qb_ref[:, sl], cos, sin, first_half) * scale
        qi = _rope128(qi_ref[:, sl], cos, sin, first_half) * scale
        qb_sw = pltpu.roll(qb, B_HEAD_DIM, axis=1)
        group = (2 * pair) // (B_HEADS // B_KV_HEADS)
        for sub in range(2):
            h = 2 * pair + sub
            src = qb if sub == group else qb_sw
            keep = low if group == 0 else jnp.logical_not(low)
            qbx_ref[:, h * LANES:(h + 1) * LANES] = jnp.where(keep, src, 0.0).astype(BF16)
            keep_i = low if sub == 0 else jnp.logical_not(low)
            qix_ref[:, h * LANES:(h + 1) * LANES] = jnp.where(keep_i, qi, 0.0).astype(BF16)
    k = _rope128(kv_ref[:, :LANES], cos, sin, first_half)
    v = kv_ref[:, LANES:]
    kvf_ref[:, :LANES] = k
    kvf_ref[:, LANES:] = v
    kvb_ref[:, :LANES] = k.astype(BF16)
    kvb_ref[:, LANES:] = v.astype(BF16)
    ki = _rope128(ki_ref[...], cos, sin, first_half)
    kif_ref[...] = ki
    kib_ref[...] = (ki + pltpu.roll(ki, B_HEAD_DIM, axis=1)).astype(BF16)


def _even_post(h, cos, sin, tm):
    m = h.shape[0]
    row = lambda w, cb: pl.BlockSpec((tm, w), lambda i: (i, cb))
    return pl.pallas_call(
        _even_post_kernel,
        grid=(m // tm,),
        in_specs=[row(512, EV_QB // 512), row(512, EV_QI // 512), row(256, EV_KB // 256),
                  row(LANES, EV_KI // LANES), row(LANES, 0), row(LANES, 0)],
        out_specs=[row(1024, 0), row(1024, 0), row(256, 0), row(256, 0), row(LANES, 0), row(LANES, 0)],
        out_shape=[jax.ShapeDtypeStruct((m, 1024), BF16), jax.ShapeDtypeStruct((m, 1024), BF16),
                   jax.ShapeDtypeStruct((m, 256), F32), jax.ShapeDtypeStruct((m, 256), BF16),
                   jax.ShapeDtypeStruct((m, LANES), F32), jax.ShapeDtypeStruct((m, LANES), BF16)],
        compiler_params=_cparams(("parallel",)),
        name="even_post",
    )(h, h, h, h, cos, sin)


def _ordinal_to_f32(k):
    return pltpu.bitcast(jnp.where(k < 0, k ^ 0x7FFFFFFF, k), F32)


def _kth_largest(count_ge, n_sel, shape):
    n_f = float(n_sel)
    base = jnp.where(count_ge(jnp.zeros(shape, F32)) >= n_f, 0, INT_MIN).astype(I32)

    def bit_step(i, base):
        cand = base | (jnp.int32(1) << (30 - i))
        return jnp.where(count_ge(_ordinal_to_f32(cand)) >= n_f, cand, base)

    return _ordinal_to_f32(lax.fori_loop(0, 31, bit_step, base))


def _tie_cut(count_eq_before, need, n_cols_log2):
    def step(i, c):
        cand = c + (jnp.int32(1) << (n_cols_log2 - 1 - i))
        return jnp.where(count_eq_before(cand) < need, cand, c)
    return lax.fori_loop(0, n_cols_log2, step, jnp.zeros(need.shape, I32))


def _topk_select(score, cols, n_sel, n_cols_log2):
    rows = score.shape[0]
    count = lambda m: jnp.sum(m, axis=1, keepdims=True)
    thr = _kth_largest(lambda t: count(jnp.where(score >= t, 1.0, 0.0)), n_sel, (rows, 1))
    gt = jnp.where(score > thr, 1.0, 0.0)
    eq = jnp.where(score == thr, 1.0, 0.0)
    need = float(n_sel) - count(gt)
    c_all = jnp.full((rows, 1), (1 << n_cols_log2) - 1, I32)
    c_star = lax.cond(jnp.max(count(eq) - need) > 0.0,
                      lambda _: _tie_cut(lambda c: count(jnp.where(cols < c, eq, 0.0)), need, n_cols_log2),
                      lambda _: c_all, 0)
    return gt + jnp.where(cols <= c_star, eq, 0.0)


def _merge_head_pair(a, b, group, low):
    if group == 0:
        return jnp.where(low, a, pltpu.roll(b, B_HEAD_DIM, axis=1))
    return jnp.where(low, pltpu.roll(a, B_HEAD_DIM, axis=1), b)


DSA_KEY_BLOCK = 512
DSA_ATT_BLOCK = 512


def _transpose_bf16(x):
    return x.astype(F32).T.astype(BF16)


def _fold_keys(x, reduce):
    keys, nq = x.shape
    return reduce(reduce(x.reshape(keys // 64, 64, nq), axis=0), axis=0, keepdims=True)


def _dsa_prompt_kernel(qb_ref, qi_ref, wi_ref, kv_ref, ki_ref, o_ref, s_ref, acc_ref, qit_ref, qbt_ref,
                       *, n_seq, n_sel, n_cols_log2):
    kb_w = DSA_KEY_BLOCK
    nq = n_seq * ROW_TILE
    j = pl.program_id(1)
    nk = ((j + 1) * ROW_TILE + kb_w - 1) // kb_w
    per = B_HEADS // B_KV_HEADS
    key_in_block = lax.broadcasted_iota(I32, (kb_w, 1), 0)
    qpos1 = j * ROW_TILE + lax.broadcasted_iota(I32, (1, ROW_TILE), 1) - PAD_ROWS
    qpos = jnp.concatenate([qpos1] * n_seq, axis=1)
    key_sum = lambda a: jnp.sum(a, axis=0, keepdims=True)

    def key_rows(kb):
        return pl.ds(pl.multiple_of(kb * kb_w, kb_w), kb_w)

    w_rows = []
    for b in range(n_seq):
        w_rows.append(wi_ref[b].T * (IDX_HEADS ** -0.5))
        for h in range(IDX_HEADS):
            qit_ref[b, h * LANES:(h + 1) * LANES, :] = _transpose_bf16(qi_ref[b, :, h * LANES:(h + 1) * LANES])
        for h in range(B_HEADS):
            a, i = b * B_KV_HEADS + h // per, h % per
            qbt_ref[a, :, i * ROW_TILE:(i + 1) * ROW_TILE] = _transpose_bf16(qb_ref[b, :, h * LANES:(h + 1) * LANES])

    def score_block(kb, carry):
        kpos = kb * kb_w + key_in_block - PAD_ROWS
        for b in range(n_seq):
            ki = ki_ref[b, key_rows(kb), :]
            acc = jnp.zeros((kb_w, ROW_TILE), F32)
            for h in range(IDX_HEADS):
                d = jnp.dot(ki, qit_ref[b, h * LANES:(h + 1) * LANES, :], preferred_element_type=F32)
                acc = acc + w_rows[b][h:h + 1, :] * jnp.maximum(d, 0.0)
            s = jnp.where(kpos <= qpos1, acc, NEG_BIG)
            s_ref[kb, :, b * ROW_TILE:(b + 1) * ROW_TILE] = jnp.where(kpos >= 0, s, -jnp.inf)
        return carry
    lax.fori_loop(0, nk, score_block, 0)

    n_chain = 8

    def over_blocks(fn):
        def body(kb, a):
            f = fn(kb, s_ref[kb]).reshape(kb_w // (n_chain * SUBLANES), n_chain * SUBLANES, nq)
            return a + jnp.sum(f, axis=0)
        return key_sum(lax.fori_loop(0, nk, body, jnp.zeros((n_chain * SUBLANES, nq), F32)))

    thr = _kth_largest(lambda t: over_blocks(lambda kb, s: jnp.where(s >= t, 1.0, 0.0)), n_sel, (1, nq))
    need = float(n_sel) - over_blocks(lambda kb, s: jnp.where(s > thr, 1.0, 0.0))
    n_eq = over_blocks(lambda kb, s: jnp.where(s == thr, 1.0, 0.0))
    takes_all = qpos < n_sel
    c_all = jnp.full((1, nq), (1 << n_cols_log2) - 1, I32)

    def eq_before(c):
        return over_blocks(lambda kb, s: jnp.where(s == thr, jnp.where(kb * kb_w + key_in_block < c, 1.0, 0.0), 0.0))

    c_star = lax.cond(jnp.max(jnp.where(takes_all, 0.0, n_eq - need)) > 0.0,
                      lambda _: _tie_cut(eq_before, need, n_cols_log2), lambda _: c_all, 0)

    n_att = n_seq * B_KV_HEADS
    acc_ref[...] = jnp.zeros_like(acc_ref)
    sub_w = DSA_ATT_BLOCK
    key_in_sub = key_in_block[:sub_w]

    def attend_block(kb, carry):
        carry = list(carry)
        for sub in range(kb_w // sub_w):
            lo = sub * sub_w
            s = s_ref[kb, lo:lo + sub_w, :]
            kidx = kb * kb_w + lo + key_in_sub
            kpos = kidx - PAD_ROWS
            picked = jnp.where(s > thr, 1.0, jnp.where(s == thr, jnp.where(kidx <= c_star, 1.0, 0.0), 0.0))
            picked = jnp.where(takes_all, 1.0, picked)
            valid = jnp.where(kpos >= 0, jnp.where(kpos <= qpos, picked, 0.0), 0.0)
            sub_rows = pl.ds(pl.multiple_of(kb * kb_w + lo, sub_w), sub_w)
            for b in range(n_seq):
                ok = jnp.concatenate([valid[:, b * ROW_TILE:(b + 1) * ROW_TILE]] * per, axis=1) > 0.5
                v_t = _transpose_bf16(kv_ref[b, sub_rows, LANES:])
                k_blk = kv_ref[b, sub_rows, :LANES]
                for g in range(B_KV_HEADS):
                    a = b * B_KV_HEADS + g
                    m_old, l_old = carry[2 * a], carry[2 * a + 1]
                    logits = jnp.dot(k_blk, qbt_ref[a], preferred_element_type=F32)
                    logits = jnp.where(ok, logits, NEG_BIG)
                    m_new = jnp.maximum(m_old, _fold_keys(logits, jnp.max))
                    alpha = jnp.exp(m_old - m_new)
                    p = jnp.exp(logits - m_new)
                    carry[2 * a] = m_new
                    carry[2 * a + 1] = alpha * l_old + _fold_keys(p, jnp.sum)
                    acc_ref[a] = acc_ref[a] * alpha + jnp.dot(v_t, p.astype(BF16), preferred_element_type=F32)
        return tuple(carry)

    init = []
    for a in range(n_att):
        init += [jnp.full((1, per * ROW_TILE), -jnp.inf, F32), jnp.zeros((1, per * ROW_TILE), F32)]
    stats = lax.fori_loop(0, nk, attend_block, tuple(init))

    low = lax.broadcasted_iota(I32, (1, LANES), 1) < B_HEAD_DIM
    for a in range(n_att):
        b, g = a // B_KV_HEADS, a % B_KV_HEADS
        o_t = acc_ref[a] / stats[2 * a + 1]
        o = [o_t[:, i * ROW_TILE:(i + 1) * ROW_TILE].T for i in range(per)]
        for pair in range(per // 2):
            c = (g * per) // 2 + pair
            o_ref[b, :, c * LANES:(c + 1) * LANES] = _merge_head_pair(o[2 * pair], o[2 * pair + 1], g, low)


def _dsa_prompt(qbx, qix, h, kvb, kib, n_batch, n_chunks):
    t = n_chunks * ROW_TILE
    n_seq = 2 if n_batch % 2 == 0 else 1
    n_kb = -(-t // DSA_KEY_BLOCK)
    tk = n_kb * DSA_KEY_BLOCK
    seq3 = lambda a: a.reshape(n_batch, t, a.shape[-1])
    pad_keys = lambda a: jnp.pad(seq3(a), ((0, 0), (0, tk - t), (0, 0)))
    rowblk = lambda w, cb: pl.BlockSpec((n_seq, ROW_TILE, w), lambda b, j: (b, j, cb))
    out = pl.pallas_call(
        functools.partial(_dsa_prompt_kernel, n_seq=n_seq, n_sel=min(TOPK_MAX, (t - PAD_ROWS) // 4),
                          n_cols_log2=math.ceil(math.log2(tk))),
        grid=(n_batch // n_seq, n_chunks),
        in_specs=[rowblk(1024, 0), rowblk(1024, 0), rowblk(LANES, EV_WI // LANES),
                  pl.BlockSpec((n_seq, tk, 256), lambda b, j: (b, 0, 0)),
                  pl.BlockSpec((n_seq, tk, LANES), lambda b, j: (b, 0, 0))],
        out_specs=rowblk(512, 0),
        out_shape=jax.ShapeDtypeStruct((n_batch, t, B_WIDTH), F32),
        scratch_shapes=[pltpu.VMEM((n_kb, DSA_KEY_BLOCK, n_seq * ROW_TILE), F32),
                        pltpu.VMEM((n_seq * B_KV_HEADS, LANES, (B_HEADS // B_KV_HEADS) * ROW_TILE), F32),
                        pltpu.VMEM((n_seq, IDX_HEADS * LANES, ROW_TILE), BF16),
                        pltpu.VMEM((n_seq * B_KV_HEADS, LANES, (B_HEADS // B_KV_HEADS) * ROW_TILE), BF16)],
        compiler_params=_cparams(("parallel", "arbitrary")),
        name="dsa_prompt",
    )(seq3(qbx), seq3(qix), seq3(h), pad_keys(kvb), pad_keys(kib))
    return out.reshape(n_batch * t, B_WIDTH)


def _page_specs(n_pages, width, col_block):
    return [pl.BlockSpec((1, PAGE_SIZE, width), functools.partial(lambda i, pt, p: (pt[i, p], 0, col_block), p=p))
            for p in range(n_pages)]


def _dsa_sample_score_kernel(pt_ref, q_ref, w_ref, qx_ref, kx_ref, *refs, n_pages, n_cols):
    del pt_ref
    page_refs, o_ref = refs[:n_pages], refs[n_pages]
    nt = (((1,), (1,)), ((), ()))
    q = q_ref[0]
    w = w_ref[0] * (IDX_HEADS ** -0.5)
    pieces = []
    for p in range(n_pages):
        d = lax.dot_general(q, page_refs[p][0].astype(BF16), nt, preferred_element_type=F32)
        pieces.append(jnp.sum(w * jnp.maximum(d, 0.0), axis=0, keepdims=True))
    d_self = jnp.sum(qx_ref[0].astype(F32) * kx_ref[0].astype(F32), axis=1, keepdims=True)
    s_self = jnp.sum(w[:, :1] * jnp.maximum(d_self, 0.0), axis=0, keepdims=True)
    lane = lax.broadcasted_iota(I32, (1, LANES), 1)
    pieces.append(jnp.where(lane == 0, s_self, 0.0))
    pad = n_cols - (n_pages + 1) * LANES
    if pad:
        pieces.append(jnp.zeros((1, pad), F32))
    o_ref[0] = jnp.concatenate(pieces, axis=1)


def _dsa_sample_scores(page_table, q_is, w_ib, q_ix, k_ib, cache_kidx2, n_cols):
    db, n_pages = page_table.shape
    per_seq = lambda shape: pl.BlockSpec((1,) + shape, lambda i, pt: (i, 0, 0))
    grid_spec = pltpu.PrefetchScalarGridSpec(
        num_scalar_prefetch=1, grid=(db,),
        in_specs=[per_seq((IDX_HEADS, LANES)), per_seq((IDX_HEADS, LANES)), per_seq((IDX_HEADS, LANES)),
                  per_seq((1, LANES))] + _page_specs(n_pages, LANES, 0),
        out_specs=per_seq((1, n_cols)))
    return pl.pallas_call(
        functools.partial(_dsa_sample_score_kernel, n_pages=n_pages, n_cols=n_cols),
        grid_spec=grid_spec,
        out_shape=jax.ShapeDtypeStruct((db, 1, n_cols), F32),
        compiler_params=_cparams(("arbitrary",)),
        name="dsa_sample_scores",
    )(page_table, q_is, w_ib, q_ix, k_ib, *([cache_kidx2] * n_pages))


def _dsa_sample_select_kernel(s_ref, o_ref, *, n_keys, n_sel, n_cols_log2):
    cols = lax.broadcasted_iota(I32, (1, s_ref.shape[1]), 1)
    score = jnp.where(cols < n_keys, s_ref[...], -jnp.inf)
    o_ref[...] = _topk_select(score, cols, n_sel, n_cols_log2)


def _dsa_sample_select(scores, n_keys):
    db, n_cols = scores.shape
    return pl.pallas_call(
        functools.partial(_dsa_sample_select_kernel, n_keys=n_keys, n_sel=min(TOPK_MAX, n_keys // 4),
                          n_cols_log2=math.ceil(math.log2(n_cols))),
        out_shape=jax.ShapeDtypeStruct((db, n_cols), F32),
        compiler_params=pltpu.CompilerParams(vmem_limit_bytes=VMEM_LIMIT_BYTES),
        name="dsa_sample_select",
    )(scores)


def _dsa_sample_attend_kernel(pt_ref, q_ref, m_ref, kn_ref, *refs, n_pages):
    del pt_ref
    k_refs, v_refs, o_ref = refs[:n_pages], refs[n_pages:2 * n_pages], refs[2 * n_pages]
    nt = (((1,), (1,)), ((), ()))
    q = q_ref[0]
    mask = m_ref[0]
    logits = []
    for p in range(n_pages):
        l_p = lax.dot_general(q, k_refs[p][0].astype(BF16), nt, preferred_element_type=F32)
        logits.append(jnp.where(mask[:, p * LANES:(p + 1) * LANES] > 0.5, l_p, NEG_BIG))
    kn = kn_ref[0]
    k_new = kn[:, :LANES].astype(BF16).astype(F32)
    v_new = kn[:, LANES:].astype(BF16).astype(F32)
    l_self = jnp.sum(q.astype(F32) * k_new, axis=1, keepdims=True)
    l_self = jnp.where(mask[:, n_pages * LANES:n_pages * LANES + 1] > 0.5, l_self, NEG_BIG)
    mx = l_self
    for l_p in logits:
        mx = jnp.maximum(mx, jnp.max(l_p, axis=1, keepdims=True))
    p_self = jnp.exp(l_self - mx)
    den = p_self
    acc = p_self * v_new
    for p in range(n_pages):
        w_p = jnp.exp(logits[p] - mx)
        den = den + jnp.sum(w_p, axis=1, keepdims=True)
        acc = acc + jnp.dot(w_p.astype(BF16), v_refs[p][0].astype(BF16), preferred_element_type=F32)
    o_ref[0] = acc / den


def _dsa_sample_attend(page_table, q8, mask, kv_new, cache_k2, cache_v2, layer):
    db, n_pages = page_table.shape
    n_cols = mask.shape[-1]
    per_seq = lambda shape: pl.BlockSpec((1,) + shape, lambda i, pt: (i, 0, 0))
    grid_spec = pltpu.PrefetchScalarGridSpec(
        num_scalar_prefetch=1, grid=(db,),
        in_specs=[per_seq((B_HEADS, LANES)), per_seq((1, n_cols)), per_seq((1, 2 * LANES))]
        + _page_specs(n_pages, LANES, layer) + _page_specs(n_pages, LANES, layer),
        out_specs=per_seq((B_HEADS, LANES)))
    return pl.pallas_call(
        functools.partial(_dsa_sample_attend_kernel, n_pages=n_pages),
        grid_spec=grid_spec,
        out_shape=jax.ShapeDtypeStruct((db, B_HEADS, LANES), F32),
        compiler_params=_cparams(("arbitrary",)),
        name="dsa_sample_attend",
    )(page_table, q8, mask, kv_new, *([cache_k2] * n_pages), *([cache_v2] * n_pages))


def _even_out_kernel(oa_ref, ga_ref, ob_ref, x_ref, w_ref, ng_ref, lg_ref, lb_ref, o_ref, *, tm, seq_pad):
    ng = ng_ref[...]
    acc = jnp.dot(ob_ref[...].astype(BF16), w_ref[A_WIDTH:, :], preferred_element_type=F32)
    for h in range(A_HEADS):
        sl = slice(h * A_DV, (h + 1) * A_DV)
        oa = oa_ref[:, sl]
        oa = oa * lax.rsqrt(jnp.mean(oa * oa, axis=-1, keepdims=True) + RMS_EPS) * ng
        ga = ga_ref[:, sl]
        oa = oa * (ga * _sigmoid(ga))
        acc = acc + jnp.dot(oa.astype(BF16), w_ref[sl, :], preferred_element_type=F32)
    y = DN_ALPHA * x_ref[...] + acc
    o_ref[...] = _finish_rows(y, lg_ref[...], lb_ref[...], pl.program_id(0), tm, seq_pad)


def _even_out(oa, h, ob, x, w, ng, lg, lb, tm, seq_pad):
    m = x.shape[0]
    row = lambda w_, cb: pl.BlockSpec((tm, w_), lambda i: (i, cb))
    full = lambda a: pl.BlockSpec(a.shape, lambda i: (0,) * a.ndim)
    return pl.pallas_call(
        functools.partial(_even_out_kernel, tm=tm, seq_pad=seq_pad),
        grid=(m // tm,),
        in_specs=[row(A_WIDTH, 0), row(A_WIDTH, EV_GA // A_WIDTH), row(B_WIDTH, 0), row(D_MODEL, 0),
                  full(w), full(ng), full(lg), full(lb)],
        out_specs=row(D_MODEL, 0),
        out_shape=jax.ShapeDtypeStruct((m, D_MODEL), F32),
        compiler_params=_cparams(("parallel",)),
        name="even_out",
    )(oa, h, ob, x, w, ng, lg, lb)


RT_EXPERT0 = N_GROUPS


def _lane_argmax(v, lane):
    mx = jnp.max(v, axis=-1, keepdims=True)
    idx = jnp.min(jnp.where(v == mx, lane, float(LANES)), axis=-1, keepdims=True)
    return mx, idx


def _router_gates(x, wrh_ref, wrl_ref, rb_ref):
    xh = x.astype(BF16)
    xl = (x - xh.astype(F32)).astype(BF16)
    logits = (jnp.dot(xh, wrh_ref[...], preferred_element_type=F32)
              + jnp.dot(xl, wrh_ref[...], preferred_element_type=F32)
              + jnp.dot(xh, wrl_ref[...], preferred_element_type=F32)) + rb_ref[...]
    lane_i = lax.broadcasted_iota(I32, logits.shape, 1)
    lane = lane_i.astype(F32)
    neg_inf = -jnp.inf
    g_logits = jnp.where(lane_i < N_GROUPS, logits, neg_inf)
    g_max, g_idx = _lane_argmax(g_logits, lane)
    g_val = 1.0 / jnp.sum(jnp.exp(g_logits - g_max), axis=-1, keepdims=True)
    e_lane = lane_i - RT_EXPERT0
    lane_group = jnp.where(e_lane >= 0, e_lane >> 2, -1)
    lane_group = jnp.where(lane_i < RT_EXPERT0 + N_EXPERTS, lane_group, -1).astype(F32)
    e_logits = jnp.where(lane_group == g_idx, logits, neg_inf)
    e_max, first = _lane_argmax(e_logits, lane)
    p = jnp.exp(e_logits - e_max)
    p = p / jnp.sum(p, axis=-1, keepdims=True)
    p1 = jnp.sum(jnp.where(lane == first, p, 0.0), axis=-1, keepdims=True)
    rest = jnp.where(lane == first, neg_inf, jnp.where(lane_group == g_idx, p, neg_inf))
    p2, second = _lane_argmax(rest, lane)
    scale = g_val / (p1 + p2)
    gates = jnp.where(lane == first, p1 * scale, jnp.where(lane == second, p2 * scale, 0.0))
    return gates, g_idx


def _moe_kernel(x_ref, wrh_ref, wrl_ref, rb_ref, wgu_ref, wd_ref, lg_ref, lb_ref, o_ref, acc_ref, gate_ref,
                *, tm, seq_pad):
    e = pl.program_id(1)

    @pl.when(e == 0)
    def _():
        gate_ref[...] = _router_gates(x_ref[...], wrh_ref, wrl_ref, rb_ref)[0]
        acc_ref[...] = jnp.zeros_like(acc_ref)

    lane = lax.broadcasted_iota(I32, (1, LANES), 1)
    gate = jnp.sum(jnp.where(lane == e + RT_EXPERT0, gate_ref[...], 0.0), axis=-1, keepdims=True)
    hgu = jnp.dot(x_ref[...].astype(BF16), wgu_ref[0], preferred_element_type=F32)
    hg = hgu[:, :EXPERT_FF]
    act = (hg * _sigmoid(hg)) * hgu[:, EXPERT_FF:] * gate
    acc_ref[...] += jnp.dot(act.astype(BF16), wd_ref[0], preferred_element_type=F32)

    @pl.when(e == N_EXPERTS - 1)
    def _():
        y = DN_ALPHA * x_ref[...] + acc_ref[...]
        o_ref[...] = _finish_rows(y, lg_ref[...], lb_ref[...], pl.program_id(0), tm, seq_pad)


def _moe(x, p, lg, lb, tm, seq_pad):
    m = x.shape[0]
    full = lambda a: pl.BlockSpec(a.shape, lambda i, e: (0,) * a.ndim)
    return pl.pallas_call(
        functools.partial(_moe_kernel, tm=tm, seq_pad=seq_pad),
        grid=(m // tm, N_EXPERTS),
        in_specs=[pl.BlockSpec((tm, D_MODEL), lambda i, e: (i, 0)),
                  full(p["wrh"]), full(p["wrl"]), full(p["rb"]),
                  pl.BlockSpec((1, D_MODEL, 2 * EXPERT_FF), lambda i, e: (e, 0, 0)),
                  pl.BlockSpec((1, EXPERT_FF, D_MODEL), lambda i, e: (e, 0, 0)),
                  full(lg), full(lb)],
        out_specs=pl.BlockSpec((tm, D_MODEL), lambda i, e: (i, 0)),
        out_shape=jax.ShapeDtypeStruct((m, D_MODEL), F32),
        scratch_shapes=[pltpu.VMEM((tm, D_MODEL), F32), pltpu.VMEM((tm, LANES), F32)],
        compiler_params=_cparams(("parallel", "arbitrary")),
        name="moe",
    )(x, p["wrh"], p["wrl"], p["rb"], p["wgu"], p["wd"], lg, lb)


def _pack_moe_params(rg_w, rg_b, re_w, re_b, w_gate, w_up, w_down):
    d = rg_w.shape[0]
    wr = jnp.concatenate([rg_w, re_w, jnp.zeros((d, LANES - N_GROUPS - N_EXPERTS), F32)], axis=1).astype(F32)
    wrh = wr.astype(BF16)
    wrl = (wr - wrh.astype(F32)).astype(BF16)
    rb = jnp.concatenate([rg_b, re_b, jnp.zeros((LANES - N_GROUPS - N_EXPERTS,), F32)]).reshape(1, LANES)
    return dict(wrh=wrh, wrl=wrl, rb=rb.astype(F32),
                wgu=jnp.concatenate([w_gate, w_up], axis=2).astype(BF16), wd=w_down.astype(BF16))


RT_GROUP_LANE = 0
RT_KEEP_LANE = 1
XG_COLS = D_MODEL + LANES
MOE_SORT_TILE = 512
GATHER_UNROLL = 8


def _moe_route_kernel(x_ref, wrh_ref, wrl_ref, rb_ref, o_ref, *, tm, seq_pad):
    x = x_ref[...]
    gates, g_idx = _router_gates(x, wrh_ref, wrl_ref, rb_ref)
    lane = lax.broadcasted_iota(I32, gates.shape, 1)
    keep = _keep_rows(pl.program_id(0), tm, seq_pad)
    tile = jnp.where(lane == RT_GROUP_LANE, g_idx, jnp.where(lane == RT_KEEP_LANE, keep, gates))
    o_ref[:, :D_MODEL] = x
    o_ref[:, D_MODEL:] = tile


def _moe_route(x, p, tm, seq_pad):
    m = x.shape[0]
    full = lambda a: pl.BlockSpec(a.shape, lambda i: (0,) * a.ndim)
    return pl.pallas_call(
        functools.partial(_moe_route_kernel, tm=tm, seq_pad=seq_pad),
        grid=(m // tm,),
        in_specs=[pl.BlockSpec((tm, D_MODEL), lambda i: (i, 0)), full(p["wrh"]), full(p["wrl"]), full(p["rb"])],
        out_specs=pl.BlockSpec((tm, XG_COLS), lambda i: (i, 0)),
        out_shape=jax.ShapeDtypeStruct((m, XG_COLS), F32),
        compiler_params=_cparams(("parallel",)),
        name="moe_route",
    )(x, p["wrh"], p["wrl"], p["rb"])


def _moe_grouped_kernel(src_ref, dst_ref, tg_ref, xg_hbm, wgu_ref, wd_ref, lg_ref, lb_ref, out_hbm,
                        xbuf, ybuf, sem_in, sem_out, *, tm):
    i = pl.program_id(0)
    last = pl.num_programs(0) - 1
    slot = i % 2
    other = 1 - slot

    def row_in(tile, s, r):
        return pltpu.make_async_copy(xg_hbm.at[pl.ds(src_ref[tile * tm + r], 1)], xbuf.at[s, pl.ds(r, 1)],
                                     sem_in.at[s])

    def row_out(tile, s, r):
        return pltpu.make_async_copy(ybuf.at[s, pl.ds(r, 1)], out_hbm.at[pl.ds(dst_ref[tile * tm + r], 1)],
                                     sem_out.at[s])

    def every_row(fn):
        def body(r, c):
            fn(r)
            return c
        lax.fori_loop(0, tm, body, 0, unroll=GATHER_UNROLL)

    @pl.when(i == 0)
    def _():
        ybuf[...] = jnp.zeros_like(ybuf)
        every_row(lambda r: row_in(0, 0, r).start())

    every_row(lambda r: row_in(i, slot, r).wait())

    @pl.when(i >= 1)
    def _():
        every_row(lambda r: row_out(i - 1, slot, r).wait())

    x = xbuf[slot, :, :D_MODEL]
    rt = xbuf[slot, :, D_MODEL:]
    lane = lax.broadcasted_iota(I32, (1, LANES), 1)
    first_lane = RT_EXPERT0 + EXPERTS_PER_GROUP * tg_ref[i]
    x_bf = x.astype(BF16)
    acc = jnp.zeros((tm, D_MODEL), F32)
    chunk = tm // EXPERTS_PER_GROUP
    for e in range(EXPERTS_PER_GROUP):
        for r in range(e * chunk, (e + 1) * chunk):
            row_in(i + 1, other, r).start()
            row_out(i, other, r).start()
        gate = jnp.sum(jnp.where(lane == first_lane + e, rt, 0.0), axis=-1, keepdims=True)
        hgu = jnp.dot(x_bf, wgu_ref[0, e], preferred_element_type=F32)
        hg = hgu[:, :EXPERT_FF]
        act = (hg * _sigmoid(hg)) * hgu[:, EXPERT_FF:] * gate
        acc = acc + jnp.dot(act.astype(BF16), wd_ref[0, e], preferred_element_type=F32)
    keep = jnp.sum(jnp.where(lane == RT_KEEP_LANE, rt, 0.0), axis=-1, keepdims=True)
    ybuf[slot] = _layer_norm(DN_ALPHA * x + acc, lg_ref[...], lb_ref[...]) * keep

    @pl.when(i == last)
    def _():
        every_row(lambda r: row_in(i + 1, other, r).wait())
        every_row(lambda r: row_out(i, other, r).wait())
        every_row(lambda r: row_out(i + 1, slot, r).start())
        every_row(lambda r: row_out(i + 1, slot, r).wait())


def _moe_grouped(x, p, lg, lb, seq_pad):
    m = x.shape[0]
    tm = MOE_SORT_TILE
    xg = _moe_route(x, p, _row_tile(m), seq_pad)
    gid = xg[:, D_MODEL + RT_GROUP_LANE].astype(I32)
    onehot = (gid[:, None] == jnp.arange(N_GROUPS, dtype=I32)[None, :]).astype(I32)
    cnt = jnp.sum(onehot, axis=0)
    rank = jnp.take_along_axis(jnp.cumsum(onehot, axis=0) - onehot, gid[:, None], axis=1)[:, 0]
    seg = ((cnt + tm - 1) // tm) * tm
    off = jnp.cumsum(seg) - seg
    n_tiles = -(-m // tm) + N_GROUPS
    dest = off[gid] + rank
    tokens = jnp.arange(m, dtype=I32)
    src = jnp.zeros(((n_tiles + 1) * tm,), I32).at[dest].set(tokens)
    pos = jnp.arange(-tm, n_tiles * tm, dtype=I32)
    dst = (m + pos % tm).at[dest + tm].set(tokens)
    tile_start = jnp.arange(n_tiles, dtype=I32) * tm
    tg = jnp.sum((tile_start[:, None] >= off[None, 1:]).astype(I32), axis=1)

    wgu = p["wgu"].reshape(N_GROUPS, EXPERTS_PER_GROUP, D_MODEL, 2 * EXPERT_FF)
    wd = p["wd"].reshape(N_GROUPS, EXPERTS_PER_GROUP, EXPERT_FF, D_MODEL)
    grid_spec = pltpu.PrefetchScalarGridSpec(
        num_scalar_prefetch=3, grid=(n_tiles,),
        in_specs=[pl.BlockSpec(memory_space=pl.ANY),
                  pl.BlockSpec((1, EXPERTS_PER_GROUP, D_MODEL, 2 * EXPERT_FF), lambda i, s, d, g: (g[i], 0, 0, 0)),
                  pl.BlockSpec((1, EXPERTS_PER_GROUP, EXPERT_FF, D_MODEL), lambda i, s, d, g: (g[i], 0, 0, 0)),
                  pl.BlockSpec(lg.shape, lambda i, s, d, g: (0, 0)),
                  pl.BlockSpec(lb.shape, lambda i, s, d, g: (0, 0))],
        out_specs=pl.BlockSpec(memory_space=pl.ANY),
        scratch_shapes=[pltpu.VMEM((2, tm, XG_COLS), F32), pltpu.VMEM((2, tm, D_MODEL), F32),
                        pltpu.SemaphoreType.DMA((2,)), pltpu.SemaphoreType.DMA((2,))])
    out = pl.pallas_call(
        functools.partial(_moe_grouped_kernel, tm=tm),
        grid_spec=grid_spec,
        out_shape=jax.ShapeDtypeStruct((m + tm, D_MODEL), F32),
        compiler_params=_cparams(("arbitrary",)),
        name="moe_grouped",
    )(src, dst, tg, xg, wgu, wd, lg, lb)
    return out[:m]


C_HIST = 32
D_HIST = 8


def _softplus(x):
    return jnp.maximum(x, 0.0) + jnp.log1p(jnp.exp(-jnp.abs(x)))


def _gelu_tanh(x):
    return 0.5 * x * (1.0 + jnp.tanh(math.sqrt(2.0 / math.pi) * (x + 0.044715 * (x * x * x))))


def _lru_gates(xc, wab_ref, ba, bx, lam):
    proj = jnp.dot(xc.astype(BF16), wab_ref[...], preferred_element_type=F32)
    r = _sigmoid(proj[:, :D_WIDTH] + ba)
    ig = _sigmoid(proj[:, D_WIDTH:] + bx)
    log_a = -LRU_C * r * _softplus(-lam)
    a = jnp.exp(log_a)
    th = jnp.tanh(log_a)
    drive = jnp.sqrt(jnp.maximum(-2.0 * th / (1.0 - th), 0.0)) * ig * xc
    return a, drive


def _odd_seq_kernel(ca_ref, cg_ref, dx_ref, dg_ref, cw_ref, cb_ref, lng_ref, lnb_ref, dw_ref, db_ref,
                    wab_ref, ba_ref, bx_ref, lam_ref,
                    y_ref, cst_ref, dst_ref, hst_ref, uext, dext, hc):
    c = pl.program_id(1)

    @pl.when(c == 0)
    def _():
        uext[:C_HIST, :] = jnp.zeros((C_HIST, C_WIDTH), F32)
        dext[:D_HIST, :] = jnp.zeros((D_HIST, D_WIDTH), F32)
        hc[...] = jnp.zeros_like(hc)

    uext[C_HIST:, :] = ca_ref[...] * _sigmoid(cg_ref[...])
    acc = jnp.zeros((ROW_TILE, C_WIDTH), F32)
    for j in range(C_CONV):
        off = C_HIST - (C_CONV - 1) + j
        acc = acc + cw_ref[j:j + 1, :] * uext[off:off + ROW_TILE, :]
    yc = _layer_norm(acc + cb_ref[...], lng_ref[...], lnb_ref[...])
    y_ref[:, :C_WIDTH] = yc * _sigmoid(yc)

    dext[D_HIST:, :] = dx_ref[...]
    xc = jnp.zeros((ROW_TILE, D_WIDTH), F32)
    for j in range(D_CONV):
        off = D_HIST - (D_CONV - 1) + j
        xc = xc + dw_ref[j:j + 1, :] * dext[off:off + ROW_TILE, :]
    xc = xc + db_ref[...]
    a, u = _lru_gates(xc, wab_ref, ba_ref[...], bx_ref[...], lam_ref[...])
    rows = lax.broadcasted_iota(I32, (ROW_TILE, 1), 0)
    u = jnp.where(jnp.logical_and(c == 0, rows < PAD_ROWS), 0.0, u)
    d = 1
    while d < ROW_TILE:
        head = rows < d
        a_prev = jnp.where(head, 1.0, pltpu.roll(a, d, axis=0))
        u_prev = jnp.where(head, 0.0, pltpu.roll(u, d, axis=0))
        u = u + a * u_prev
        a = a * a_prev
        d *= 2
    hs = a * hc[...] + u
    y_ref[:, C_WIDTH:] = hs * _gelu_tanh(dg_ref[...])

    hc[...] = hs[ROW_TILE - 1:ROW_TILE, :]
    uext[:C_HIST, :] = uext[ROW_TILE:ROW_TILE + C_HIST, :]
    dext[:D_HIST, :] = dext[ROW_TILE:ROW_TILE + D_HIST, :]

    @pl.when(c == pl.num_programs(1) - 1)
    def _():
        cst_ref[0] = uext[C_HIST + ROW_TILE - (C_CONV - 1):C_HIST + ROW_TILE, :]
        dst_ref[0] = dext[D_HIST + ROW_TILE - (D_CONV - 1):D_HIST + ROW_TILE, :]
        hst_ref[0] = hs[ROW_TILE - 1:ROW_TILE, :]


def _odd_seq(h, p, n_batch, n_chunks):
    m = h.shape[0]
    blk = lambda cb: pl.BlockSpec((ROW_TILE, 512), lambda b, c: (b * n_chunks + c, cb))
    full = lambda a: pl.BlockSpec(a.shape, lambda b, c: (0,) * a.ndim)
    params = [p["cw"], p["cb"], p["lng"], p["lnb"], p["dw"], p["db"], p["wab"], p["ba"], p["bx"], p["lam"]]
    state = lambda r: pl.BlockSpec((1, r, 512), lambda b, c: (b, 0, 0))
    return pl.pallas_call(
        _odd_seq_kernel,
        grid=(n_batch, n_chunks),
        in_specs=[blk(0), blk(1), blk(2), blk(3)] + [full(a) for a in params],
        out_specs=[pl.BlockSpec((ROW_TILE, 1024), lambda b, c: (b * n_chunks + c, 0)),
                   state(C_CONV - 1), state(D_CONV - 1), state(1)],
        out_shape=[jax.ShapeDtypeStruct((m, C_WIDTH + D_WIDTH), F32),
                   jax.ShapeDtypeStruct((n_batch, C_CONV - 1, C_WIDTH), F32),
                   jax.ShapeDtypeStruct((n_batch, D_CONV - 1, D_WIDTH), F32),
                   jax.ShapeDtypeStruct((n_batch, 1, D_WIDTH), F32)],
        scratch_shapes=[pltpu.VMEM((C_HIST + ROW_TILE, C_WIDTH), F32),
                        pltpu.VMEM((D_HIST + ROW_TILE, D_WIDTH), F32),
                        pltpu.VMEM((1, D_WIDTH), F32)],
        compiler_params=_cparams(("parallel", "arbitrary")),
        name="odd_seq",
    )(h, h, h, h, *params)


def _pack_odd_params(cw, cb, lng, lnb, dw, db, wa, ba, wx, bx, lam):
    def block_diag(w):
        out = jnp.zeros((D_WIDTH, D_WIDTH), w.dtype)
        for n in range(D_BLOCKS):
            out = out.at[n * D_BLOCK_W:(n + 1) * D_BLOCK_W, n * D_BLOCK_W:(n + 1) * D_BLOCK_W].set(w[n])
        return out
    row = lambda v: v.reshape(1, -1).astype(F32)
    return dict(cw=cw.astype(F32), cb=row(cb), lng=row(lng), lnb=row(lnb), dw=dw.astype(F32), db=row(db),
                wab=jnp.concatenate([block_diag(wa), block_diag(wx)], axis=1).astype(BF16),
                ba=row(ba), bx=row(bx), lam=row(lam))


def _odd_sample_kernel(ca_ref, cg_ref, dx_ref, dg_ref, cs_ref, ds_ref, h0_ref,
                       cw_ref, cb_ref, lng_ref, lnb_ref, dw_ref, db_ref, wab_ref, ba_ref, bx_ref, lam_ref,
                       y_ref, cso_ref, dso_ref, ho_ref):
    u = ca_ref[...] * _sigmoid(cg_ref[...])
    acc = cw_ref[C_CONV - 1:C_CONV, :] * u
    for j in range(C_CONV - 1):
        acc = acc + cw_ref[j:j + 1, :] * cs_ref[j]
        if j > 0:
            cso_ref[j - 1] = cs_ref[j]
    cso_ref[C_CONV - 2] = u
    yc = _layer_norm(acc + cb_ref[...], lng_ref[...], lnb_ref[...])
    y_ref[:, :C_WIDTH] = yc * _sigmoid(yc)

    dx = dx_ref[...]
    xc = dw_ref[D_CONV - 1:D_CONV, :] * dx
    for j in range(D_CONV - 1):
        xc = xc + dw_ref[j:j + 1, :] * ds_ref[j]
        if j > 0:
            dso_ref[j - 1] = ds_ref[j]
    dso_ref[D_CONV - 2] = dx
    xc = xc + db_ref[...]
    a, drive = _lru_gates(xc, wab_ref, ba_ref[...], bx_ref[...], lam_ref[...])
    h = a * h0_ref[...] + drive
    ho_ref[...] = h
    y_ref[:, C_WIDTH:] = h * _gelu_tanh(dg_ref[...])


def _odd_sample(h, cs_t, ds_t, h0, p):
    db = h.shape[0]
    params = [p["cw"], p["cb"], p["lng"], p["lnb"], p["dw"], p["db"], p["wab"], p["ba"], p["bx"], p["lam"]]
    full = lambda a: pl.BlockSpec(a.shape, lambda i: (0,) * a.ndim)
    blk = lambda cb: pl.BlockSpec((db, 512), lambda i: (0, cb))
    return pl.pallas_call(
        _odd_sample_kernel,
        grid=(1,),
        in_specs=[blk(0), blk(1), blk(2), blk(3), full(cs_t), full(ds_t), full(h0)] + [full(a) for a in params],
        out_specs=[pl.BlockSpec((db, 1024), lambda i: (0, 0)), full(cs_t), full(ds_t), full(h0)],
        out_shape=[jax.ShapeDtypeStruct((db, C_WIDTH + D_WIDTH), F32),
                   jax.ShapeDtypeStruct(cs_t.shape, F32), jax.ShapeDtypeStruct(ds_t.shape, F32),
                   jax.ShapeDtypeStruct(h0.shape, F32)],
        compiler_params=_cparams(("arbitrary",)),
        name="odd_sample",
    )(h, h, h, h, cs_t, ds_t, h0, *params)


def _mm_postnorm_kernel(a_ref, x_ref, w_ref, lg_ref, lb_ref, o_ref, *, tm, seq_pad):
    acc = jnp.dot(a_ref[...].astype(BF16), w_ref[...], preferred_element_type=F32)
    y = DN_ALPHA * x_ref[...] + acc
    o_ref[...] = _finish_rows(y, lg_ref[...], lb_ref[...], pl.program_id(0), tm, seq_pad)


def _mm_postnorm(a, x, w, lg, lb, tm, seq_pad):
    m = x.shape[0]
    row = lambda w_: pl.BlockSpec((tm, w_), lambda i: (i, 0))
    full = lambda arr: pl.BlockSpec(arr.shape, lambda i: (0,) * arr.ndim)
    return pl.pallas_call(
        functools.partial(_mm_postnorm_kernel, tm=tm, seq_pad=seq_pad),
        grid=(m // tm,),
        in_specs=[row(a.shape[1]), row(D_MODEL), full(w), full(lg), full(lb)],
        out_specs=row(D_MODEL),
        out_shape=jax.ShapeDtypeStruct((m, D_MODEL), F32),
        compiler_params=_cparams(("parallel",)),
        name="mm_postnorm",
    )(a, x, w, lg, lb)


def _row_tile(m):
    for tm in (512, 256, ROW_TILE):
        if m % tm == 0:
            return tm
    return m


def _moe_tile(m):
    for tm in (1024, 512, 256, ROW_TILE):
        if m % tm == 0:
            return tm
    return m


def _even_sample_attention(hs, qbx, qix, kvf, kib, cache_k2, cache_v2, cache_kidx2, page_table, layer):
    db, n_pages = page_table.shape
    past = n_pages * PAGE_SIZE
    n_keys = past + 1
    n_cols = -(-(n_keys) // LANES) * LANES
    qi = qix.reshape(db, IDX_HEADS, 2, IDX_DIM)
    qi = jnp.stack([qi[:, h, h % 2] for h in range(IDX_HEADS)], axis=1)
    zeros = jnp.zeros_like(qi)
    q_is = jnp.concatenate([qi, zeros] if layer == 0 else [zeros, qi], axis=-1)
    w_ib = jnp.broadcast_to(hs[:, EV_WI:EV_WI + IDX_HEADS, None], (db, IDX_HEADS, LANES))
    scores = _dsa_sample_scores(page_table, q_is, w_ib, qix.reshape(db, IDX_HEADS, LANES),
                                kib.reshape(db, 1, LANES), cache_kidx2, n_cols)
    mask = _dsa_sample_select(scores.reshape(db, n_cols), n_keys)
    o8 = _dsa_sample_attend(page_table, qbx.reshape(db, B_HEADS, LANES), mask.reshape(db, 1, n_cols),
                            kvf.reshape(db, 1, 2 * LANES), cache_k2, cache_v2, layer)
    per = B_HEADS // B_KV_HEADS
    halves = [o8[:, h, (h // per) * B_HEAD_DIM:(h // per + 1) * B_HEAD_DIM] for h in range(B_HEADS)]
    return jnp.concatenate(halves, axis=-1)


def kernel(x_prompt, x_sample, cache_k, cache_v, cache_kidx, state_hgrn, state_conv_c, state_conv_d, state_lru,
           page_table, meta_tokens, w_in_even, w_out_even, hgrn_lb_logits, hgrn_norm_g, w_in_odd, w_out_odd,
           conv_c_w, conv_c_b, conv_c_ln_g, conv_c_ln_b, conv_d_w, conv_d_b, lru_wa, lru_ba, lru_wx, lru_bx,
           lru_lambda, ln1_g, ln1_b, ln2_g, ln2_b, router_g_w, router_g_b, router_e_w, router_e_b,
           w_gate, w_up, w_down):
    bsz, seq, _ = x_prompt.shape
    dbsz, dseq, _ = x_sample.shape
    assert dseq == 1 and seq % ROW_TILE == 0 and dbsz % HS_TB == 0
    t_real = N_META + seq
    n_chunks = (PAD_ROWS + t_real) // ROW_TILE
    t_pad = n_chunks * ROW_TILE
    n_phys = cache_k.shape[0]
    past_len = page_table.shape[1] * PAGE_SIZE
    n_even = cache_k.shape[2]

    meta = jnp.broadcast_to(meta_tokens.astype(F32)[None], (bsz, N_META, D_MODEL))
    xp = jnp.concatenate([jnp.zeros((bsz, PAD_ROWS, D_MODEL), F32), meta, x_prompt.astype(F32)], axis=1)
    xp = xp.reshape(bsz * t_pad, D_MODEL)
    xs = x_sample.reshape(dbsz, D_MODEL).astype(F32)
    tm_p, tm_s = _row_tile(bsz * t_pad), _row_tile(dbsz)

    cos_p, sin_p = _rope_tables(np.maximum(np.arange(t_pad) - PAD_ROWS, 0))
    cos_p, sin_p = jnp.tile(jnp.asarray(cos_p), (bsz, 1)), jnp.tile(jnp.asarray(sin_p), (bsz, 1))
    cos_s, sin_s = _rope_tables(np.full((dbsz,), past_len))
    cos_s, sin_s = jnp.asarray(cos_s), jnp.asarray(sin_s)

    sm = jax.nn.softmax(hgrn_lb_logits.astype(F32), axis=0)
    lower_bounds = jnp.cumsum(sm, axis=0) - sm[0]

    cache_k2 = cache_k.reshape(n_phys, PAGE_SIZE, n_even * B_KV_WIDTH)
    cache_v2 = cache_v.reshape(n_phys, PAGE_SIZE, n_even * B_KV_WIDTH)
    cache_kidx2 = cache_kidx.reshape(n_phys, PAGE_SIZE, n_even * IDX_DIM)
    assert n_even * IDX_DIM == LANES and B_KV_WIDTH == LANES

    row2 = lambda v: v.reshape(1, -1).astype(F32)
    unpad = lambda a, w: a.reshape(bsz, t_pad, w)[:, PAD_ROWS:]
    kp, vp, ip, hp, cp, dp, lp = [], [], [], [], [], [], []
    ks, vs, iks, hsm, csm, dsm, lsm = [], [], [], [], [], [], []
    for layer in range(DEPTH):
        li = layer // 2
        lg1, lb1 = row2(ln1_g[layer]), row2(ln1_b[layer])
        if layer % 2 == 0:
            w_in = _pack_even_weight(w_in_even[li])
            w_out = w_out_even[li].astype(BF16)
            lb = lower_bounds[li].reshape(1, A_WIDTH)
            ng = row2(hgrn_norm_g[li])
            h = _matmul(xp, w_in, tm_p, 512)
            qbx, qix, kvf, kvb, kif, kib = _even_post(h, cos_p, sin_p, tm_p)
            oa, s_p = _hgrn_prompt(h, lb, bsz, n_chunks)
            ob = _dsa_prompt(qbx, qix, h, kvb, kib, bsz, n_chunks)
            xp = _even_out(oa, h, ob, xp, w_out, ng, lg1, lb1, tm_p, t_pad)
            kp.append(unpad(kvf[:, :LANES], LANES).reshape(bsz, t_real, B_KV_HEADS, B_HEAD_DIM))
            vp.append(unpad(kvf[:, LANES:], LANES).reshape(bsz, t_real, B_KV_HEADS, B_HEAD_DIM))
            ip.append(unpad(kif[:, :IDX_DIM], IDX_DIM))
            hp.append(s_p)
            h = _matmul(xs, w_in, tm_s, 512)
            qbx, qix, kvf, kvb, kif, kib = _even_post(h, cos_s, sin_s, tm_s)
            oa, s_s = _hgrn_sample(h, lb, state_hgrn, li)
            ob = _even_sample_attention(h, qbx, qix, kvf, kib, cache_k2, cache_v2, cache_kidx2, page_table, li)
            xs = _even_out(oa, h, ob, xs, w_out, ng, lg1, lb1, tm_s, None)
            ks.append(kvf[:, :LANES].reshape(dbsz, 1, B_KV_HEADS, B_HEAD_DIM))
            vs.append(kvf[:, LANES:].reshape(dbsz, 1, B_KV_HEADS, B_HEAD_DIM))
            iks.append(kif[:, :IDX_DIM].reshape(dbsz, 1, IDX_DIM))
            hsm.append(s_s)
        else:
            w_in = w_in_odd[li].astype(BF16)
            w_out = w_out_odd[li].astype(BF16)
            p = _pack_odd_params(conv_c_w[li], conv_c_b[li], conv_c_ln_g[li], conv_c_ln_b[li], conv_d_w[li],
                                 conv_d_b[li], lru_wa[li], lru_ba[li], lru_wx[li], lru_bx[li], lru_lambda[li])
            h = _matmul(xp, w_in, tm_p, 512)
            y, c_p, d_p, h_p = _odd_seq(h, p, bsz, n_chunks)
            xp = _mm_postnorm(y, xp, w_out, lg1, lb1, tm_p, t_pad)
            cp.append(c_p); dp.append(d_p); lp.append(h_p[:, 0])
            h = _matmul(xs, w_in, tm_s, 512)
            y, c_s, d_s, h_s = _odd_sample(h, jnp.swapaxes(state_conv_c[:, li], 0, 1).astype(F32),
                                           jnp.swapaxes(state_conv_d[:, li], 0, 1).astype(F32),
                                           state_lru[:, li].astype(F32), p)
            xs = _mm_postnorm(y, xs, w_out, lg1, lb1, tm_s, None)
            csm.append(jnp.swapaxes(c_s, 0, 1)); dsm.append(jnp.swapaxes(d_s, 0, 1)); lsm.append(h_s)
        mp = _pack_moe_params(router_g_w[layer], router_g_b[layer], router_e_w[layer], router_e_b[layer],
                              w_gate[layer], w_up[layer], w_down[layer])
        lg2, lb2 = row2(ln2_g[layer]), row2(ln2_b[layer])
        xp = _moe_grouped(xp, mp, lg2, lb2, t_pad)
        xs = _moe(xs, mp, lg2, lb2, _moe_tile(dbsz), None)

    y_prompt = xp.reshape(bsz, t_pad, D_MODEL)[:, PAD_ROWS + N_META:]
    y_sample = xs.reshape(dbsz, 1, D_MODEL)
    return (y_prompt, y_sample, jnp.stack(kp, axis=2), jnp.stack(vp, axis=2), jnp.stack(ip, axis=2),
            jnp.stack(hp, axis=1), jnp.stack(cp, axis=1), jnp.stack(dp, axis=1), jnp.stack(lp, axis=1),
            jnp.stack(ks, axis=2), jnp.stack(vs, axis=2), jnp.stack(iks, axis=2), jnp.stack(hsm, axis=1),
            jnp.stack(csm, axis=1), jnp.stack(dsm, axis=1), jnp.stack(lsm, axis=1))
```

```python
import functools
import math

import numpy as np
import jax
import jax.numpy as jnp
from jax import lax
from jax.experimental import pallas as pl
from jax.experimental.pallas import tpu as pltpu

F32 = jnp.float32
BF16 = jnp.bfloat16
I32 = jnp.int32

D_MODEL = 1024
DEPTH = 4
PAGE_SIZE = 128
N_META = 16
A_HEADS = 4
A_DK = 128
A_DV = 128
A_WIDTH = A_HEADS * A_DV
B_HEADS = 8
B_KV_HEADS = 2
B_HEAD_DIM = 64
B_WIDTH = B_HEADS * B_HEAD_DIM
B_KV_WIDTH = B_KV_HEADS * B_HEAD_DIM
IDX_HEADS = 8
IDX_DIM = 64
TOPK_MAX = 256
ROPE_THETA = 10000.0
C_WIDTH = 512
C_CONV = 31
D_WIDTH = 512
D_CONV = 4
D_BLOCKS = 8
D_BLOCK_W = D_WIDTH // D_BLOCKS
LRU_C = 8.0
N_GROUPS = 4
EXPERTS_PER_GROUP = 4
N_EXPERTS = N_GROUPS * EXPERTS_PER_GROUP
TOP_E = 2
EXPERT_FF = 256
DN_ALPHA = (2 * DEPTH) ** 0.25
LN_EPS = 1e-5
RMS_EPS = 1e-6
NEG_BIG = -1e30
LB_TINY = 1e-30

LANES = 128
SUBLANES = 8
ROW_TILE = 128
VMEM_LIMIT_BYTES = 56 * 1024 * 1024

EV_QA, EV_FA, EV_IA, EV_GA = 0, 512, 1024, 1536
EV_QB, EV_QI, EV_KB, EV_VB, EV_KI, EV_WI = 2048, 2560, 3072, 3200, 3328, 3456
EV_COLS = 3584
PAD_ROWS = ROW_TILE - N_META
INT_MIN = -2147483648


def _cparams(sem):
    return pltpu.CompilerParams(dimension_semantics=sem, vmem_limit_bytes=VMEM_LIMIT_BYTES)


def _sigmoid(x):
    return 1.0 / (1.0 + jnp.exp(-x))


def _log_sigmoid(x):
    return jnp.minimum(x, 0.0) - jnp.log1p(jnp.exp(-jnp.abs(x)))


def _layer_norm(y, g, b):
    mu = jnp.mean(y, axis=-1, keepdims=True)
    d = y - mu
    var = jnp.mean(d * d, axis=-1, keepdims=True)
    return d * lax.rsqrt(var + LN_EPS) * g + b


def _keep_rows(tile, tm, seq_pad):
    r = (tile * tm + lax.broadcasted_iota(I32, (tm, 1), 0)).astype(F32)
    pos = r - jnp.floor((r + 0.5) * (1.0 / seq_pad)) * seq_pad
    return jnp.where(pos >= PAD_ROWS, 1.0, 0.0)


def _finish_rows(y, g, b, tile, tm, seq_pad):
    out = _layer_norm(y, g, b)
    return out if seq_pad is None else out * _keep_rows(tile, tm, seq_pad)


def _mm_kernel(x_ref, w_ref, o_ref, *, tn):
    x = x_ref[...].astype(BF16)
    for c in range(0, w_ref.shape[1], tn):
        o_ref[:, c:c + tn] = jnp.dot(x, w_ref[:, c:c + tn], preferred_element_type=F32)


def _matmul(x, w, tm, tn):
    m, k = x.shape
    n = w.shape[1]
    assert m % tm == 0 and n % tn == 0
    return pl.pallas_call(
        functools.partial(_mm_kernel, tn=tn),
        grid=(m // tm,),
        in_specs=[pl.BlockSpec((tm, k), lambda i: (i, 0)),
                  pl.BlockSpec((k, n), lambda i: (0, 0))],
        out_specs=pl.BlockSpec((tm, n), lambda i: (i, 0)),
        out_shape=jax.ShapeDtypeStruct((m, n), F32),
        compiler_params=_cparams(("parallel",)),
        name="matmul",
    )(x, w)


HG_LEVELS = 7


def _hgrn_mid_rows(b_ref, level):
    half = 1 << level
    blk = half * 2
    if blk >= SUBLANES:
        pieces = []
        for start in range(0, ROW_TILE, blk):
            m = start + half - 1
            pieces.append(jnp.broadcast_to(b_ref[m:m + 1, :], (blk, LANES)))
        return pieces[0] if len(pieces) == 1 else jnp.concatenate(pieces, axis=0)
    sub = lax.broadcasted_iota(I32, (SUBLANES, LANES), 0)
    pieces = []
    for start in range(0, ROW_TILE, SUBLANES):
        acc = None
        for off in range(SUBLANES - blk, -1, -blk):
            m = start + off + half - 1
            row = jnp.broadcast_to(b_ref[m:m + 1, :], (SUBLANES, LANES))
            acc = row if acc is None else jnp.where(sub < off + blk, row, acc)
        pieces.append(acc)
    return jnp.concatenate(pieces, axis=0)


def _hgrn_kernel(qa_ref, fa_ref, ia_ref, lb_ref, o_ref, s_ref, st_ref, b_ref):
    c = pl.program_id(1)

    @pl.when(c == 0)
    def _():
        st_ref[...] = jnp.zeros_like(st_ref)

    for h in range(A_HEADS):
        sl = slice(h * A_DK, (h + 1) * A_DK)
        st_new = _hgrn_head_chunk(qa_ref[:, sl], fa_ref[:, sl], ia_ref[:, sl], lb_ref[:, sl],
                                  o_ref.at[:, sl], st_ref.at[h], b_ref.at[h])

        @pl.when(c == pl.num_programs(1) - 1)
        def _():
            s_ref[0, h] = st_new.T


def _hgrn_head_chunk(qa, z, v, lb, o_ref, st_ref, b_ref):
    la = jnp.log(jnp.maximum(lb, LB_TINY))
    lc = jnp.log1p(-lb) + _log_sigmoid(z)
    log_f = jnp.maximum(la, lc) + jnp.log1p(jnp.exp(-jnp.abs(la - lc)))
    k = (1.0 - lb) * _sigmoid(-z)
    q = qa * _sigmoid(qa)

    rows = lax.broadcasted_iota(I32, (ROW_TILE, 1), 0)
    cols = lax.broadcasted_iota(I32, (1, ROW_TILE), 1)

    b = log_f
    for lv in range(HG_LEVELS):
        d = 1 << lv
        b = b + jnp.where(rows >= d, pltpu.roll(b, d, axis=0), 0.0)
    b_ref[...] = b

    scores = jnp.zeros((ROW_TILE, ROW_TILE), F32)
    nt = (((1,), (1,)), ((), ()))
    for lv in range(HG_LEVELS):
        e = jnp.exp(-jnp.abs(b - _hgrn_mid_rows(b_ref, lv)))
        upper = (rows & (1 << lv)) != 0
        qd = jnp.where(upper, q * e, 0.0).astype(BF16)
        kd = jnp.where(upper, 0.0, k * e).astype(BF16)
        s_l = lax.dot_general(qd, kd, nt, preferred_element_type=F32)
        if lv + 1 < HG_LEVELS:
            same = (rows >> (lv + 1)) == (cols >> (lv + 1))
            scores = scores + jnp.where(same, s_l, 0.0)
        else:
            scores = scores + s_l
    diag = jnp.sum(q * k, axis=1, keepdims=True)
    scores = jnp.where(rows == cols, diag, scores)

    st = st_ref[...]
    v_bf = v.astype(BF16)
    o = jnp.dot(scores.astype(BF16), v_bf, preferred_element_type=F32)
    o = o + lax.dot_general((q * jnp.exp(b)).astype(BF16), st.astype(BF16), nt,
                            preferred_element_type=F32)
    o_ref[...] = o

    b_last = b_ref[ROW_TILE - 1:ROW_TILE, :]
    kdl = (k * jnp.exp(b_last - b)).astype(BF16)
    st_new = st * jnp.exp(b_last) + lax.dot_general(v_bf, kdl, (((0,), (0,)), ((), ())),
                                                    preferred_element_type=F32)
    st_ref[...] = st_new
    return st_new


def _hgrn_prompt(h, lb, n_batch, n_chunks):
    m = h.shape[0]
    col = lambda base: (lambda b, c: (b * n_chunks + c, base // A_WIDTH))
    return pl.pallas_call(
        _hgrn_kernel,
        grid=(n_batch, n_chunks),
        in_specs=[pl.BlockSpec((ROW_TILE, A_WIDTH), col(EV_QA)),
                  pl.BlockSpec((ROW_TILE, A_WIDTH), col(EV_FA)),
                  pl.BlockSpec((ROW_TILE, A_WIDTH), col(EV_IA)),
                  pl.BlockSpec((1, A_WIDTH), lambda b, c: (0, 0))],
        out_specs=[pl.BlockSpec((ROW_TILE, A_WIDTH), lambda b, c: (b * n_chunks + c, 0)),
                   pl.BlockSpec((1, A_HEADS, A_DK, A_DV), lambda b, c: (b, 0, 0, 0))],
        out_shape=[jax.ShapeDtypeStruct((m, A_WIDTH), F32),
                   jax.ShapeDtypeStruct((n_batch, A_HEADS, A_DK, A_DV), F32)],
        scratch_shapes=[pltpu.VMEM((A_HEADS, A_DV, A_DK), F32), pltpu.VMEM((A_HEADS, ROW_TILE, LANES), F32)],
        compiler_params=_cparams(("parallel", "arbitrary")),
        name="hgrn_prompt",
    )(h, h, h, lb)


HS_TB = 8


def _hgrn_sample_kernel(qa_ref, fa_ref, ia_ref, lb_ref, s_ref, o_ref, so_ref):
    lb = lb_ref[...]
    z = fa_ref[...]
    la = jnp.log(jnp.maximum(lb, LB_TINY))
    lc = jnp.log1p(-lb) + _log_sigmoid(z)
    f = jnp.exp(jnp.maximum(la, lc) + jnp.log1p(jnp.exp(-jnp.abs(la - lc))))
    k = (1.0 - lb) * _sigmoid(-z)
    qa = qa_ref[...]
    q = qa * _sigmoid(qa)
    v = ia_ref[...]
    pad = jnp.zeros((LANES - 3 * HS_TB, LANES), F32)
    for h in range(A_HEADS):
        sl = slice(h * A_DK, (h + 1) * A_DK)
        cols = jnp.concatenate([q[:, sl], k[:, sl], f[:, sl], pad], axis=0).T
        for b in range(HS_TB):
            qc = cols[:, b:b + 1]
            kc = cols[:, HS_TB + b:HS_TB + b + 1]
            fc = cols[:, 2 * HS_TB + b:2 * HS_TB + b + 1]
            s_new = fc * s_ref[b, 0, h] + kc * v[b:b + 1, sl]
            so_ref[b, h] = s_new
            o_ref[b:b + 1, sl] = jnp.sum(qc * s_new, axis=0, keepdims=True)


def _hgrn_sample(h, lb, state, layer):
    db = h.shape[0]
    row = lambda cb: pl.BlockSpec((HS_TB, A_WIDTH), lambda i: (i, cb))
    return pl.pallas_call(
        _hgrn_sample_kernel,
        grid=(db // HS_TB,),
        in_specs=[row(EV_QA // A_WIDTH), row(EV_FA // A_WIDTH), row(EV_IA // A_WIDTH),
                  pl.BlockSpec((1, A_WIDTH), lambda i: (0, 0)),
                  pl.BlockSpec((HS_TB, 1, A_HEADS, A_DK, A_DV), lambda i: (i, layer, 0, 0, 0))],
        out_specs=[row(0), pl.BlockSpec((HS_TB, A_HEADS, A_DK, A_DV), lambda i: (i, 0, 0, 0))],
        out_shape=[jax.ShapeDtypeStruct((db, A_WIDTH), F32),
                   jax.ShapeDtypeStruct((db, A_HEADS, A_DK, A_DV), F32)],
        compiler_params=_cparams(("parallel",)),
        name="hgrn_sample",
    )(h, h, h, lb, state)


def _pack_even_weight(w):
    parts = (A_HEADS * A_DK, A_HEADS * A_DK, A_WIDTH, A_WIDTH, B_WIDTH, B_KV_WIDTH, B_KV_WIDTH,
             IDX_HEADS * IDX_DIM, IDX_DIM, IDX_HEADS)
    cuts = [int(c) for c in np.cumsum(parts)[:-1]]
    qa, fa, ia, ga, qb, kb, vb, qi, ki, wi = jnp.split(w, cuts, axis=1)
    zeros = lambda n: jnp.zeros((w.shape[0], n), w.dtype)
    out = jnp.concatenate([qa, fa, ia, ga, qb, qi, kb, vb, ki, zeros(LANES - IDX_DIM),
                           wi, zeros(LANES - IDX_HEADS)], axis=1)
    assert out.shape[1] == EV_COLS
    return out.astype(BF16)

def _rope_tables(pos):
    half = B_HEAD_DIM // 2
    lane = np.arange(LANES)
    inv = ROPE_THETA ** (-(lane % half).astype(np.float64) / half)
    ang = np.asarray(pos, np.float64)[:, None] * inv[None, :]
    sign = np.where((lane % B_HEAD_DIM) < half, -1.0, 1.0)
    return np.cos(ang).astype(np.float32), (np.sin(ang) * sign[None, :]).astype(np.float32)


def _rope128(x, cos, sin, first_half):
    rot = jnp.where(first_half, pltpu.roll(x, LANES - 32, axis=1), pltpu.roll(x, 32, axis=1))
    return x * cos + rot * sin


def _even_post_kernel(qb_ref, qi_ref, kv_ref, ki_ref, cos_ref, sin_ref,
                      qbx_ref, qix_ref, kvf_ref, kvb_ref, kif_ref, kib_ref):
    cos = cos_ref[...]
    sin = sin_ref[...]
    lane = lax.broadcasted_iota(I32, (1, LANES), 1)
    first_half = (lane % B_HEAD_DIM) < (B_HEAD_DIM // 2)
    low = lane < B_HEAD_DIM
    scale = B_HEAD_DIM ** -0.5
    for pair in range(B_HEADS // 2):
        sl = slice(pair * LANES, (pair + 1) * LANES)
        qb = _rope128(qb_ref[:, sl], cos, sin, first_half) * scale
        qi = _rope128(qi_ref[:, sl], cos, sin, first_half) * scale
        qb_sw = pltpu.roll(qb, B_HEAD_DIM, axis=1)
        group = (2 * pair) // (B_HEADS // B_KV_HEADS)
        for sub in range(2):
            h = 2 * pair + sub
            src = qb if sub == group else qb_sw
            keep = low if group == 0 else jnp.logical_not(low)
            qbx_ref[:, h * LANES:(h + 1) * LANES] = jnp.where(keep, src, 0.0).astype(BF16)
            keep_i = low if sub == 0 else jnp.logical_not(low)
            qix_ref[:, h * LANES:(h + 1) * LANES] = jnp.where(keep_i, qi, 0.0).astype(BF16)
    k = _rope128(kv_ref[:, :LANES], cos, sin, first_half)
    v = kv_ref[:, LANES:]
    kvf_ref[:, :LANES] = k
    kvf_ref[:, LANES:] = v
    kvb_ref[:, :LANES] = k.astype(BF16)
    kvb_ref[:, LANES:] = v.astype(BF16)
    ki = _rope128(ki_ref[...], cos, sin, first_half)
    kif_ref[...] = ki
    kib_ref[...] = (ki + pltpu.roll(ki, B_HEAD_DIM, axis=1)).astype(BF16)


def _even_post(h, cos, sin, tm):
    m = h.shape[0]
    row = lambda w, cb: pl.BlockSpec((tm, w), lambda i: (i, cb))
    return pl.pallas_call(
        _even_post_kernel,
        grid=(m // tm,),
        in_specs=[row(512, EV_QB // 512), row(512, EV_QI // 512), row(256, EV_KB // 256),
                  row(LANES, EV_KI // LANES), row(LANES, 0), row(LANES, 0)],
        out_specs=[row(1024, 0), row(1024, 0), row(256, 0), row(256, 0), row(LANES, 0), row(LANES, 0)],
        out_shape=[jax.ShapeDtypeStruct((m, 1024), BF16), jax.ShapeDtypeStruct((m, 1024), BF16),
                   jax.ShapeDtypeStruct((m, 256), F32), jax.ShapeDtypeStruct((m, 256), BF16),
                   jax.ShapeDtypeStruct((m, LANES), F32), jax.ShapeDtypeStruct((m, LANES), BF16)],
        compiler_params=_cparams(("parallel",)),
        name="even_post",
    )(h, h, h, h, cos, sin)


def _ordinal_to_f32(k):
    return pltpu.bitcast(jnp.where(k < 0, k ^ 0x7FFFFFFF, k), F32)


def _kth_largest(count_ge, n_sel, shape):
    n_f = float(n_sel)
    base = jnp.where(count_ge(jnp.zeros(shape, F32)) >= n_f, 0, INT_MIN).astype(I32)

    def bit_step(i, base):
        cand = base | (jnp.int32(1) << (30 - i))
        return jnp.where(count_ge(_ordinal_to_f32(cand)) >= n_f, cand, base)

    return _ordinal_to_f32(lax.fori_loop(0, 31, bit_step, base))


def _tie_cut(count_eq_before, need, n_cols_log2):
    def step(i, c):
        cand = c + (jnp.int32(1) << (n_cols_log2 - 1 - i))
        return jnp.where(count_eq_before(cand) < need, cand, c)
    return lax.fori_loop(0, n_cols_log2, step, jnp.zeros(need.shape, I32))


def _topk_select(score, cols, n_sel, n_cols_log2):
    rows = score.shape[0]
    count = lambda m: jnp.sum(m, axis=1, keepdims=True)
    thr = _kth_largest(lambda t: count(jnp.where(score >= t, 1.0, 0.0)), n_sel, (rows, 1))
    gt = jnp.where(score > thr, 1.0, 0.0)
    eq = jnp.where(score == thr, 1.0, 0.0)
    need = float(n_sel) - count(gt)
    c_all = jnp.full((rows, 1), (1 << n_cols_log2) - 1, I32)
    c_star = lax.cond(jnp.max(count(eq) - need) > 0.0,
                      lambda _: _tie_cut(lambda c: count(jnp.where(cols < c, eq, 0.0)), need, n_cols_log2),
                      lambda _: c_all, 0)
    return gt + jnp.where(cols <= c_star, eq, 0.0)


def _merge_head_pair(a, b, group, low):
    if group == 0:
        return jnp.where(low, a, pltpu.roll(b, B_HEAD_DIM, axis=1))
    return jnp.where(low, pltpu.roll(a, B_HEAD_DIM, axis=1), b)


DSA_KEY_BLOCK = 512
DSA_ATT_BLOCK = 512


def _transpose_bf16(x):
    return x.astype(F32).T.astype(BF16)


def _fold_keys(x, reduce):
    keys, nq = x.shape
    return reduce(reduce(x.reshape(keys // 64, 64, nq), axis=0), axis=0, keepdims=True)


def _dsa_prompt_kernel(qb_ref, qi_ref, wi_ref, kv_ref, ki_ref, o_ref, s_ref, acc_ref, qit_ref, qbt_ref,
                       *, n_seq, n_sel, n_cols_log2):
    kb_w = DSA_KEY_BLOCK
    nq = n_seq * ROW_TILE
    j = pl.program_id(1)
    nk = ((j + 1) * ROW_TILE + kb_w - 1) // kb_w
    per = B_HEADS // B_KV_HEADS
    key_in_block = lax.broadcasted_iota(I32, (kb_w, 1), 0)
    qpos1 = j * ROW_TILE + lax.broadcasted_iota(I32, (1, ROW_TILE), 1) - PAD_ROWS
    qpos = jnp.concatenate([qpos1] * n_seq, axis=1)
    key_sum = lambda a: jnp.sum(a, axis=0, keepdims=True)

    def key_rows(kb):
        return pl.ds(pl.multiple_of(kb * kb_w, kb_w), kb_w)

    w_rows = []
    for b in range(n_seq):
        w_rows.append(wi_ref[b].T * (IDX_HEADS ** -0.5))
        for h in range(IDX_HEADS):
            qit_ref[b, h // 2, :, (h % 2) * ROW_TILE:(h % 2 + 1) * ROW_TILE] = _transpose_bf16(
                qi_ref[b, :, h * LANES:(h + 1) * LANES])
        for h in range(B_HEADS):
            a, i = b * B_KV_HEADS + h // per, h % per
            qbt_ref[a, :, i * ROW_TILE:(i + 1) * ROW_TILE] = _transpose_bf16(qb_ref[b, :, h * LANES:(h + 1) * LANES])

    def score_block(kb, carry):
        kpos = kb * kb_w + key_in_block - PAD_ROWS
        for b in range(n_seq):
            ki = ki_ref[b, key_rows(kb), :]
            acc = jnp.zeros((kb_w, ROW_TILE), F32)
            for pair in range(IDX_HEADS // 2):
                d = jnp.maximum(jnp.dot(ki, qit_ref[b, pair], preferred_element_type=F32), 0.0)
                for sub in range(2):
                    h = 2 * pair + sub
                    acc = acc + w_rows[b][h:h + 1, :] * d[:, sub * ROW_TILE:(sub + 1) * ROW_TILE]
            s = jnp.where(kpos <= qpos1, acc, NEG_BIG)
            s_ref[kb, :, b * ROW_TILE:(b + 1) * ROW_TILE] = jnp.where(kpos >= 0, s, -jnp.inf)
        return carry
    lax.fori_loop(0, nk, score_block, 0)

    n_chain = 4

    def over_blocks(fn):
        def body(kb, a):
            f = fn(kb, s_ref[kb]).reshape(kb_w // (n_chain * SUBLANES), n_chain * SUBLANES, nq)
            return a + jnp.sum(f, axis=0)
        return key_sum(lax.fori_loop(0, nk, body, jnp.zeros((n_chain * SUBLANES, nq), F32)))

    thr = _kth_largest(lambda t: over_blocks(lambda kb, s: jnp.where(s >= t, 1.0, 0.0)), n_sel, (1, nq))
    need = float(n_sel) - over_blocks(lambda kb, s: jnp.where(s > thr, 1.0, 0.0))
    n_eq = over_blocks(lambda kb, s: jnp.where(s == thr, 1.0, 0.0))
    takes_all = qpos < n_sel
    c_all = jnp.full((1, nq), (1 << n_cols_log2) - 1, I32)

    def eq_before(c):
        return over_blocks(lambda kb, s: jnp.where(s == thr, jnp.where(kb * kb_w + key_in_block < c, 1.0, 0.0), 0.0))

    c_star = lax.cond(jnp.max(jnp.where(takes_all, 0.0, n_eq - need)) > 0.0,
                      lambda _: _tie_cut(eq_before, need, n_cols_log2), lambda _: c_all, 0)

    n_att = n_seq * B_KV_HEADS
    acc_ref[...] = jnp.zeros_like(acc_ref)
    sub_w = DSA_ATT_BLOCK
    key_in_sub = key_in_block[:sub_w]

    def attend_block(kb, carry):
        carry = list(carry)
        for sub in range(kb_w // sub_w):
            lo = sub * sub_w
            s = s_ref[kb, lo:lo + sub_w, :]
            kidx = kb * kb_w + lo + key_in_sub
            kpos = kidx - PAD_ROWS
            picked = jnp.where(s > thr, 1.0, jnp.where(s == thr, jnp.where(kidx <= c_star, 1.0, 0.0), 0.0))
            picked = jnp.where(takes_all, 1.0, picked)
            valid = jnp.where(kpos >= 0, jnp.where(kpos <= qpos, picked, 0.0), 0.0)
            sub_rows = pl.ds(pl.multiple_of(kb * kb_w + lo, sub_w), sub_w)
            for b in range(n_seq):
                ok = jnp.concatenate([valid[:, b * ROW_TILE:(b + 1) * ROW_TILE]] * per, axis=1) > 0.5
                v_t = _transpose_bf16(kv_ref[b, sub_rows, LANES:])
                k_blk = kv_ref[b, sub_rows, :LANES]
                for g in range(B_KV_HEADS):
                    a = b * B_KV_HEADS + g
                    m_old, l_old = carry[2 * a], carry[2 * a + 1]
                    logits = jnp.dot(k_blk, qbt_ref[a], preferred_element_type=F32)
                    logits = jnp.where(ok, logits, NEG_BIG)
                    m_new = jnp.maximum(m_old, _fold_keys(logits, jnp.max))
                    alpha = jnp.exp(m_old - m_new)
                    p = jnp.exp(logits - m_new)
                    carry[2 * a] = m_new
                    carry[2 * a + 1] = alpha * l_old + _fold_keys(p, jnp.sum)
                    acc_ref[a] = acc_ref[a] * alpha + jnp.dot(v_t, p.astype(BF16), preferred_element_type=F32)
        return tuple(carry)

    init = []
    for a in range(n_att):
        init += [jnp.full((1, per * ROW_TILE), -jnp.inf, F32), jnp.zeros((1, per * ROW_TILE), F32)]
    stats = lax.fori_loop(0, nk, attend_block, tuple(init))

    low = lax.broadcasted_iota(I32, (1, LANES), 1) < B_HEAD_DIM
    for a in range(n_att):
        b, g = a // B_KV_HEADS, a % B_KV_HEADS
        o_t = acc_ref[a] / stats[2 * a + 1]
        o = [o_t[:, i * ROW_TILE:(i + 1) * ROW_TILE].T for i in range(per)]
        for pair in range(per // 2):
            c = (g * per) // 2 + pair
            o_ref[b, :, c * LANES:(c + 1) * LANES] = _merge_head_pair(o[2 * pair], o[2 * pair + 1], g, low)


def _dsa_prompt(qbx, qix, h, kvb, kib, n_batch, n_chunks):
    t = n_chunks * ROW_TILE
    n_seq = 2 if n_batch % 2 == 0 else 1
    n_kb = -(-t // DSA_KEY_BLOCK)
    tk = n_kb * DSA_KEY_BLOCK
    seq3 = lambda a: a.reshape(n_batch, t, a.shape[-1])
    pad_keys = lambda a: jnp.pad(seq3(a), ((0, 0), (0, tk - t), (0, 0)))
    rowblk = lambda w, cb: pl.BlockSpec((n_seq, ROW_TILE, w), lambda b, j: (b, j, cb))
    out = pl.pallas_call(
        functools.partial(_dsa_prompt_kernel, n_seq=n_seq, n_sel=min(TOPK_MAX, (t - PAD_ROWS) // 4),
                          n_cols_log2=math.ceil(math.log2(tk))),
        grid=(n_batch // n_seq, n_chunks),
        in_specs=[rowblk(1024, 0), rowblk(1024, 0), rowblk(LANES, EV_WI // LANES),
                  pl.BlockSpec((n_seq, tk, 256), lambda b, j: (b, 0, 0)),
                  pl.BlockSpec((n_seq, tk, LANES), lambda b, j: (b, 0, 0))],
        out_specs=rowblk(512, 0),
        out_shape=jax.ShapeDtypeStruct((n_batch, t, B_WIDTH), F32),
        scratch_shapes=[pltpu.VMEM((n_kb, DSA_KEY_BLOCK, n_seq * ROW_TILE), F32),
                        pltpu.VMEM((n_seq * B_KV_HEADS, LANES, (B_HEADS // B_KV_HEADS) * ROW_TILE), F32),
                        pltpu.VMEM((n_seq, IDX_HEADS // 2, LANES, 2 * ROW_TILE), BF16),
                        pltpu.VMEM((n_seq * B_KV_HEADS, LANES, (B_HEADS // B_KV_HEADS) * ROW_TILE), BF16)],
        compiler_params=_cparams(("parallel", "arbitrary")),
        name="dsa_prompt",
    )(seq3(qbx), seq3(qix), seq3(h), pad_keys(kvb), pad_keys(kib))
    return out.reshape(n_batch * t, B_WIDTH)


def _page_specs(n_pages, width, col_block):
    return [pl.BlockSpec((1, PAGE_SIZE, width), functools.partial(lambda i, pt, p: (pt[i, p], 0, col_block), p=p))
            for p in range(n_pages)]


def _dsa_sample_score_kernel(pt_ref, q_ref, w_ref, qx_ref, kx_ref, *refs, n_pages, n_cols):
    del pt_ref
    page_refs, o_ref = refs[:n_pages], refs[n_pages]
    nt = (((1,), (1,)), ((), ()))
    q = q_ref[0]
    w = w_ref[0] * (IDX_HEADS ** -0.5)
    pieces = []
    for p in range(n_pages):
        d = lax.dot_general(q, page_refs[p][0].astype(BF16), nt, preferred_element_type=F32)
        pieces.append(jnp.sum(w * jnp.maximum(d, 0.0), axis=0, keepdims=True))
    d_self = jnp.sum(qx_ref[0].astype(F32) * kx_ref[0].astype(F32), axis=1, keepdims=True)
    s_self = jnp.sum(w[:, :1] * jnp.maximum(d_self, 0.0), axis=0, keepdims=True)
    lane = lax.broadcasted_iota(I32, (1, LANES), 1)
    pieces.append(jnp.where(lane == 0, s_self, 0.0))
    pad = n_cols - (n_pages + 1) * LANES
    if pad:
        pieces.append(jnp.zeros((1, pad), F32))
    o_ref[0] = jnp.concatenate(pieces, axis=1)


def _dsa_sample_scores(page_table, q_is, w_ib, q_ix, k_ib, cache_kidx2, n_cols):
    db, n_pages = page_table.shape
    per_seq = lambda shape: pl.BlockSpec((1,) + shape, lambda i, pt: (i, 0, 0))
    grid_spec = pltpu.PrefetchScalarGridSpec(
        num_scalar_prefetch=1, grid=(db,),
        in_specs=[per_seq((IDX_HEADS, LANES)), per_seq((IDX_HEADS, LANES)), per_seq((IDX_HEADS, LANES)),
                  per_seq((1, LANES))] + _page_specs(n_pages, LANES, 0),
        out_specs=per_seq((1, n_cols)))
    return pl.pallas_call(
        functools.partial(_dsa_sample_score_kernel, n_pages=n_pages, n_cols=n_cols),
        grid_spec=grid_spec,
        out_shape=jax.ShapeDtypeStruct((db, 1, n_cols), F32),
        compiler_params=_cparams(("arbitrary",)),
        name="dsa_sample_scores",
    )(page_table, q_is, w_ib, q_ix, k_ib, *([cache_kidx2] * n_pages))


def _dsa_sample_select_kernel(s_ref, o_ref, *, n_keys, n_sel, n_cols_log2):
    cols = lax.broadcasted_iota(I32, (1, s_ref.shape[1]), 1)
    score = jnp.where(cols < n_keys, s_ref[...], -jnp.inf)
    o_ref[...] = _topk_select(score, cols, n_sel, n_cols_log2)


def _dsa_sample_select(scores, n_keys):
    db, n_cols = scores.shape
    return pl.pallas_call(
        functools.partial(_dsa_sample_select_kernel, n_keys=n_keys, n_sel=min(TOPK_MAX, n_keys // 4),
                          n_cols_log2=math.ceil(math.log2(n_cols))),
        out_shape=jax.ShapeDtypeStruct((db, n_cols), F32),
        compiler_params=pltpu.CompilerParams(vmem_limit_bytes=VMEM_LIMIT_BYTES),
        name="dsa_sample_select",
    )(scores)


def _dsa_sample_attend_kernel(pt_ref, q_ref, m_ref, kn_ref, *refs, n_pages):
    del pt_ref
    k_refs, v_refs, o_ref = refs[:n_pages], refs[n_pages:2 * n_pages], refs[2 * n_pages]
    nt = (((1,), (1,)), ((), ()))
    q = q_ref[0]
    mask = m_ref[0]
    logits = []
    for p in range(n_pages):
        l_p = lax.dot_general(q, k_refs[p][0].astype(BF16), nt, preferred_element_type=F32)
        logits.append(jnp.where(mask[:, p * LANES:(p + 1) * LANES] > 0.5, l_p, NEG_BIG))
    kn = kn_ref[0]
    k_new = kn[:, :LANES].astype(BF16).astype(F32)
    v_new = kn[:, LANES:].astype(BF16).astype(F32)
    l_self = jnp.sum(q.astype(F32) * k_new, axis=1, keepdims=True)
    l_self = jnp.where(mask[:, n_pages * LANES:n_pages * LANES + 1] > 0.5, l_self, NEG_BIG)
    mx = l_self
    for l_p in logits:
        mx = jnp.maximum(mx, jnp.max(l_p, axis=1, keepdims=True))
    p_self = jnp.exp(l_self - mx)
    den = p_self
    acc = p_self * v_new
    for p in range(n_pages):
        w_p = jnp.exp(logits[p] - mx)
        den = den + jnp.sum(w_p, axis=1, keepdims=True)
        acc = acc + jnp.dot(w_p.astype(BF16), v_refs[p][0].astype(BF16), preferred_element_type=F32)
    o_ref[0] = acc / den


def _dsa_sample_attend(page_table, q8, mask, kv_new, cache_k2, cache_v2, layer):
    db, n_pages = page_table.shape
    n_cols = mask.shape[-1]
    per_seq = lambda shape: pl.BlockSpec((1,) + shape, lambda i, pt: (i, 0, 0))
    grid_spec = pltpu.PrefetchScalarGridSpec(
        num_scalar_prefetch=1, grid=(db,),
        in_specs=[per_seq((B_HEADS, LANES)), per_seq((1, n_cols)), per_seq((1, 2 * LANES))]
        + _page_specs(n_pages, LANES, layer) + _page_specs(n_pages, LANES, layer),
        out_specs=per_seq((B_HEADS, LANES)))
    return pl.pallas_call(
        functools.partial(_dsa_sample_attend_kernel, n_pages=n_pages),
        grid_spec=grid_spec,
        out_shape=jax.ShapeDtypeStruct((db, B_HEADS, LANES), F32),
        compiler_params=_cparams(("arbitrary",)),
        name="dsa_sample_attend",
    )(page_table, q8, mask, kv_new, *([cache_k2] * n_pages), *([cache_v2] * n_pages))


def _even_out_kernel(oa_ref, ga_ref, ob_ref, x_ref, w_ref, ng_ref, lg_ref, lb_ref, o_ref, *, tm, seq_pad):
    ng = ng_ref[...]
    acc = jnp.dot(ob_ref[...].astype(BF16), w_ref[A_WIDTH:, :], preferred_element_type=F32)
    for h in range(A_HEADS):
        sl = slice(h * A_DV, (h + 1) * A_DV)
        oa = oa_ref[:, sl]
        oa = oa * lax.rsqrt(jnp.mean(oa * oa, axis=-1, keepdims=True) + RMS_EPS) * ng
        ga = ga_ref[:, sl]
        oa = oa * (ga * _sigmoid(ga))
        acc = acc + jnp.dot(oa.astype(BF16), w_ref[sl, :], preferred_element_type=F32)
    y = DN_ALPHA * x_ref[...] + acc
    o_ref[...] = _finish_rows(y, lg_ref[...], lb_ref[...], pl.program_id(0), tm, seq_pad)


def _even_out(oa, h, ob, x, w, ng, lg, lb, tm, seq_pad):
    m = x.shape[0]
    row = lambda w_, cb: pl.BlockSpec((tm, w_), lambda i: (i, cb))
    full = lambda a: pl.BlockSpec(a.shape, lambda i: (0,) * a.ndim)
    return pl.pallas_call(
        functools.partial(_even_out_kernel, tm=tm, seq_pad=seq_pad),
        grid=(m // tm,),
        in_specs=[row(A_WIDTH, 0), row(A_WIDTH, EV_GA // A_WIDTH), row(B_WIDTH, 0), row(D_MODEL, 0),
                  full(w), full(ng), full(lg), full(lb)],
        out_specs=row(D_MODEL, 0),
        out_shape=jax.ShapeDtypeStruct((m, D_MODEL), F32),
        compiler_params=_cparams(("parallel",)),
        name="even_out",
    )(oa, h, ob, x, w, ng, lg, lb)


RT_EXPERT0 = N_GROUPS


def _lane_argmax(v, lane):
    mx = jnp.max(v, axis=-1, keepdims=True)
    idx = jnp.min(jnp.where(v == mx, lane, float(LANES)), axis=-1, keepdims=True)
    return mx, idx


def _router_gates(x, wrh_ref, wrl_ref, rb_ref):
    xh = x.astype(BF16)
    xl = (x - xh.astype(F32)).astype(BF16)
    logits = (jnp.dot(xh, wrh_ref[...], preferred_element_type=F32)
              + jnp.dot(xl, wrh_ref[...], preferred_element_type=F32)
              + jnp.dot(xh, wrl_ref[...], preferred_element_type=F32)) + rb_ref[...]
    lane_i = lax.broadcasted_iota(I32, logits.shape, 1)
    lane = lane_i.astype(F32)
    neg_inf = -jnp.inf
    g_logits = jnp.where(lane_i < N_GROUPS, logits, neg_inf)
    g_max, g_idx = _lane_argmax(g_logits, lane)
    g_val = 1.0 / jnp.sum(jnp.exp(g_logits - g_max), axis=-1, keepdims=True)
    e_lane = lane_i - RT_EXPERT0
    lane_group = jnp.where(e_lane >= 0, e_lane >> 2, -1)
    lane_group = jnp.where(lane_i < RT_EXPERT0 + N_EXPERTS, lane_group, -1).astype(F32)
    e_logits = jnp.where(lane_group == g_idx, logits, neg_inf)
    e_max, first = _lane_argmax(e_logits, lane)
    p = jnp.exp(e_logits - e_max)
    p = p / jnp.sum(p, axis=-1, keepdims=True)
    p1 = jnp.sum(jnp.where(lane == first, p, 0.0), axis=-1, keepdims=True)
    rest = jnp.where(lane == first, neg_inf, jnp.where(lane_group == g_idx, p, neg_inf))
    p2, second = _lane_argmax(rest, lane)
    scale = g_val / (p1 + p2)
    return jnp.where(lane == first, p1 * scale, jnp.where(lane == second, p2 * scale, 0.0))


MOE_EXPERTS_PER_STEP = EXPERTS_PER_GROUP


def _moe_kernel(x_ref, wrh_ref, wrl_ref, rb_ref, wgu_ref, wd_ref, lg_ref, lb_ref, o_ref, acc_ref, gate_ref, xb_ref,
                *, tm, seq_pad):
    step = pl.program_id(1)

    @pl.when(step == 0)
    def _():
        gate_ref[...] = _router_gates(x_ref[...], wrh_ref, wrl_ref, rb_ref)
        acc_ref[...] = jnp.zeros_like(acc_ref)
        xb_ref[...] = x_ref[...].astype(BF16)

    lane = lax.broadcasted_iota(I32, (1, LANES), 1)
    for j in range(MOE_EXPERTS_PER_STEP):
        e = step * MOE_EXPERTS_PER_STEP + j
        gate = jnp.sum(jnp.where(lane == e + RT_EXPERT0, gate_ref[...], 0.0), axis=-1, keepdims=True)
        hgu = jnp.dot(xb_ref[...], wgu_ref[j], preferred_element_type=F32)
        hg = hgu[:, :EXPERT_FF]
        act = (hg * _sigmoid(hg)) * hgu[:, EXPERT_FF:] * gate
        acc_ref[...] += jnp.dot(act.astype(BF16), wd_ref[j], preferred_element_type=F32)

    @pl.when(step == pl.num_programs(1) - 1)
    def _():
        y = DN_ALPHA * x_ref[...] + acc_ref[...]
        o_ref[...] = _finish_rows(y, lg_ref[...], lb_ref[...], pl.program_id(0), tm, seq_pad)


def _moe(x, p, lg, lb, tm, seq_pad):
    m = x.shape[0]
    eps = MOE_EXPERTS_PER_STEP
    full = lambda a: pl.BlockSpec(a.shape, lambda i, e: (0,) * a.ndim)
    return pl.pallas_call(
        functools.partial(_moe_kernel, tm=tm, seq_pad=seq_pad),
        grid=(m // tm, N_EXPERTS // eps),
        in_specs=[pl.BlockSpec((tm, D_MODEL), lambda i, e: (i, 0)),
                  full(p["wrh"]), full(p["wrl"]), full(p["rb"]),
                  pl.BlockSpec((eps, D_MODEL, 2 * EXPERT_FF), lambda i, e: (e, 0, 0)),
                  pl.BlockSpec((eps, EXPERT_FF, D_MODEL), lambda i, e: (e, 0, 0)),
                  full(lg), full(lb)],
        out_specs=pl.BlockSpec((tm, D_MODEL), lambda i, e: (i, 0)),
        out_shape=jax.ShapeDtypeStruct((m, D_MODEL), F32),
        scratch_shapes=[pltpu.VMEM((tm, D_MODEL), F32), pltpu.VMEM((tm, LANES), F32),
                        pltpu.VMEM((tm, D_MODEL), BF16)],
        compiler_params=_cparams(("parallel", "arbitrary")),
        name="moe",
    )(x, p["wrh"], p["wrl"], p["rb"], p["wgu"], p["wd"], lg, lb)


def _pack_moe_params(rg_w, rg_b, re_w, re_b, w_gate, w_up, w_down):
    d = rg_w.shape[0]
    wr = jnp.concatenate([rg_w, re_w, jnp.zeros((d, LANES - N_GROUPS - N_EXPERTS), F32)], axis=1).astype(F32)
    wrh = wr.astype(BF16)
    wrl = (wr - wrh.astype(F32)).astype(BF16)
    rb = jnp.concatenate([rg_b, re_b, jnp.zeros((LANES - N_GROUPS - N_EXPERTS,), F32)]).reshape(1, LANES)
    return dict(wrh=wrh, wrl=wrl, rb=rb.astype(F32),
                wgu=jnp.concatenate([w_gate, w_up], axis=2).astype(BF16), wd=w_down.astype(BF16))


C_HIST = 32
D_HIST = 8


def _softplus(x):
    return jnp.maximum(x, 0.0) + jnp.log1p(jnp.exp(-jnp.abs(x)))


def _gelu_tanh(x):
    return 0.5 * x * (1.0 + jnp.tanh(math.sqrt(2.0 / math.pi) * (x + 0.044715 * (x * x * x))))


def _lru_gates(xc, wab_ref, ba, bx, lam):
    proj = jnp.dot(xc.astype(BF16), wab_ref[...], preferred_element_type=F32)
    r = _sigmoid(proj[:, :D_WIDTH] + ba)
    ig = _sigmoid(proj[:, D_WIDTH:] + bx)
    log_a = -LRU_C * r * _softplus(-lam)
    a = jnp.exp(log_a)
    th = jnp.tanh(log_a)
    drive = jnp.sqrt(jnp.maximum(-2.0 * th / (1.0 - th), 0.0)) * ig * xc
    return a, drive


def _odd_seq_kernel(ca_ref, cg_ref, dx_ref, dg_ref, cw_ref, cb_ref, lng_ref, lnb_ref, dw_ref, db_ref,
                    wab_ref, ba_ref, bx_ref, lam_ref,
                    y_ref, cst_ref, dst_ref, hst_ref, uext, dext, hc):
    c = pl.program_id(1)

    @pl.when(c == 0)
    def _():
        uext[:C_HIST, :] = jnp.zeros((C_HIST, C_WIDTH), F32)
        dext[:D_HIST, :] = jnp.zeros((D_HIST, D_WIDTH), F32)
        hc[...] = jnp.zeros_like(hc)

    uext[C_HIST:, :] = ca_ref[...] * _sigmoid(cg_ref[...])
    acc = jnp.zeros((ROW_TILE, C_WIDTH), F32)
    for j in range(C_CONV):
        off = C_HIST - (C_CONV - 1) + j
        acc = acc + cw_ref[j:j + 1, :] * uext[off:off + ROW_TILE, :]
    yc = _layer_norm(acc + cb_ref[...], lng_ref[...], lnb_ref[...])
    y_ref[:, :C_WIDTH] = yc * _sigmoid(yc)

    dext[D_HIST:, :] = dx_ref[...]
    xc = jnp.zeros((ROW_TILE, D_WIDTH), F32)
    for j in range(D_CONV):
        off = D_HIST - (D_CONV - 1) + j
        xc = xc + dw_ref[j:j + 1, :] * dext[off:off + ROW_TILE, :]
    xc = xc + db_ref[...]
    a, u = _lru_gates(xc, wab_ref, ba_ref[...], bx_ref[...], lam_ref[...])
    rows = lax.broadcasted_iota(I32, (ROW_TILE, 1), 0)
    u = jnp.where(jnp.logical_and(c == 0, rows < PAD_ROWS), 0.0, u)
    d = 1
    while d < ROW_TILE:
        head = rows < d
        a_prev = jnp.where(head, 1.0, pltpu.roll(a, d, axis=0))
        u_prev = jnp.where(head, 0.0, pltpu.roll(u, d, axis=0))
        u = u + a * u_prev
        a = a * a_prev
        d *= 2
    hs = a * hc[...] + u
    y_ref[:, C_WIDTH:] = hs * _gelu_tanh(dg_ref[...])

    hc[...] = hs[ROW_TILE - 1:ROW_TILE, :]
    uext[:C_HIST, :] = uext[ROW_TILE:ROW_TILE + C_HIST, :]
    dext[:D_HIST, :] = dext[ROW_TILE:ROW_TILE + D_HIST, :]

    @pl.when(c == pl.num_programs(1) - 1)
    def _():
        cst_ref[0] = uext[C_HIST + ROW_TILE - (C_CONV - 1):C_HIST + ROW_TILE, :]
        dst_ref[0] = dext[D_HIST + ROW_TILE - (D_CONV - 1):D_HIST + ROW_TILE, :]
        hst_ref[0] = hs[ROW_TILE - 1:ROW_TILE, :]


def _odd_seq(h, p, n_batch, n_chunks):
    m = h.shape[0]
    blk = lambda cb: pl.BlockSpec((ROW_TILE, 512), lambda b, c: (b * n_chunks + c, cb))
    full = lambda a: pl.BlockSpec(a.shape, lambda b, c: (0,) * a.ndim)
    params = [p["cw"], p["cb"], p["lng"], p["lnb"], p["dw"], p["db"], p["wab"], p["ba"], p["bx"], p["lam"]]
    state = lambda r: pl.BlockSpec((1, r, 512), lambda b, c: (b, 0, 0))
    return pl.pallas_call(
        _odd_seq_kernel,
        grid=(n_batch, n_chunks),
        in_specs=[blk(0), blk(1), blk(2), blk(3)] + [full(a) for a in params],
        out_specs=[pl.BlockSpec((ROW_TILE, 1024), lambda b, c: (b * n_chunks + c, 0)),
                   state(C_CONV - 1), state(D_CONV - 1), state(1)],
        out_shape=[jax.ShapeDtypeStruct((m, C_WIDTH + D_WIDTH), F32),
                   jax.ShapeDtypeStruct((n_batch, C_CONV - 1, C_WIDTH), F32),
                   jax.ShapeDtypeStruct((n_batch, D_CONV - 1, D_WIDTH), F32),
                   jax.ShapeDtypeStruct((n_batch, 1, D_WIDTH), F32)],
        scratch_shapes=[pltpu.VMEM((C_HIST + ROW_TILE, C_WIDTH), F32),
                        pltpu.VMEM((D_HIST + ROW_TILE, D_WIDTH), F32),
                        pltpu.VMEM((1, D_WIDTH), F32)],
        compiler_params=_cparams(("parallel", "arbitrary")),
        name="odd_seq",
    )(h, h, h, h, *params)


def _pack_odd_params(cw, cb, lng, lnb, dw, db, wa, ba, wx, bx, lam):
    def block_diag(w):
        out = jnp.zeros((D_WIDTH, D_WIDTH), w.dtype)
        for n in range(D_BLOCKS):
            out = out.at[n * D_BLOCK_W:(n + 1) * D_BLOCK_W, n * D_BLOCK_W:(n + 1) * D_BLOCK_W].set(w[n])
        return out
    row = lambda v: v.reshape(1, -1).astype(F32)
    return dict(cw=cw.astype(F32), cb=row(cb), lng=row(lng), lnb=row(lnb), dw=dw.astype(F32), db=row(db),
                wab=jnp.concatenate([block_diag(wa), block_diag(wx)], axis=1).astype(BF16),
                ba=row(ba), bx=row(bx), lam=row(lam))


def _odd_sample_kernel(ca_ref, cg_ref, dx_ref, dg_ref, cs_ref, ds_ref, h0_ref,
                       cw_ref, cb_ref, lng_ref, lnb_ref, dw_ref, db_ref, wab_ref, ba_ref, bx_ref, lam_ref,
                       y_ref, cso_ref, dso_ref, ho_ref):
    u = ca_ref[...] * _sigmoid(cg_ref[...])
    acc = cw_ref[C_CONV - 1:C_CONV, :] * u
    for j in range(C_CONV - 1):
        acc = acc + cw_ref[j:j + 1, :] * cs_ref[j]
        if j > 0:
            cso_ref[j - 1] = cs_ref[j]
    cso_ref[C_CONV - 2] = u
    yc = _layer_norm(acc + cb_ref[...], lng_ref[...], lnb_ref[...])
    y_ref[:, :C_WIDTH] = yc * _sigmoid(yc)

    dx = dx_ref[...]
    xc = dw_ref[D_CONV - 1:D_CONV, :] * dx
    for j in range(D_CONV - 1):
        xc = xc + dw_ref[j:j + 1, :] * ds_ref[j]
        if j > 0:
            dso_ref[j - 1] = ds_ref[j]
    dso_ref[D_CONV - 2] = dx
    xc = xc + db_ref[...]
    a, drive = _lru_gates(xc, wab_ref, ba_ref[...], bx_ref[...], lam_ref[...])
    h = a * h0_ref[...] + drive
    ho_ref[...] = h
    y_ref[:, C_WIDTH:] = h * _gelu_tanh(dg_ref[...])


def _odd_sample(h, cs_t, ds_t, h0, p):
    db = h.shape[0]
    params = [p["cw"], p["cb"], p["lng"], p["lnb"], p["dw"], p["db"], p["wab"], p["ba"], p["bx"], p["lam"]]
    full = lambda a: pl.BlockSpec(a.shape, lambda i: (0,) * a.ndim)
    blk = lambda cb: pl.BlockSpec((db, 512), lambda i: (0, cb))
    return pl.pallas_call(
        _odd_sample_kernel,
        grid=(1,),
        in_specs=[blk(0), blk(1), blk(2), blk(3), full(cs_t), full(ds_t), full(h0)] + [full(a) for a in params],
        out_specs=[pl.BlockSpec((db, 1024), lambda i: (0, 0)), full(cs_t), full(ds_t), full(h0)],
        out_shape=[jax.ShapeDtypeStruct((db, C_WIDTH + D_WIDTH), F32),
                   jax.ShapeDtypeStruct(cs_t.shape, F32), jax.ShapeDtypeStruct(ds_t.shape, F32),
                   jax.ShapeDtypeStruct(h0.shape, F32)],
        compiler_params=_cparams(("arbitrary",)),
        name="odd_sample",
    )(h, h, h, h, cs_t, ds_t, h0, *params)


def _mm_postnorm_kernel(a_ref, x_ref, w_ref, lg_ref, lb_ref, o_ref, *, tm, seq_pad):
    acc = jnp.dot(a_ref[...].astype(BF16), w_ref[...], preferred_element_type=F32)
    y = DN_ALPHA * x_ref[...] + acc
    o_ref[...] = _finish_rows(y, lg_ref[...], lb_ref[...], pl.program_id(0), tm, seq_pad)


def _mm_postnorm(a, x, w, lg, lb, tm, seq_pad):
    m = x.shape[0]
    row = lambda w_: pl.BlockSpec((tm, w_), lambda i: (i, 0))
    full = lambda arr: pl.BlockSpec(arr.shape, lambda i: (0,) * arr.ndim)
    return pl.pallas_call(
        functools.partial(_mm_postnorm_kernel, tm=tm, seq_pad=seq_pad),
        grid=(m // tm,),
        in_specs=[row(a.shape[1]), row(D_MODEL), full(w), full(lg), full(lb)],
        out_specs=row(D_MODEL),
        out_shape=jax.ShapeDtypeStruct((m, D_MODEL), F32),
        compiler_params=_cparams(("parallel",)),
        name="mm_postnorm",
    )(a, x, w, lg, lb)


def _row_tile(m):
    for tm in (512, 256, ROW_TILE):
        if m % tm == 0:
            return tm
    return m


def _moe_tile(m):
    for tm in (1024, 512, 256, ROW_TILE):
        if m % tm == 0:
            return tm
    return m


def _even_sample_attention(hs, qbx, qix, kvf, kib, cache_k2, cache_v2, cache_kidx2, page_table, layer):
    db, n_pages = page_table.shape
    past = n_pages * PAGE_SIZE
    n_keys = past + 1
    n_cols = -(-(n_keys) // LANES) * LANES
    qi = qix.reshape(db, IDX_HEADS, 2, IDX_DIM)
    qi = jnp.stack([qi[:, h, h % 2] for h in range(IDX_HEADS)], axis=1)
    zeros = jnp.zeros_like(qi)
    q_is = jnp.concatenate([qi, zeros] if layer == 0 else [zeros, qi], axis=-1)
    w_ib = jnp.broadcast_to(hs[:, EV_WI:EV_WI + IDX_HEADS, None], (db, IDX_HEADS, LANES))
    scores = _dsa_sample_scores(page_table, q_is, w_ib, qix.reshape(db, IDX_HEADS, LANES),
                                kib.reshape(db, 1, LANES), cache_kidx2, n_cols)
    mask = _dsa_sample_select(scores.reshape(db, n_cols), n_keys)
    o8 = _dsa_sample_attend(page_table, qbx.reshape(db, B_HEADS, LANES), mask.reshape(db, 1, n_cols),
                            kvf.reshape(db, 1, 2 * LANES), cache_k2, cache_v2, layer)
    per = B_HEADS // B_KV_HEADS
    halves = [o8[:, h, (h // per) * B_HEAD_DIM:(h // per + 1) * B_HEAD_DIM] for h in range(B_HEADS)]
    return jnp.concatenate(halves, axis=-1)


def kernel(x_prompt, x_sample, cache_k, cache_v, cache_kidx, state_hgrn, state_conv_c, state_conv_d, state_lru,
           page_table, meta_tokens, w_in_even, w_out_even, hgrn_lb_logits, hgrn_norm_g, w_in_odd, w_out_odd,
           conv_c_w, conv_c_b, conv_c_ln_g, conv_c_ln_b, conv_d_w, conv_d_b, lru_wa, lru_ba, lru_wx, lru_bx,
           lru_lambda, ln1_g, ln1_b, ln2_g, ln2_b, router_g_w, router_g_b, router_e_w, router_e_b,
           w_gate, w_up, w_down):
    bsz, seq, _ = x_prompt.shape
    dbsz, dseq, _ = x_sample.shape
    assert dseq == 1 and seq % ROW_TILE == 0 and dbsz % HS_TB == 0
    t_real = N_META + seq
    n_chunks = (PAD_ROWS + t_real) // ROW_TILE
    t_pad = n_chunks * ROW_TILE
    n_phys = cache_k.shape[0]
    past_len = page_table.shape[1] * PAGE_SIZE
    n_even = cache_k.shape[2]

    meta = jnp.broadcast_to(meta_tokens.astype(F32)[None], (bsz, N_META, D_MODEL))
    xp = jnp.concatenate([jnp.zeros((bsz, PAD_ROWS, D_MODEL), F32), meta, x_prompt.astype(F32)], axis=1)
    xp = xp.reshape(bsz * t_pad, D_MODEL)
    xs = x_sample.reshape(dbsz, D_MODEL).astype(F32)
    tm_p, tm_s = _row_tile(bsz * t_pad), _row_tile(dbsz)

    cos_p, sin_p = _rope_tables(np.maximum(np.arange(t_pad) - PAD_ROWS, 0))
    cos_p, sin_p = jnp.tile(jnp.asarray(cos_p), (bsz, 1)), jnp.tile(jnp.asarray(sin_p), (bsz, 1))
    cos_s, sin_s = _rope_tables(np.full((dbsz,), past_len))
    cos_s, sin_s = jnp.asarray(cos_s), jnp.asarray(sin_s)

    sm = jax.nn.softmax(hgrn_lb_logits.astype(F32), axis=0)
    lower_bounds = jnp.cumsum(sm, axis=0) - sm[0]

    cache_k2 = cache_k.reshape(n_phys, PAGE_SIZE, n_even * B_KV_WIDTH)
    cache_v2 = cache_v.reshape(n_phys, PAGE_SIZE, n_even * B_KV_WIDTH)
    cache_kidx2 = cache_kidx.reshape(n_phys, PAGE_SIZE, n_even * IDX_DIM)
    assert n_even * IDX_DIM == LANES and B_KV_WIDTH == LANES

    row2 = lambda v: v.reshape(1, -1).astype(F32)
    unpad = lambda a, w: a.reshape(bsz, t_pad, w)[:, PAD_ROWS:]
    kp, vp, ip, hp, cp, dp, lp = [], [], [], [], [], [], []
    ks, vs, iks, hsm, csm, dsm, lsm = [], [], [], [], [], [], []
    for layer in range(DEPTH):
        li = layer // 2
        lg1, lb1 = row2(ln1_g[layer]), row2(ln1_b[layer])
        if layer % 2 == 0:
            w_in = _pack_even_weight(w_in_even[li])
            w_out = w_out_even[li].astype(BF16)
            lb = lower_bounds[li].reshape(1, A_WIDTH)
            ng = row2(hgrn_norm_g[li])
            h = _matmul(xp, w_in, tm_p, 512)
            qbx, qix, kvf, kvb, kif, kib = _even_post(h, cos_p, sin_p, tm_p)
            oa, s_p = _hgrn_prompt(h, lb, bsz, n_chunks)
            ob = _dsa_prompt(qbx, qix, h, kvb, kib, bsz, n_chunks)
            xp = _even_out(oa, h, ob, xp, w_out, ng, lg1, lb1, tm_p, t_pad)
            kp.append(unpad(kvf[:, :LANES], LANES).reshape(bsz, t_real, B_KV_HEADS, B_HEAD_DIM))
            vp.append(unpad(kvf[:, LANES:], LANES).reshape(bsz, t_real, B_KV_HEADS, B_HEAD_DIM))
            ip.append(unpad(kif[:, :IDX_DIM], IDX_DIM))
            hp.append(s_p)
            h = _matmul(xs, w_in, tm_s, 512)
            qbx, qix, kvf, kvb, kif, kib = _even_post(h, cos_s, sin_s, tm_s)
            oa, s_s = _hgrn_sample(h, lb, state_hgrn, li)
            ob = _even_sample_attention(h, qbx, qix, kvf, kib, cache_k2, cache_v2, cache_kidx2, page_table, li)
            xs = _even_out(oa, h, ob, xs, w_out, ng, lg1, lb1, tm_s, None)
            ks.append(kvf[:, :LANES].reshape(dbsz, 1, B_KV_HEADS, B_HEAD_DIM))
            vs.append(kvf[:, LANES:].reshape(dbsz, 1, B_KV_HEADS, B_HEAD_DIM))
            iks.append(kif[:, :IDX_DIM].reshape(dbsz, 1, IDX_DIM))
            hsm.append(s_s)
        else:
            w_in = w_in_odd[li].astype(BF16)
            w_out = w_out_odd[li].astype(BF16)
            p = _pack_odd_params(conv_c_w[li], conv_c_b[li], conv_c_ln_g[li], conv_c_ln_b[li], conv_d_w[li],
                                 conv_d_b[li], lru_wa[li], lru_ba[li], lru_wx[li], lru_bx[li], lru_lambda[li])
            h = _matmul(xp, w_in, tm_p, 512)
            y, c_p, d_p, h_p = _odd_seq(h, p, bsz, n_chunks)
            xp = _mm_postnorm(y, xp, w_out, lg1, lb1, tm_p, t_pad)
            cp.append(c_p); dp.append(d_p); lp.append(h_p[:, 0])
            h = _matmul(xs, w_in, tm_s, 512)
            y, c_s, d_s, h_s = _odd_sample(h, jnp.swapaxes(state_conv_c[:, li], 0, 1).astype(F32),
                                           jnp.swapaxes(state_conv_d[:, li], 0, 1).astype(F32),
                                           state_lru[:, li].astype(F32), p)
            xs = _mm_postnorm(y, xs, w_out, lg1, lb1, tm_s, None)
            csm.append(jnp.swapaxes(c_s, 0, 1)); dsm.append(jnp.swapaxes(d_s, 0, 1)); lsm.append(h_s)
        mp = _pack_moe_params(router_g_w[layer], router_g_b[layer], router_e_w[layer], router_e_b[layer],
                              w_gate[layer], w_up[layer], w_down[layer])
        lg2, lb2 = row2(ln2_g[layer]), row2(ln2_b[layer])
        xp = _moe(xp, mp, lg2, lb2, _moe_tile(bsz * t_pad), t_pad)
        xs = _moe(xs, mp, lg2, lb2, _moe_tile(dbsz), None)

    y_prompt = xp.reshape(bsz, t_pad, D_MODEL)[:, PAD_ROWS + N_META:]
    y_sample = xs.reshape(dbsz, 1, D_MODEL)
    return (y_prompt, y_sample, jnp.stack(kp, axis=2), jnp.stack(vp, axis=2), jnp.stack(ip, axis=2),
            jnp.stack(hp, axis=1), jnp.stack(cp, axis=1), jnp.stack(dp, axis=1), jnp.stack(lp, axis=1),
            jnp.stack(ks, axis=2), jnp.stack(vs, axis=2), jnp.stack(iks, axis=2), jnp.stack(hsm, axis=1),
            jnp.stack(csm, axis=1), jnp.stack(dsm, axis=1), jnp.stack(lsm, axis=1))
```

```python
import functools
import math

import numpy as np
import jax
import jax.numpy as jnp
from jax import lax
from jax.experimental import pallas as pl
from jax.experimental.pallas import tpu as pltpu

F32 = jnp.float32
BF16 = jnp.bfloat16
I32 = jnp.int32

D_MODEL = 1024
DEPTH = 4
PAGE_SIZE = 128
N_META = 16
A_HEADS = 4
A_DK = 128
A_DV = 128
A_WIDTH = A_HEADS * A_DV
B_HEADS = 8
B_KV_HEADS = 2
B_HEAD_DIM = 64
B_WIDTH = B_HEADS * B_HEAD_DIM
B_KV_WIDTH = B_KV_HEADS * B_HEAD_DIM
IDX_HEADS = 8
IDX_DIM = 64
TOPK_MAX = 256
ROPE_THETA = 10000.0
C_WIDTH = 512
C_CONV = 31
D_WIDTH = 512
D_CONV = 4
D_BLOCKS = 8
D_BLOCK_W = D_WIDTH // D_BLOCKS
LRU_C = 8.0
N_GROUPS = 4
EXPERTS_PER_GROUP = 4
N_EXPERTS = N_GROUPS * EXPERTS_PER_GROUP
TOP_E = 2
EXPERT_FF = 256
DN_ALPHA = (2 * DEPTH) ** 0.25
LN_EPS = 1e-5
RMS_EPS = 1e-6
NEG_BIG = -1e30
LB_TINY = 1e-30

LANES = 128
SUBLANES = 8
ROW_TILE = 128
VMEM_LIMIT_BYTES = 56 * 1024 * 1024

EV_QA, EV_FA, EV_IA, EV_GA = 0, 512, 1024, 1536
EV_QB, EV_QI, EV_KB, EV_VB, EV_KI, EV_WI = 2048, 2560, 3072, 3200, 3328, 3456
EV_COLS = 3584
PAD_ROWS = ROW_TILE - N_META
INT_MIN = -2147483648


def _cparams(sem):
    return pltpu.CompilerParams(dimension_semantics=sem, vmem_limit_bytes=VMEM_LIMIT_BYTES)


def _sigmoid(x):
    return 1.0 / (1.0 + jnp.exp(-x))


def _log_sigmoid(x):
    return jnp.minimum(x, 0.0) - jnp.log1p(jnp.exp(-jnp.abs(x)))


def _layer_norm(y, g, b):
    mu = jnp.mean(y, axis=-1, keepdims=True)
    d = y - mu
    var = jnp.mean(d * d, axis=-1, keepdims=True)
    return d * lax.rsqrt(var + LN_EPS) * g + b


def _keep_rows(tile, tm, seq_pad):
    r = (tile * tm + lax.broadcasted_iota(I32, (tm, 1), 0)).astype(F32)
    pos = r - jnp.floor((r + 0.5) * (1.0 / seq_pad)) * seq_pad
    return jnp.where(pos >= PAD_ROWS, 1.0, 0.0)


def _finish_rows(y, g, b, tile, tm, seq_pad):
    out = _layer_norm(y, g, b)
    return out if seq_pad is None else out * _keep_rows(tile, tm, seq_pad)


def _mm_kernel(x_ref, w_ref, o_ref, *, tn):
    x = x_ref[...].astype(BF16)
    for c in range(0, w_ref.shape[1], tn):
        o_ref[:, c:c + tn] = jnp.dot(x, w_ref[:, c:c + tn], preferred_element_type=F32)


def _matmul(x, w, tm, tn):
    m, k = x.shape
    n = w.shape[1]
    assert m % tm == 0 and n % tn == 0
    return pl.pallas_call(
        functools.partial(_mm_kernel, tn=tn),
        grid=(m // tm,),
        in_specs=[pl.BlockSpec((tm, k), lambda i: (i, 0)),
                  pl.BlockSpec((k, n), lambda i: (0, 0))],
        out_specs=pl.BlockSpec((tm, n), lambda i: (i, 0)),
        out_shape=jax.ShapeDtypeStruct((m, n), F32),
        compiler_params=_cparams(("parallel",)),
        name="matmul",
    )(x, w)


HG_LEVELS = 7


def _hgrn_mid_rows(b_ref, level):
    half = 1 << level
    blk = half * 2
    if blk >= SUBLANES:
        pieces = []
        for start in range(0, ROW_TILE, blk):
            m = start + half - 1
            pieces.append(jnp.broadcast_to(b_ref[m:m + 1, :], (blk, LANES)))
        return pieces[0] if len(pieces) == 1 else jnp.concatenate(pieces, axis=0)
    sub = lax.broadcasted_iota(I32, (SUBLANES, LANES), 0)
    pieces = []
    for start in range(0, ROW_TILE, SUBLANES):
        acc = None
        for off in range(SUBLANES - blk, -1, -blk):
            m = start + off + half - 1
            row = jnp.broadcast_to(b_ref[m:m + 1, :], (SUBLANES, LANES))
            acc = row if acc is None else jnp.where(sub < off + blk, row, acc)
        pieces.append(acc)
    return jnp.concatenate(pieces, axis=0)


def _hgrn_kernel(qa_ref, fa_ref, ia_ref, lb_ref, o_ref, s_ref, st_ref, b_ref):
    c = pl.program_id(1)

    @pl.when(c == 0)
    def _():
        st_ref[...] = jnp.zeros_like(st_ref)

    for h in range(A_HEADS):
        sl = slice(h * A_DK, (h + 1) * A_DK)
        st_new = _hgrn_head_chunk(qa_ref[:, sl], fa_ref[:, sl], ia_ref[:, sl], lb_ref[:, sl],
                                  o_ref.at[:, sl], st_ref.at[h], b_ref.at[h])

        @pl.when(c == pl.num_programs(1) - 1)
        def _():
            s_ref[0, h] = st_new.T


def _hgrn_head_chunk(qa, z, v, lb, o_ref, st_ref, b_ref):
    la = jnp.log(jnp.maximum(lb, LB_TINY))
    lc = jnp.log1p(-lb) + _log_sigmoid(z)
    log_f = jnp.maximum(la, lc) + jnp.log1p(jnp.exp(-jnp.abs(la - lc)))
    k = (1.0 - lb) * _sigmoid(-z)
    q = qa * _sigmoid(qa)

    rows = lax.broadcasted_iota(I32, (ROW_TILE, 1), 0)
    cols = lax.broadcasted_iota(I32, (1, ROW_TILE), 1)

    b = log_f
    for lv in range(HG_LEVELS):
        d = 1 << lv
        b = b + jnp.where(rows >= d, pltpu.roll(b, d, axis=0), 0.0)
    b_ref[...] = b

    scores = jnp.zeros((ROW_TILE, ROW_TILE), F32)
    nt = (((1,), (1,)), ((), ()))
    for lv in range(HG_LEVELS):
        e = jnp.exp(-jnp.abs(b - _hgrn_mid_rows(b_ref, lv)))
        upper = (rows & (1 << lv)) != 0
        qd = jnp.where(upper, q * e, 0.0).astype(BF16)
        kd = jnp.where(upper, 0.0, k * e).astype(BF16)
        s_l = lax.dot_general(qd, kd, nt, preferred_element_type=F32)
        if lv + 1 < HG_LEVELS:
            same = (rows >> (lv + 1)) == (cols >> (lv + 1))
            scores = scores + jnp.where(same, s_l, 0.0)
        else:
            scores = scores + s_l
    diag = jnp.sum(q * k, axis=1, keepdims=True)
    scores = jnp.where(rows == cols, diag, scores)

    st = st_ref[...]
    v_bf = v.astype(BF16)
    o = jnp.dot(scores.astype(BF16), v_bf, preferred_element_type=F32)
    o = o + lax.dot_general((q * jnp.exp(b)).astype(BF16), st.astype(BF16), nt,
                            preferred_element_type=F32)
    o_ref[...] = o

    b_last = b_ref[ROW_TILE - 1:ROW_TILE, :]
    kdl = (k * jnp.exp(b_last - b)).astype(BF16)
    st_new = st * jnp.exp(b_last) + lax.dot_general(v_bf, kdl, (((0,), (0,)), ((), ())),
                                                    preferred_element_type=F32)
    st_ref[...] = st_new
    return st_new


def _hgrn_prompt(h, lb, n_batch, n_chunks):
    m = h.shape[0]
    col = lambda base: (lambda b, c: (b * n_chunks + c, base // A_WIDTH))
    return pl.pallas_call(
        _hgrn_kernel,
        grid=(n_batch, n_chunks),
        in_specs=[pl.BlockSpec((ROW_TILE, A_WIDTH), col(EV_QA)),
                  pl.BlockSpec((ROW_TILE, A_WIDTH), col(EV_FA)),
                  pl.BlockSpec((ROW_TILE, A_WIDTH), col(EV_IA)),
                  pl.BlockSpec((1, A_WIDTH), lambda b, c: (0, 0))],
        out_specs=[pl.BlockSpec((ROW_TILE, A_WIDTH), lambda b, c: (b * n_chunks + c, 0)),
                   pl.BlockSpec((1, A_HEADS, A_DK, A_DV), lambda b, c: (b, 0, 0, 0))],
        out_shape=[jax.ShapeDtypeStruct((m, A_WIDTH), F32),
                   jax.ShapeDtypeStruct((n_batch, A_HEADS, A_DK, A_DV), F32)],
        scratch_shapes=[pltpu.VMEM((A_HEADS, A_DV, A_DK), F32), pltpu.VMEM((A_HEADS, ROW_TILE, LANES), F32)],
        compiler_params=_cparams(("parallel", "arbitrary")),
        name="hgrn_prompt",
    )(h, h, h, lb)


HS_TB = 8


def _hgrn_sample_kernel(qa_ref, fa_ref, ia_ref, lb_ref, s_ref, o_ref, so_ref):
    lb = lb_ref[...]
    z = fa_ref[...]
    la = jnp.log(jnp.maximum(lb, LB_TINY))
    lc = jnp.log1p(-lb) + _log_sigmoid(z)
    f = jnp.exp(jnp.maximum(la, lc) + jnp.log1p(jnp.exp(-jnp.abs(la - lc))))
    k = (1.0 - lb) * _sigmoid(-z)
    qa = qa_ref[...]
    q = qa * _sigmoid(qa)
    v = ia_ref[...]
    pad = jnp.zeros((LANES - 3 * HS_TB, LANES), F32)
    for h in range(A_HEADS):
        sl = slice(h * A_DK, (h + 1) * A_DK)
        cols = jnp.concatenate([q[:, sl], k[:, sl], f[:, sl], pad], axis=0).T
        for b in range(HS_TB):
            qc = cols[:, b:b + 1]
            kc = cols[:, HS_TB + b:HS_TB + b + 1]
            fc = cols[:, 2 * HS_TB + b:2 * HS_TB + b + 1]
            s_new = fc * s_ref[b, 0, h] + kc * v[b:b + 1, sl]
            so_ref[b, h] = s_new
            o_ref[b:b + 1, sl] = jnp.sum(qc * s_new, axis=0, keepdims=True)


def _hgrn_sample(h, lb, state, layer):
    db = h.shape[0]
    row = lambda cb: pl.BlockSpec((HS_TB, A_WIDTH), lambda i: (i, cb))
    return pl.pallas_call(
        _hgrn_sample_kernel,
        grid=(db // HS_TB,),
        in_specs=[row(EV_QA // A_WIDTH), row(EV_FA // A_WIDTH), row(EV_IA // A_WIDTH),
                  pl.BlockSpec((1, A_WIDTH), lambda i: (0, 0)),
                  pl.BlockSpec((HS_TB, 1, A_HEADS, A_DK, A_DV), lambda i: (i, layer, 0, 0, 0))],
        out_specs=[row(0), pl.BlockSpec((HS_TB, A_HEADS, A_DK, A_DV), lambda i: (i, 0, 0, 0))],
        out_shape=[jax.ShapeDtypeStruct((db, A_WIDTH), F32),
                   jax.ShapeDtypeStruct((db, A_HEADS, A_DK, A_DV), F32)],
        compiler_params=_cparams(("parallel",)),
        name="hgrn_sample",
    )(h, h, h, lb, state)


def _pack_even_weight(w):
    parts = (A_HEADS * A_DK, A_HEADS * A_DK, A_WIDTH, A_WIDTH, B_WIDTH, B_KV_WIDTH, B_KV_WIDTH,
             IDX_HEADS * IDX_DIM, IDX_DIM, IDX_HEADS)
    cuts = [int(c) for c in np.cumsum(parts)[:-1]]
    qa, fa, ia, ga, qb, kb, vb, qi, ki, wi = jnp.split(w, cuts, axis=1)
    zeros = lambda n: jnp.zeros((w.shape[0], n), w.dtype)
    out = jnp.concatenate([qa, fa, ia, ga, qb, qi, kb, vb, ki, zeros(LANES - IDX_DIM),
                           wi, zeros(LANES - IDX_HEADS)], axis=1)
    assert out.shape[1] == EV_COLS
    return out.astype(BF16)

def _rope_tables(pos):
    half = B_HEAD_DIM // 2
    lane = np.arange(LANES)
    inv = ROPE_THETA ** (-(lane % half).astype(np.float64) / half)
    ang = np.asarray(pos, np.float64)[:, None] * inv[None, :]
    sign = np.where((lane % B_HEAD_DIM) < half, -1.0, 1.0)
    return np.cos(ang).astype(np.float32), (np.sin(ang) * sign[None, :]).astype(np.float32)


def _rope128(x, cos, sin, first_half):
    rot = jnp.where(first_half, pltpu.roll(x, LANES - 32, axis=1), pltpu.roll(x, 32, axis=1))
    return x * cos + rot * sin


def _even_post_kernel(qb_ref, qi_ref, kv_ref, ki_ref, cos_ref, sin_ref,
                      qbx_ref, qix_ref, kvf_ref, kvb_ref, kif_ref, kib_ref):
    cos = cos_ref[...]
    sin = sin_ref[...]
    lane = lax.broadcasted_iota(I32, (1, LANES), 1)
    first_half = (lane % B_HEAD_DIM) < (B_HEAD_DIM // 2)
    low = lane < B_HEAD_DIM
    scale = B_HEAD_DIM ** -0.5
    for pair in range(B_HEADS // 2):
        sl = slice(pair * LANES, (pair + 1) * LANES)
        qb = _rope128(qb_ref[:, sl], cos, sin, first_half) * scale
        qi = _rope128(qi_ref[:, sl], cos, sin, first_half) * scale
        qb_sw = pltpu.roll(qb, B_HEAD_DIM, axis=1)
        group = (2 * pair) // (B_HEADS // B_KV_HEADS)
        for sub in range(2):
            h = 2 * pair + sub
            src = qb if sub == group else qb_sw
            keep = low if group == 0 else jnp.logical_not(low)
            qbx_ref[:, h * LANES:(h + 1) * LANES] = jnp.where(keep, src, 0.0).astype(BF16)
            keep_i = low if sub == 0 else jnp.logical_not(low)
            qix_ref[:, h * LANES:(h + 1) * LANES] = jnp.where(keep_i, qi, 0.0).astype(BF16)
    k = _rope128(kv_ref[:, :LANES], cos, sin, first_half)
    v = kv_ref[:, LANES:]
    kvf_ref[:, :LANES] = k
    kvf_ref[:, LANES:] = v
    kvb_ref[:, :LANES] = k.astype(BF16)
    kvb_ref[:, LANES:] = v.astype(BF16)
    ki = _rope128(ki_ref[...], cos, sin, first_half)
    kif_ref[...] = ki
    kib_ref[...] = (ki + pltpu.roll(ki, B_HEAD_DIM, axis=1)).astype(BF16)


def _even_post(h, cos, sin, tm):
    m = h.shape[0]
    row = lambda w, cb: pl.BlockSpec((tm, w), lambda i: (i, cb))
    return pl.pallas_call(
        _even_post_kernel,
        grid=(m // tm,),
        in_specs=[row(512, EV_QB // 512), row(512, EV_QI // 512), row(256, EV_KB // 256),
                  row(LANES, EV_KI // LANES), row(LANES, 0), row(LANES, 0)],
        out_specs=[row(1024, 0), row(1024, 0), row(256, 0), row(256, 0), row(LANES, 0), row(LANES, 0)],
        out_shape=[jax.ShapeDtypeStruct((m, 1024), BF16), jax.ShapeDtypeStruct((m, 1024), BF16),
                   jax.ShapeDtypeStruct((m, 256), F32), jax.ShapeDtypeStruct((m, 256), BF16),
                   jax.ShapeDtypeStruct((m, LANES), F32), jax.ShapeDtypeStruct((m, LANES), BF16)],
        compiler_params=_cparams(("parallel",)),
        name="even_post",
    )(h, h, h, h, cos, sin)


def _ordinal_to_f32(k):
    return pltpu.bitcast(jnp.where(k < 0, k ^ 0x7FFFFFFF, k), F32)


def _kth_largest(count_ge, n_sel, shape):
    n_f = float(n_sel)
    base = jnp.where(count_ge(jnp.zeros(shape, F32)) >= n_f, 0, INT_MIN).astype(I32)

    def bit_step(i, base):
        cand = base | (jnp.int32(1) << (30 - i))
        return jnp.where(count_ge(_ordinal_to_f32(cand)) >= n_f, cand, base)

    return _ordinal_to_f32(lax.fori_loop(0, 31, bit_step, base))


def _tie_cut(count_eq_before, need, n_cols_log2):
    def step(i, c):
        cand = c + (jnp.int32(1) << (n_cols_log2 - 1 - i))
        return jnp.where(count_eq_before(cand) < need, cand, c)
    return lax.fori_loop(0, n_cols_log2, step, jnp.zeros(need.shape, I32))


def _topk_select(score, cols, n_sel, n_cols_log2):
    rows = score.shape[0]
    count = lambda m: jnp.sum(m, axis=1, keepdims=True)
    thr = _kth_largest(lambda t: count(jnp.where(score >= t, 1.0, 0.0)), n_sel, (rows, 1))
    gt = jnp.where(score > thr, 1.0, 0.0)
    eq = jnp.where(score == thr, 1.0, 0.0)
    need = float(n_sel) - count(gt)
    c_all = jnp.full((rows, 1), (1 << n_cols_log2) - 1, I32)
    c_star = lax.cond(jnp.max(count(eq) - need) > 0.0,
                      lambda _: _tie_cut(lambda c: count(jnp.where(cols < c, eq, 0.0)), need, n_cols_log2),
                      lambda _: c_all, 0)
    return gt + jnp.where(cols <= c_star, eq, 0.0)


def _merge_head_pair(a, b, group, low):
    if group == 0:
        return jnp.where(low, a, pltpu.roll(b, B_HEAD_DIM, axis=1))
    return jnp.where(low, pltpu.roll(a, B_HEAD_DIM, axis=1), b)


DSA_KEY_BLOCK = 512
DSA_ATT_BLOCK = 512


def _transpose_bf16(x):
    return x.astype(F32).T.astype(BF16)


def _fold_keys(x, reduce):
    keys, nq = x.shape
    return reduce(reduce(x.reshape(keys // 64, 64, nq), axis=0), axis=0, keepdims=True)


def _dsa_prompt_kernel(qb_ref, qi_ref, wi_ref, kv_ref, ki_ref, o_ref, s_ref, acc_ref, qit_ref, qbt_ref,
                       *, n_seq, n_sel, n_cols_log2):
    kb_w = DSA_KEY_BLOCK
    nq = n_seq * ROW_TILE
    j = pl.program_id(1)
    nk = ((j + 1) * ROW_TILE + kb_w - 1) // kb_w
    per = B_HEADS // B_KV_HEADS
    key_in_block = lax.broadcasted_iota(I32, (kb_w, 1), 0)
    qpos1 = j * ROW_TILE + lax.broadcasted_iota(I32, (1, ROW_TILE), 1) - PAD_ROWS
    qpos = jnp.concatenate([qpos1] * n_seq, axis=1)
    key_sum = lambda a: jnp.sum(a, axis=0, keepdims=True)

    def key_rows(kb):
        return pl.ds(pl.multiple_of(kb * kb_w, kb_w), kb_w)

    w_rows = []
    for b in range(n_seq):
        w_rows.append(wi_ref[b].T * (IDX_HEADS ** -0.5))
        for h in range(IDX_HEADS):
            qit_ref[b, h // 2, :, (h % 2) * ROW_TILE:(h % 2 + 1) * ROW_TILE] = _transpose_bf16(
                qi_ref[b, :, h * LANES:(h + 1) * LANES])
        for h in range(B_HEADS):
            a, i = b * B_KV_HEADS + h // per, h % per
            qbt_ref[a, :, i * ROW_TILE:(i + 1) * ROW_TILE] = _transpose_bf16(qb_ref[b, :, h * LANES:(h + 1) * LANES])

    def score_block(kb, carry):
        kpos = kb * kb_w + key_in_block - PAD_ROWS
        for b in range(n_seq):
            ki = ki_ref[b, key_rows(kb), :]
            acc = jnp.zeros((kb_w, ROW_TILE), F32)
            for pair in range(IDX_HEADS // 2):
                d = jnp.maximum(jnp.dot(ki, qit_ref[b, pair], preferred_element_type=F32), 0.0)
                for sub in range(2):
                    h = 2 * pair + sub
                    acc = acc + w_rows[b][h:h + 1, :] * d[:, sub * ROW_TILE:(sub + 1) * ROW_TILE]
            s = jnp.where(kpos <= qpos1, acc, NEG_BIG)
            s_ref[kb, :, b * ROW_TILE:(b + 1) * ROW_TILE] = jnp.where(kpos >= 0, s, -jnp.inf)
        return carry
    lax.fori_loop(0, nk, score_block, 0)

    n_chain = 4

    def over_blocks(fn):
        def body(kb, a):
            f = fn(kb, s_ref[kb]).reshape(kb_w // (n_chain * SUBLANES), n_chain * SUBLANES, nq)
            return a + jnp.sum(f, axis=0)
        return key_sum(lax.fori_loop(0, nk, body, jnp.zeros((n_chain * SUBLANES, nq), F32)))

    thr = _kth_largest(lambda t: over_blocks(lambda kb, s: jnp.where(s >= t, 1.0, 0.0)), n_sel, (1, nq))
    need = float(n_sel) - over_blocks(lambda kb, s: jnp.where(s > thr, 1.0, 0.0))
    n_eq = over_blocks(lambda kb, s: jnp.where(s == thr, 1.0, 0.0))
    takes_all = qpos < n_sel
    c_all = jnp.full((1, nq), (1 << n_cols_log2) - 1, I32)

    def eq_before(c):
        return over_blocks(lambda kb, s: jnp.where(s == thr, jnp.where(kb * kb_w + key_in_block < c, 1.0, 0.0), 0.0))

    c_star = lax.cond(jnp.max(jnp.where(takes_all, 0.0, n_eq - need)) > 0.0,
                      lambda _: _tie_cut(eq_before, need, n_cols_log2), lambda _: c_all, 0)

    n_att = n_seq * B_KV_HEADS
    acc_ref[...] = jnp.zeros_like(acc_ref)
    sub_w = DSA_ATT_BLOCK
    key_in_sub = key_in_block[:sub_w]

    def attend_block(kb, carry):
        carry = list(carry)
        for sub in range(kb_w // sub_w):
            lo = sub * sub_w
            s = s_ref[kb, lo:lo + sub_w, :]
            kidx = kb * kb_w + lo + key_in_sub
            kpos = kidx - PAD_ROWS
            picked = jnp.where(s > thr, 1.0, jnp.where(s == thr, jnp.where(kidx <= c_star, 1.0, 0.0), 0.0))
            picked = jnp.where(takes_all, 1.0, picked)
            valid = jnp.where(kpos >= 0, jnp.where(kpos <= qpos, picked, 0.0), 0.0)
            sub_rows = pl.ds(pl.multiple_of(kb * kb_w + lo, sub_w), sub_w)
            for b in range(n_seq):
                ok = jnp.concatenate([valid[:, b * ROW_TILE:(b + 1) * ROW_TILE]] * per, axis=1) > 0.5
                v_t = _transpose_bf16(kv_ref[b, sub_rows, LANES:])
                k_blk = kv_ref[b, sub_rows, :LANES]
                for g in range(B_KV_HEADS):
                    a = b * B_KV_HEADS + g
                    m_old, l_old = carry[2 * a], carry[2 * a + 1]
                    logits = jnp.dot(k_blk, qbt_ref[a], preferred_element_type=F32)
                    logits = jnp.where(ok, logits, NEG_BIG)
                    m_new = jnp.maximum(m_old, _fold_keys(logits, jnp.max))
                    alpha = jnp.exp(m_old - m_new)
                    p = jnp.exp(logits - m_new)
                    carry[2 * a] = m_new
                    carry[2 * a + 1] = alpha * l_old + _fold_keys(p, jnp.sum)
                    acc_ref[a] = acc_ref[a] * alpha + jnp.dot(v_t, p.astype(BF16), preferred_element_type=F32)
        return tuple(carry)

    init = []
    for a in range(n_att):
        init += [jnp.full((1, per * ROW_TILE), -jnp.inf, F32), jnp.zeros((1, per * ROW_TILE), F32)]
    stats = lax.fori_loop(0, nk, attend_block, tuple(init))

    low = lax.broadcasted_iota(I32, (1, LANES), 1) < B_HEAD_DIM
    for a in range(n_att):
        b, g = a // B_KV_HEADS, a % B_KV_HEADS
        o_t = acc_ref[a] / stats[2 * a + 1]
        o = [o_t[:, i * ROW_TILE:(i + 1) * ROW_TILE].T for i in range(per)]
        for pair in range(per // 2):
            c = (g * per) // 2 + pair
            o_ref[b, :, c * LANES:(c + 1) * LANES] = _merge_head_pair(o[2 * pair], o[2 * pair + 1], g, low)


def _dsa_prompt(qbx, qix, h, kvb, kib, n_batch, n_chunks):
    t = n_chunks * ROW_TILE
    n_seq = 2 if n_batch % 2 == 0 else 1
    n_kb = -(-t // DSA_KEY_BLOCK)
    tk = n_kb * DSA_KEY_BLOCK
    seq3 = lambda a: a.reshape(n_batch, t, a.shape[-1])
    pad_keys = lambda a: jnp.pad(seq3(a), ((0, 0), (0, tk - t), (0, 0)))
    rowblk = lambda w, cb: pl.BlockSpec((n_seq, ROW_TILE, w), lambda b, j: (b, j, cb))
    out = pl.pallas_call(
        functools.partial(_dsa_prompt_kernel, n_seq=n_seq, n_sel=min(TOPK_MAX, (t - PAD_ROWS) // 4),
                          n_cols_log2=math.ceil(math.log2(tk))),
        grid=(n_batch // n_seq, n_chunks),
        in_specs=[rowblk(1024, 0), rowblk(1024, 0), rowblk(LANES, EV_WI // LANES),
                  pl.BlockSpec((n_seq, tk, 256), lambda b, j: (b, 0, 0)),
                  pl.BlockSpec((n_seq, tk, LANES), lambda b, j: (b, 0, 0))],
        out_specs=rowblk(512, 0),
        out_shape=jax.ShapeDtypeStruct((n_batch, t, B_WIDTH), F32),
        scratch_shapes=[pltpu.VMEM((n_kb, DSA_KEY_BLOCK, n_seq * ROW_TILE), F32),
                        pltpu.VMEM((n_seq * B_KV_HEADS, LANES, (B_HEADS // B_KV_HEADS) * ROW_TILE), F32),
                        pltpu.VMEM((n_seq, IDX_HEADS // 2, LANES, 2 * ROW_TILE), BF16),
                        pltpu.VMEM((n_seq * B_KV_HEADS, LANES, (B_HEADS // B_KV_HEADS) * ROW_TILE), BF16)],
        compiler_params=_cparams(("parallel", "arbitrary")),
        name="dsa_prompt",
    )(seq3(qbx), seq3(qix), seq3(h), pad_keys(kvb), pad_keys(kib))
    return out.reshape(n_batch * t, B_WIDTH)


DS_SEQ = 2


def _page_specs(n_seq, n_pages, width, col_block):
    return [pl.BlockSpec((1, PAGE_SIZE, width),
                         functools.partial(lambda i, pt, s, p: (pt[i * n_seq + s, p], 0, col_block), s=s, p=p))
            for s in range(n_seq) for p in range(n_pages)]


def _dsa_sample_score_kernel(pt_ref, q_ref, w_ref, qx_ref, kx_ref, *refs, n_seq, n_pages, n_cols):
    del pt_ref
    page_refs, o_ref = refs[:n_seq * n_pages], refs[n_seq * n_pages]
    nt = (((1,), (1,)), ((), ()))
    lane = lax.broadcasted_iota(I32, (1, LANES), 1)
    for s in range(n_seq):
        q = q_ref[s]
        w = w_ref[s] * (IDX_HEADS ** -0.5)
        pieces = []
        for p in range(n_pages):
            d = lax.dot_general(q, page_refs[s * n_pages + p][0].astype(BF16), nt,
                                preferred_element_type=F32)
            pieces.append(jnp.sum(w * jnp.maximum(d, 0.0), axis=0, keepdims=True))
        d_self = jnp.sum(qx_ref[s].astype(F32) * kx_ref[s].astype(F32), axis=1, keepdims=True)
        s_self = jnp.sum(w[:, :1] * jnp.maximum(d_self, 0.0), axis=0, keepdims=True)
        pieces.append(jnp.where(lane == 0, s_self, 0.0))
        pad = n_cols - (n_pages + 1) * LANES
        if pad:
            pieces.append(jnp.zeros((1, pad), F32))
        o_ref[s] = jnp.concatenate(pieces, axis=1)


def _dsa_sample_scores(page_table, q_is, w_ib, q_ix, k_ib, cache_kidx2, n_cols):
    db, n_pages = page_table.shape
    n_seq = DS_SEQ if db % DS_SEQ == 0 else 1
    per_seq = lambda shape: pl.BlockSpec((n_seq,) + shape, lambda i, pt: (i, 0, 0))
    grid_spec = pltpu.PrefetchScalarGridSpec(
        num_scalar_prefetch=1, grid=(db // n_seq,),
        in_specs=[per_seq((IDX_HEADS, LANES)), per_seq((IDX_HEADS, LANES)), per_seq((IDX_HEADS, LANES)),
                  per_seq((1, LANES))] + _page_specs(n_seq, n_pages, LANES, 0),
        out_specs=per_seq((1, n_cols)))
    return pl.pallas_call(
        functools.partial(_dsa_sample_score_kernel, n_seq=n_seq, n_pages=n_pages, n_cols=n_cols),
        grid_spec=grid_spec,
        out_shape=jax.ShapeDtypeStruct((db, 1, n_cols), F32),
        compiler_params=_cparams(("arbitrary",)),
        name="dsa_sample_scores",
    )(page_table, q_is, w_ib, q_ix, k_ib, *([cache_kidx2] * (n_seq * n_pages)))


def _dsa_sample_select_kernel(s_ref, o_ref, *, n_keys, n_sel, n_cols_log2):
    cols = lax.broadcasted_iota(I32, (1, s_ref.shape[1]), 1)
    score = jnp.where(cols < n_keys, s_ref[...], -jnp.inf)
    o_ref[...] = _topk_select(score, cols, n_sel, n_cols_log2)


def _dsa_sample_select(scores, n_keys):
    db, n_cols = scores.shape
    return pl.pallas_call(
        functools.partial(_dsa_sample_select_kernel, n_keys=n_keys, n_sel=min(TOPK_MAX, n_keys // 4),
                          n_cols_log2=math.ceil(math.log2(n_cols))),
        out_shape=jax.ShapeDtypeStruct((db, n_cols), F32),
        compiler_params=pltpu.CompilerParams(vmem_limit_bytes=VMEM_LIMIT_BYTES),
        name="dsa_sample_select",
    )(scores)


def _dsa_sample_attend_kernel(pt_ref, q_ref, m_ref, kn_ref, *refs, n_seq, n_pages):
    del pt_ref
    n_blocks = n_seq * n_pages
    k_refs, v_refs, o_ref = refs[:n_blocks], refs[n_blocks:2 * n_blocks], refs[2 * n_blocks]
    nt = (((1,), (1,)), ((), ()))
    for s in range(n_seq):
        q = q_ref[s]
        mask = m_ref[s]
        logits = []
        for p in range(n_pages):
            l_p = lax.dot_general(q, k_refs[s * n_pages + p][0].astype(BF16), nt,
                                  preferred_element_type=F32)
            logits.append(jnp.where(mask[:, p * LANES:(p + 1) * LANES] > 0.5, l_p, NEG_BIG))
        kn = kn_ref[s]
        k_new = kn[:, :LANES].astype(BF16).astype(F32)
        v_new = kn[:, LANES:].astype(BF16).astype(F32)
        l_self = jnp.sum(q.astype(F32) * k_new, axis=1, keepdims=True)
        l_self = jnp.where(mask[:, n_pages * LANES:n_pages * LANES + 1] > 0.5, l_self, NEG_BIG)
        mx = l_self
        for l_p in logits:
            mx = jnp.maximum(mx, jnp.max(l_p, axis=1, keepdims=True))
        p_self = jnp.exp(l_self - mx)
        den = p_self
        acc = p_self * v_new
        for p in range(n_pages):
            w_p = jnp.exp(logits[p] - mx)
            den = den + jnp.sum(w_p, axis=1, keepdims=True)
            acc = acc + jnp.dot(w_p.astype(BF16), v_refs[s * n_pages + p][0].astype(BF16),
                                preferred_element_type=F32)
        o_ref[s] = acc / den


def _dsa_sample_attend(page_table, q8, mask, kv_new, cache_k2, cache_v2, layer):
    db, n_pages = page_table.shape
    n_seq = DS_SEQ if db % DS_SEQ == 0 else 1
    n_cols = mask.shape[-1]
    per_seq = lambda shape: pl.BlockSpec((n_seq,) + shape, lambda i, pt: (i, 0, 0))
    grid_spec = pltpu.PrefetchScalarGridSpec(
        num_scalar_prefetch=1, grid=(db // n_seq,),
        in_specs=[per_seq((B_HEADS, LANES)), per_seq((1, n_cols)), per_seq((1, 2 * LANES))]
        + _page_specs(n_seq, n_pages, LANES, layer) + _page_specs(n_seq, n_pages, LANES, layer),
        out_specs=per_seq((B_HEADS, LANES)))
    return pl.pallas_call(
        functools.partial(_dsa_sample_attend_kernel, n_seq=n_seq, n_pages=n_pages),
        grid_spec=grid_spec,
        out_shape=jax.ShapeDtypeStruct((db, B_HEADS, LANES), F32),
        compiler_params=_cparams(("arbitrary",)),
        name="dsa_sample_attend",
    )(page_table, q8, mask, kv_new, *([cache_k2] * (n_seq * n_pages)), *([cache_v2] * (n_seq * n_pages)))


def _even_out_kernel(oa_ref, ga_ref, ob_ref, x_ref, w_ref, ng_ref, lg_ref, lb_ref, o_ref, *, tm, seq_pad):
    ng = ng_ref[...]
    acc = jnp.dot(ob_ref[...].astype(BF16), w_ref[A_WIDTH:, :], preferred_element_type=F32)
    for h in range(A_HEADS):
        sl = slice(h * A_DV, (h + 1) * A_DV)
        oa = oa_ref[:, sl]
        oa = oa * lax.rsqrt(jnp.mean(oa * oa, axis=-1, keepdims=True) + RMS_EPS) * ng
        ga = ga_ref[:, sl]
        oa = oa * (ga * _sigmoid(ga))
        acc = acc + jnp.dot(oa.astype(BF16), w_ref[sl, :], preferred_element_type=F32)
    y = DN_ALPHA * x_ref[...] + acc
    o_ref[...] = _finish_rows(y, lg_ref[...], lb_ref[...], pl.program_id(0), tm, seq_pad)


def _even_out(oa, h, ob, x, w, ng, lg, lb, tm, seq_pad):
    m = x.shape[0]
    row = lambda w_, cb: pl.BlockSpec((tm, w_), lambda i: (i, cb))
    full = lambda a: pl.BlockSpec(a.shape, lambda i: (0,) * a.ndim)
    return pl.pallas_call(
        functools.partial(_even_out_kernel, tm=tm, seq_pad=seq_pad),
        grid=(m // tm,),
        in_specs=[row(A_WIDTH, 0), row(A_WIDTH, EV_GA // A_WIDTH), row(B_WIDTH, 0), row(D_MODEL, 0),
                  full(w), full(ng), full(lg), full(lb)],
        out_specs=row(D_MODEL, 0),
        out_shape=jax.ShapeDtypeStruct((m, D_MODEL), F32),
        compiler_params=_cparams(("parallel",)),
        name="even_out",
    )(oa, h, ob, x, w, ng, lg, lb)


RT_EXPERT0 = N_GROUPS


def _lane_argmax(v, lane):
    mx = jnp.max(v, axis=-1, keepdims=True)
    idx = jnp.min(jnp.where(v == mx, lane, float(LANES)), axis=-1, keepdims=True)
    return mx, idx


def _router_gates(x, wrh_ref, wrl_ref, rb_ref):
    xh = x.astype(BF16)
    xl = (x - xh.astype(F32)).astype(BF16)
    logits = (jnp.dot(xh, wrh_ref[...], preferred_element_type=F32)
              + jnp.dot(xl, wrh_ref[...], preferred_element_type=F32)
              + jnp.dot(xh, wrl_ref[...], preferred_element_type=F32)) + rb_ref[...]
    lane_i = lax.broadcasted_iota(I32, logits.shape, 1)
    lane = lane_i.astype(F32)
    neg_inf = -jnp.inf
    g_logits = jnp.where(lane_i < N_GROUPS, logits, neg_inf)
    g_max, g_idx = _lane_argmax(g_logits, lane)
    g_val = 1.0 / jnp.sum(jnp.exp(g_logits - g_max), axis=-1, keepdims=True)
    e_lane = lane_i - RT_EXPERT0
    lane_group = jnp.where(e_lane >= 0, e_lane >> 2, -1)
    lane_group = jnp.where(lane_i < RT_EXPERT0 + N_EXPERTS, lane_group, -1).astype(F32)
    e_logits = jnp.where(lane_group == g_idx, logits, neg_inf)
    e_max, first = _lane_argmax(e_logits, lane)
    p = jnp.exp(e_logits - e_max)
    p = p / jnp.sum(p, axis=-1, keepdims=True)
    p1 = jnp.sum(jnp.where(lane == first, p, 0.0), axis=-1, keepdims=True)
    rest = jnp.where(lane == first, neg_inf, jnp.where(lane_group == g_idx, p, neg_inf))
    p2, second = _lane_argmax(rest, lane)
    scale = g_val / (p1 + p2)
    return jnp.where(lane == first, p1 * scale, jnp.where(lane == second, p2 * scale, 0.0))


MOE_EXPERTS_PER_STEP = EXPERTS_PER_GROUP


def _moe_kernel(x_ref, wrh_ref, wrl_ref, rb_ref, wgu_ref, wd_ref, lg_ref, lb_ref, o_ref, acc_ref, gate_ref, xb_ref,
                *, tm, seq_pad):
    step = pl.program_id(1)

    @pl.when(step == 0)
    def _():
        gate_ref[...] = _router_gates(x_ref[...], wrh_ref, wrl_ref, rb_ref)
        acc_ref[...] = jnp.zeros_like(acc_ref)
        xb_ref[...] = x_ref[...].astype(BF16)

    lane = lax.broadcasted_iota(I32, (1, LANES), 1)
    for j in range(MOE_EXPERTS_PER_STEP):
        e = step * MOE_EXPERTS_PER_STEP + j
        gate = jnp.sum(jnp.where(lane == e + RT_EXPERT0, gate_ref[...], 0.0), axis=-1, keepdims=True)
        hgu = jnp.dot(xb_ref[...], wgu_ref[j], preferred_element_type=F32)
        hg = hgu[:, :EXPERT_FF]
        act = (hg * _sigmoid(hg)) * hgu[:, EXPERT_FF:] * gate
        acc_ref[...] += jnp.dot(act.astype(BF16), wd_ref[j], preferred_element_type=F32)

    @pl.when(step == pl.num_programs(1) - 1)
    def _():
        y = DN_ALPHA * x_ref[...] + acc_ref[...]
        o_ref[...] = _finish_rows(y, lg_ref[...], lb_ref[...], pl.program_id(0), tm, seq_pad)


def _moe(x, p, lg, lb, tm, seq_pad):
    m = x.shape[0]
    eps = MOE_EXPERTS_PER_STEP
    full = lambda a: pl.BlockSpec(a.shape, lambda i, e: (0,) * a.ndim)
    return pl.pallas_call(
        functools.partial(_moe_kernel, tm=tm, seq_pad=seq_pad),
        grid=(m // tm, N_EXPERTS // eps),
        in_specs=[pl.BlockSpec((tm, D_MODEL), lambda i, e: (i, 0)),
                  full(p["wrh"]), full(p["wrl"]), full(p["rb"]),
                  pl.BlockSpec((eps, D_MODEL, 2 * EXPERT_FF), lambda i, e: (e, 0, 0)),
                  pl.BlockSpec((eps, EXPERT_FF, D_MODEL), lambda i, e: (e, 0, 0)),
                  full(lg), full(lb)],
        out_specs=pl.BlockSpec((tm, D_MODEL), lambda i, e: (i, 0)),
        out_shape=jax.ShapeDtypeStruct((m, D_MODEL), F32),
        scratch_shapes=[pltpu.VMEM((tm, D_MODEL), F32), pltpu.VMEM((tm, LANES), F32),
                        pltpu.VMEM((tm, D_MODEL), BF16)],
        compiler_params=_cparams(("parallel", "arbitrary")),
        name="moe",
    )(x, p["wrh"], p["wrl"], p["rb"], p["wgu"], p["wd"], lg, lb)


def _pack_moe_params(rg_w, rg_b, re_w, re_b, w_gate, w_up, w_down):
    d = rg_w.shape[0]
    wr = jnp.concatenate([rg_w, re_w, jnp.zeros((d, LANES - N_GROUPS - N_EXPERTS), F32)], axis=1).astype(F32)
    wrh = wr.astype(BF16)
    wrl = (wr - wrh.astype(F32)).astype(BF16)
    rb = jnp.concatenate([rg_b, re_b, jnp.zeros((LANES - N_GROUPS - N_EXPERTS,), F32)]).reshape(1, LANES)
    return dict(wrh=wrh, wrl=wrl, rb=rb.astype(F32),
                wgu=jnp.concatenate([w_gate, w_up], axis=2).astype(BF16), wd=w_down.astype(BF16))


C_HIST = 32
D_HIST = 8


def _softplus(x):
    return jnp.maximum(x, 0.0) + jnp.log1p(jnp.exp(-jnp.abs(x)))


def _gelu_tanh(x):
    return 0.5 * x * (1.0 + jnp.tanh(math.sqrt(2.0 / math.pi) * (x + 0.044715 * (x * x * x))))


def _lru_gates(xc, wab_ref, ba, bx, lam):
    proj = jnp.dot(xc.astype(BF16), wab_ref[...], preferred_element_type=F32)
    r = _sigmoid(proj[:, :D_WIDTH] + ba)
    ig = _sigmoid(proj[:, D_WIDTH:] + bx)
    log_a = -LRU_C * r * _softplus(-lam)
    a = jnp.exp(log_a)
    th = jnp.tanh(log_a)
    drive = jnp.sqrt(jnp.maximum(-2.0 * th / (1.0 - th), 0.0)) * ig * xc
    return a, drive


def _odd_seq_kernel(ca_ref, cg_ref, dx_ref, dg_ref, cw_ref, cb_ref, lng_ref, lnb_ref, dw_ref, db_ref,
                    wab_ref, ba_ref, bx_ref, lam_ref,
                    y_ref, cst_ref, dst_ref, hst_ref, uext, dext, hc):
    c = pl.program_id(1)

    @pl.when(c == 0)
    def _():
        uext[:C_HIST, :] = jnp.zeros((C_HIST, C_WIDTH), F32)
        dext[:D_HIST, :] = jnp.zeros((D_HIST, D_WIDTH), F32)
        hc[...] = jnp.zeros_like(hc)

    uext[C_HIST:, :] = ca_ref[...] * _sigmoid(cg_ref[...])
    acc = jnp.zeros((ROW_TILE, C_WIDTH), F32)
    for j in range(C_CONV):
        off = C_HIST - (C_CONV - 1) + j
        acc = acc + cw_ref[j:j + 1, :] * uext[off:off + ROW_TILE, :]
    yc = _layer_norm(acc + cb_ref[...], lng_ref[...], lnb_ref[...])
    y_ref[:, :C_WIDTH] = yc * _sigmoid(yc)

    dext[D_HIST:, :] = dx_ref[...]
    xc = jnp.zeros((ROW_TILE, D_WIDTH), F32)
    for j in range(D_CONV):
        off = D_HIST - (D_CONV - 1) + j
        xc = xc + dw_ref[j:j + 1, :] * dext[off:off + ROW_TILE, :]
    xc = xc + db_ref[...]
    a, u = _lru_gates(xc, wab_ref, ba_ref[...], bx_ref[...], lam_ref[...])
    rows = lax.broadcasted_iota(I32, (ROW_TILE, 1), 0)
    u = jnp.where(jnp.logical_and(c == 0, rows < PAD_ROWS), 0.0, u)
    d = 1
    while d < ROW_TILE:
        head = rows < d
        a_prev = jnp.where(head, 1.0, pltpu.roll(a, d, axis=0))
        u_prev = jnp.where(head, 0.0, pltpu.roll(u, d, axis=0))
        u = u + a * u_prev
        a = a * a_prev
        d *= 2
    hs = a * hc[...] + u
    y_ref[:, C_WIDTH:] = hs * _gelu_tanh(dg_ref[...])

    hc[...] = hs[ROW_TILE - 1:ROW_TILE, :]
    uext[:C_HIST, :] = uext[ROW_TILE:ROW_TILE + C_HIST, :]
    dext[:D_HIST, :] = dext[ROW_TILE:ROW_TILE + D_HIST, :]

    @pl.when(c == pl.num_programs(1) - 1)
    def _():
        cst_ref[0] = uext[C_HIST + ROW_TILE - (C_CONV - 1):C_HIST + ROW_TILE, :]
        dst_ref[0] = dext[D_HIST + ROW_TILE - (D_CONV - 1):D_HIST + ROW_TILE, :]
        hst_ref[0] = hs[ROW_TILE - 1:ROW_TILE, :]


def _odd_seq(h, p, n_batch, n_chunks):
    m = h.shape[0]
    blk = lambda cb: pl.BlockSpec((ROW_TILE, 512), lambda b, c: (b * n_chunks + c, cb))
    full = lambda a: pl.BlockSpec(a.shape, lambda b, c: (0,) * a.ndim)
    params = [p["cw"], p["cb"], p["lng"], p["lnb"], p["dw"], p["db"], p["wab"], p["ba"], p["bx"], p["lam"]]
    state = lambda r: pl.BlockSpec((1, r, 512), lambda b, c: (b, 0, 0))
    return pl.pallas_call(
        _odd_seq_kernel,
        grid=(n_batch, n_chunks),
        in_specs=[blk(0), blk(1), blk(2), blk(3)] + [full(a) for a in params],
        out_specs=[pl.BlockSpec((ROW_TILE, 1024), lambda b, c: (b * n_chunks + c, 0)),
                   state(C_CONV - 1), state(D_CONV - 1), state(1)],
        out_shape=[jax.ShapeDtypeStruct((m, C_WIDTH + D_WIDTH), F32),
                   jax.ShapeDtypeStruct((n_batch, C_CONV - 1, C_WIDTH), F32),
                   jax.ShapeDtypeStruct((n_batch, D_CONV - 1, D_WIDTH), F32),
                   jax.ShapeDtypeStruct((n_batch, 1, D_WIDTH), F32)],
        scratch_shapes=[pltpu.VMEM((C_HIST + ROW_TILE, C_WIDTH), F32),
                        pltpu.VMEM((D_HIST + ROW_TILE, D_WIDTH), F32),
                        pltpu.VMEM((1, D_WIDTH), F32)],
        compiler_params=_cparams(("parallel", "arbitrary")),
        name="odd_seq",
    )(h, h, h, h, *params)


def _pack_odd_params(cw, cb, lng, lnb, dw, db, wa, ba, wx, bx, lam):
    def block_diag(w):
        out = jnp.zeros((D_WIDTH, D_WIDTH), w.dtype)
        for n in range(D_BLOCKS):
            out = out.at[n * D_BLOCK_W:(n + 1) * D_BLOCK_W, n * D_BLOCK_W:(n + 1) * D_BLOCK_W].set(w[n])
        return out
    row = lambda v: v.reshape(1, -1).astype(F32)
    return dict(cw=cw.astype(F32), cb=row(cb), lng=row(lng), lnb=row(lnb), dw=dw.astype(F32), db=row(db),
                wab=jnp.concatenate([block_diag(wa), block_diag(wx)], axis=1).astype(BF16),
                ba=row(ba), bx=row(bx), lam=row(lam))


def _odd_sample_kernel(ca_ref, cg_ref, dx_ref, dg_ref, cs_ref, ds_ref, h0_ref,
                       cw_ref, cb_ref, lng_ref, lnb_ref, dw_ref, db_ref, wab_ref, ba_ref, bx_ref, lam_ref,
                       y_ref, cso_ref, dso_ref, ho_ref):
    u = ca_ref[...] * _sigmoid(cg_ref[...])
    acc = cw_ref[C_CONV - 1:C_CONV, :] * u
    for j in range(C_CONV - 1):
        acc = acc + cw_ref[j:j + 1, :] * cs_ref[j]
        if j > 0:
            cso_ref[j - 1] = cs_ref[j]
    cso_ref[C_CONV - 2] = u
    yc = _layer_norm(acc + cb_ref[...], lng_ref[...], lnb_ref[...])
    y_ref[:, :C_WIDTH] = yc * _sigmoid(yc)

    dx = dx_ref[...]
    xc = dw_ref[D_CONV - 1:D_CONV, :] * dx
    for j in range(D_CONV - 1):
        xc = xc + dw_ref[j:j + 1, :] * ds_ref[j]
        if j > 0:
            dso_ref[j - 1] = ds_ref[j]
    dso_ref[D_CONV - 2] = dx
    xc = xc + db_ref[...]
    a, drive = _lru_gates(xc, wab_ref, ba_ref[...], bx_ref[...], lam_ref[...])
    h = a * h0_ref[...] + drive
    ho_ref[...] = h
    y_ref[:, C_WIDTH:] = h * _gelu_tanh(dg_ref[...])


def _odd_sample(h, cs_t, ds_t, h0, p):
    db = h.shape[0]
    params = [p["cw"], p["cb"], p["lng"], p["lnb"], p["dw"], p["db"], p["wab"], p["ba"], p["bx"], p["lam"]]
    full = lambda a: pl.BlockSpec(a.shape, lambda i: (0,) * a.ndim)
    blk = lambda cb: pl.BlockSpec((db, 512), lambda i: (0, cb))
    return pl.pallas_call(
        _odd_sample_kernel,
        grid=(1,),
        in_specs=[blk(0), blk(1), blk(2), blk(3), full(cs_t), full(ds_t), full(h0)] + [full(a) for a in params],
        out_specs=[pl.BlockSpec((db, 1024), lambda i: (0, 0)), full(cs_t), full(ds_t), full(h0)],
        out_shape=[jax.ShapeDtypeStruct((db, C_WIDTH + D_WIDTH), F32),
                   jax.ShapeDtypeStruct(cs_t.shape, F32), jax.ShapeDtypeStruct(ds_t.shape, F32),
                   jax.ShapeDtypeStruct(h0.shape, F32)],
        compiler_params=_cparams(("arbitrary",)),
        name="odd_sample",
    )(h, h, h, h, cs_t, ds_t, h0, *params)


def _mm_postnorm_kernel(a_ref, x_ref, w_ref, lg_ref, lb_ref, o_ref, *, tm, seq_pad):
    acc = jnp.dot(a_ref[...].astype(BF16), w_ref[...], preferred_element_type=F32)
    y = DN_ALPHA * x_ref[...] + acc
    o_ref[...] = _finish_rows(y, lg_ref[...], lb_ref[...], pl.program_id(0), tm, seq_pad)


def _mm_postnorm(a, x, w, lg, lb, tm, seq_pad):
    m = x.shape[0]
    row = lambda w_: pl.BlockSpec((tm, w_), lambda i: (i, 0))
    full = lambda arr: pl.BlockSpec(arr.shape, lambda i: (0,) * arr.ndim)
    return pl.pallas_call(
        functools.partial(_mm_postnorm_kernel, tm=tm, seq_pad=seq_pad),
        grid=(m // tm,),
        in_specs=[row(a.shape[1]), row(D_MODEL), full(w), full(lg), full(lb)],
        out_specs=row(D_MODEL),
        out_shape=jax.ShapeDtypeStruct((m, D_MODEL), F32),
        compiler_params=_cparams(("parallel",)),
        name="mm_postnorm",
    )(a, x, w, lg, lb)


def _row_tile(m):
    for tm in (512, 256, ROW_TILE):
        if m % tm == 0:
            return tm
    return m


def _moe_tile(m):
    for tm in (1024, 512, 256, ROW_TILE):
        if m % tm == 0:
            return tm
    return m


def _even_sample_attention(hs, qbx, qix, kvf, kib, cache_k2, cache_v2, cache_kidx2, page_table, layer):
    db, n_pages = page_table.shape
    past = n_pages * PAGE_SIZE
    n_keys = past + 1
    n_cols = -(-(n_keys) // LANES) * LANES
    qi = qix.reshape(db, IDX_HEADS, 2, IDX_DIM)
    qi = jnp.stack([qi[:, h, h % 2] for h in range(IDX_HEADS)], axis=1)
    zeros = jnp.zeros_like(qi)
    q_is = jnp.concatenate([qi, zeros] if layer == 0 else [zeros, qi], axis=-1)
    w_ib = jnp.broadcast_to(hs[:, EV_WI:EV_WI + IDX_HEADS, None], (db, IDX_HEADS, LANES))
    scores = _dsa_sample_scores(page_table, q_is, w_ib, qix.reshape(db, IDX_HEADS, LANES),
                                kib.reshape(db, 1, LANES), cache_kidx2, n_cols)
    mask = _dsa_sample_select(scores.reshape(db, n_cols), n_keys)
    o8 = _dsa_sample_attend(page_table, qbx.reshape(db, B_HEADS, LANES), mask.reshape(db, 1, n_cols),
                            kvf.reshape(db, 1, 2 * LANES), cache_k2, cache_v2, layer)
    per = B_HEADS // B_KV_HEADS
    halves = [o8[:, h, (h // per) * B_HEAD_DIM:(h // per + 1) * B_HEAD_DIM] for h in range(B_HEADS)]
    return jnp.concatenate(halves, axis=-1)


def kernel(x_prompt, x_sample, cache_k, cache_v, cache_kidx, state_hgrn, state_conv_c, state_conv_d, state_lru,
           page_table, meta_tokens, w_in_even, w_out_even, hgrn_lb_logits, hgrn_norm_g, w_in_odd, w_out_odd,
           conv_c_w, conv_c_b, conv_c_ln_g, conv_c_ln_b, conv_d_w, conv_d_b, lru_wa, lru_ba, lru_wx, lru_bx,
           lru_lambda, ln1_g, ln1_b, ln2_g, ln2_b, router_g_w, router_g_b, router_e_w, router_e_b,
           w_gate, w_up, w_down):
    bsz, seq, _ = x_prompt.shape
    dbsz, dseq, _ = x_sample.shape
    assert dseq == 1 and seq % ROW_TILE == 0 and dbsz % HS_TB == 0
    t_real = N_META + seq
    n_chunks = (PAD_ROWS + t_real) // ROW_TILE
    t_pad = n_chunks * ROW_TILE
    n_phys = cache_k.shape[0]
    past_len = page_table.shape[1] * PAGE_SIZE
    n_even = cache_k.shape[2]

    meta = jnp.broadcast_to(meta_tokens.astype(F32)[None], (bsz, N_META, D_MODEL))
    xp = jnp.concatenate([jnp.zeros((bsz, PAD_ROWS, D_MODEL), F32), meta, x_prompt.astype(F32)], axis=1)
    xp = xp.reshape(bsz * t_pad, D_MODEL)
    xs = x_sample.reshape(dbsz, D_MODEL).astype(F32)
    tm_p, tm_s = _row_tile(bsz * t_pad), _row_tile(dbsz)

    cos_p, sin_p = _rope_tables(np.maximum(np.arange(t_pad) - PAD_ROWS, 0))
    cos_p, sin_p = jnp.tile(jnp.asarray(cos_p), (bsz, 1)), jnp.tile(jnp.asarray(sin_p), (bsz, 1))
    cos_s, sin_s = _rope_tables(np.full((dbsz,), past_len))
    cos_s, sin_s = jnp.asarray(cos_s), jnp.asarray(sin_s)

    sm = jax.nn.softmax(hgrn_lb_logits.astype(F32), axis=0)
    lower_bounds = jnp.cumsum(sm, axis=0) - sm[0]

    cache_k2 = cache_k.reshape(n_phys, PAGE_SIZE, n_even * B_KV_WIDTH)
    cache_v2 = cache_v.reshape(n_phys, PAGE_SIZE, n_even * B_KV_WIDTH)
    cache_kidx2 = cache_kidx.reshape(n_phys, PAGE_SIZE, n_even * IDX_DIM)
    assert n_even * IDX_DIM == LANES and B_KV_WIDTH == LANES

    row2 = lambda v: v.reshape(1, -1).astype(F32)
    unpad = lambda a, w: a.reshape(bsz, t_pad, w)[:, PAD_ROWS:]
    kp, vp, ip, hp, cp, dp, lp = [], [], [], [], [], [], []
    ks, vs, iks, hsm, csm, dsm, lsm = [], [], [], [], [], [], []
    for layer in range(DEPTH):
        li = layer // 2
        lg1, lb1 = row2(ln1_g[layer]), row2(ln1_b[layer])
        if layer % 2 == 0:
            w_in = _pack_even_weight(w_in_even[li])
            w_out = w_out_even[li].astype(BF16)
            lb = lower_bounds[li].reshape(1, A_WIDTH)
            ng = row2(hgrn_norm_g[li])
            h = _matmul(xp, w_in, tm_p, 512)
            qbx, qix, kvf, kvb, kif, kib = _even_post(h, cos_p, sin_p, tm_p)
            oa, s_p = _hgrn_prompt(h, lb, bsz, n_chunks)
            ob = _dsa_prompt(qbx, qix, h, kvb, kib, bsz, n_chunks)
            xp = _even_out(oa, h, ob, xp, w_out, ng, lg1, lb1, tm_p, t_pad)
            kp.append(unpad(kvf[:, :LANES], LANES).reshape(bsz, t_real, B_KV_HEADS, B_HEAD_DIM))
            vp.append(unpad(kvf[:, LANES:], LANES).reshape(bsz, t_real, B_KV_HEADS, B_HEAD_DIM))
            ip.append(unpad(kif[:, :IDX_DIM], IDX_DIM))
            hp.append(s_p)
            h = _matmul(xs, w_in, tm_s, 512)
            qbx, qix, kvf, kvb, kif, kib = _even_post(h, cos_s, sin_s, tm_s)
            oa, s_s = _hgrn_sample(h, lb, state_hgrn, li)
            ob = _even_sample_attention(h, qbx, qix, kvf, kib, cache_k2, cache_v2, cache_kidx2, page_table, li)
            xs = _even_out(oa, h, ob, xs, w_out, ng, lg1, lb1, tm_s, None)
            ks.append(kvf[:, :LANES].reshape(dbsz, 1, B_KV_HEADS, B_HEAD_DIM))
            vs.append(kvf[:, LANES:].reshape(dbsz, 1, B_KV_HEADS, B_HEAD_DIM))
            iks.append(kif[:, :IDX_DIM].reshape(dbsz, 1, IDX_DIM))
            hsm.append(s_s)
        else:
            w_in = w_in_odd[li].astype(BF16)
            w_out = w_out_odd[li].astype(BF16)
            p = _pack_odd_params(conv_c_w[li], conv_c_b[li], conv_c_ln_g[li], conv_c_ln_b[li], conv_d_w[li],
                                 conv_d_b[li], lru_wa[li], lru_ba[li], lru_wx[li], lru_bx[li], lru_lambda[li])
            h = _matmul(xp, w_in, tm_p, 512)
            y, c_p, d_p, h_p = _odd_seq(h, p, bsz, n_chunks)
            xp = _mm_postnorm(y, xp, w_out, lg1, lb1, tm_p, t_pad)
            cp.append(c_p); dp.append(d_p); lp.append(h_p[:, 0])
            h = _matmul(xs, w_in, tm_s, 512)
            y, c_s, d_s, h_s = _odd_sample(h, jnp.swapaxes(state_conv_c[:, li], 0, 1).astype(F32),
                                           jnp.swapaxes(state_conv_d[:, li], 0, 1).astype(F32),
                                           state_lru[:, li].astype(F32), p)
            xs = _mm_postnorm(y, xs, w_out, lg1, lb1, tm_s, None)
            csm.append(jnp.swapaxes(c_s, 0, 1)); dsm.append(jnp.swapaxes(d_s, 0, 1)); lsm.append(h_s)
        mp = _pack_moe_params(router_g_w[layer], router_g_b[layer], router_e_w[layer], router_e_b[layer],
                              w_gate[layer], w_up[layer], w_down[layer])
        lg2, lb2 = row2(ln2_g[layer]), row2(ln2_b[layer])
        xp = _moe(xp, mp, lg2, lb2, _moe_tile(bsz * t_pad), t_pad)
        xs = _moe(xs, mp, lg2, lb2, _moe_tile(dbsz), None)

    y_prompt = xp.reshape(bsz, t_pad, D_MODEL)[:, PAD_ROWS + N_META:]
    y_sample = xs.reshape(dbsz, 1, D_MODEL)
    return (y_prompt, y_sample, jnp.stack(kp, axis=2), jnp.stack(vp, axis=2), jnp.stack(ip, axis=2),
            jnp.stack(hp, axis=1), jnp.stack(cp, axis=1), jnp.stack(dp, axis=1), jnp.stack(lp, axis=1),
            jnp.stack(ks, axis=2), jnp.stack(vs, axis=2), jnp.stack(iks, axis=2), jnp.stack(hsm, axis=1),
            jnp.stack(csm, axis=1), jnp.stack(dsm, axis=1), jnp.stack(lsm, axis=1))
```

```python
import functools
import math

import numpy as np
import jax
import jax.numpy as jnp
from jax import lax
from jax.experimental import pallas as pl
from jax.experimental.pallas import tpu as pltpu

F32 = jnp.float32
BF16 = jnp.bfloat16
I32 = jnp.int32

D_MODEL = 1024
DEPTH = 4
PAGE_SIZE = 128
N_META = 16
A_HEADS = 4
A_DK = 128
A_DV = 128
A_WIDTH = A_HEADS * A_DV
B_HEADS = 8
B_KV_HEADS = 2
B_HEAD_DIM = 64
B_WIDTH = B_HEADS * B_HEAD_DIM
B_KV_WIDTH = B_KV_HEADS * B_HEAD_DIM
IDX_HEADS = 8
IDX_DIM = 64
TOPK_MAX = 256
ROPE_THETA = 10000.0
C_WIDTH = 512
C_CONV = 31
D_WIDTH = 512
D_CONV = 4
D_BLOCKS = 8
D_BLOCK_W = D_WIDTH // D_BLOCKS
LRU_C = 8.0
N_GROUPS = 4
EXPERTS_PER_GROUP = 4
N_EXPERTS = N_GROUPS * EXPERTS_PER_GROUP
TOP_E = 2
EXPERT_FF = 256
DN_ALPHA = (2 * DEPTH) ** 0.25
LN_EPS = 1e-5
RMS_EPS = 1e-6
NEG_BIG = -1e30
LB_TINY = 1e-30

LANES = 128
SUBLANES = 8
ROW_TILE = 128
VMEM_LIMIT_BYTES = 56 * 1024 * 1024

EV_QA, EV_FA, EV_IA, EV_GA = 0, 512, 1024, 1536
EV_QB, EV_QI, EV_KB, EV_VB, EV_KI, EV_WI = 2048, 2560, 3072, 3200, 3328, 3456
EV_COLS = 3584
PAD_ROWS = ROW_TILE - N_META
INT_MIN = -2147483648


def _cparams(sem):
    return pltpu.CompilerParams(dimension_semantics=sem, vmem_limit_bytes=VMEM_LIMIT_BYTES)


def _sigmoid(x):
    return 1.0 / (1.0 + jnp.exp(-x))


def _log_sigmoid(x):
    return jnp.minimum(x, 0.0) - jnp.log1p(jnp.exp(-jnp.abs(x)))


def _layer_norm(y, g, b):
    mu = jnp.mean(y, axis=-1, keepdims=True)
    d = y - mu
    var = jnp.mean(d * d, axis=-1, keepdims=True)
    return d * lax.rsqrt(var + LN_EPS) * g + b


def _keep_rows(tile, tm, seq_pad):
    r = (tile * tm + lax.broadcasted_iota(I32, (tm, 1), 0)).astype(F32)
    pos = r - jnp.floor((r + 0.5) * (1.0 / seq_pad)) * seq_pad
    return jnp.where(pos >= PAD_ROWS, 1.0, 0.0)


def _finish_rows(y, g, b, tile, tm, seq_pad):
    out = _layer_norm(y, g, b)
    return out if seq_pad is None else out * _keep_rows(tile, tm, seq_pad)


def _mm_kernel(x_ref, w_ref, o_ref, *, tn):
    x = x_ref[...].astype(BF16)
    for c in range(0, w_ref.shape[1], tn):
        o_ref[:, c:c + tn] = jnp.dot(x, w_ref[:, c:c + tn], preferred_element_type=F32)


def _matmul(x, w, tm, tn):
    m, k = x.shape
    n = w.shape[1]
    assert m % tm == 0 and n % tn == 0
    return pl.pallas_call(
        functools.partial(_mm_kernel, tn=tn),
        grid=(m // tm,),
        in_specs=[pl.BlockSpec((tm, k), lambda i: (i, 0)),
                  pl.BlockSpec((k, n), lambda i: (0, 0))],
        out_specs=pl.BlockSpec((tm, n), lambda i: (i, 0)),
        out_shape=jax.ShapeDtypeStruct((m, n), F32),
        compiler_params=_cparams(("parallel",)),
        name="matmul",
    )(x, w)


HG_LEVELS = 7


def _hgrn_mid_rows(b_ref, level):
    half = 1 << level
    blk = half * 2
    if blk >= SUBLANES:
        pieces = []
        for start in range(0, ROW_TILE, blk):
            m = start + half - 1
            pieces.append(jnp.broadcast_to(b_ref[m:m + 1, :], (blk, LANES)))
        return pieces[0] if len(pieces) == 1 else jnp.concatenate(pieces, axis=0)
    sub = lax.broadcasted_iota(I32, (SUBLANES, LANES), 0)
    pieces = []
    for start in range(0, ROW_TILE, SUBLANES):
        acc = None
        for off in range(SUBLANES - blk, -1, -blk):
            m = start + off + half - 1
            row = jnp.broadcast_to(b_ref[m:m + 1, :], (SUBLANES, LANES))
            acc = row if acc is None else jnp.where(sub < off + blk, row, acc)
        pieces.append(acc)
    return jnp.concatenate(pieces, axis=0)


def _hgrn_kernel(qa_ref, fa_ref, ia_ref, lb_ref, o_ref, s_ref, st_ref, b_ref):
    c = pl.program_id(1)

    @pl.when(c == 0)
    def _():
        st_ref[...] = jnp.zeros_like(st_ref)

    for h in range(A_HEADS):
        sl = slice(h * A_DK, (h + 1) * A_DK)
        st_new = _hgrn_head_chunk(qa_ref[:, sl], fa_ref[:, sl], ia_ref[:, sl], lb_ref[:, sl],
                                  o_ref.at[:, sl], st_ref.at[h], b_ref.at[h])

        @pl.when(c == pl.num_programs(1) - 1)
        def _():
            s_ref[0, h] = st_new.T


def _hgrn_head_chunk(qa, z, v, lb, o_ref, st_ref, b_ref):
    la = jnp.log(jnp.maximum(lb, LB_TINY))
    lc = jnp.log1p(-lb) + _log_sigmoid(z)
    log_f = jnp.maximum(la, lc) + jnp.log1p(jnp.exp(-jnp.abs(la - lc)))
    k = (1.0 - lb) * _sigmoid(-z)
    q = qa * _sigmoid(qa)

    rows = lax.broadcasted_iota(I32, (ROW_TILE, 1), 0)
    cols = lax.broadcasted_iota(I32, (1, ROW_TILE), 1)

    b = log_f
    for lv in range(HG_LEVELS):
        d = 1 << lv
        b = b + jnp.where(rows >= d, pltpu.roll(b, d, axis=0), 0.0)
    b_ref[...] = b

    scores = jnp.zeros((ROW_TILE, ROW_TILE), F32)
    nt = (((1,), (1,)), ((), ()))
    for lv in range(HG_LEVELS):
        e = jnp.exp(-jnp.abs(b - _hgrn_mid_rows(b_ref, lv)))
        upper = (rows & (1 << lv)) != 0
        qd = jnp.where(upper, q * e, 0.0).astype(BF16)
        kd = jnp.where(upper, 0.0, k * e).astype(BF16)
        s_l = lax.dot_general(qd, kd, nt, preferred_element_type=F32)
        if lv + 1 < HG_LEVELS:
            same = (rows >> (lv + 1)) == (cols >> (lv + 1))
            scores = scores + jnp.where(same, s_l, 0.0)
        else:
            scores = scores + s_l
    diag = jnp.sum(q * k, axis=1, keepdims=True)
    scores = jnp.where(rows == cols, diag, scores)

    st = st_ref[...]
    v_bf = v.astype(BF16)
    o = jnp.dot(scores.astype(BF16), v_bf, preferred_element_type=F32)
    o = o + lax.dot_general((q * jnp.exp(b)).astype(BF16), st.astype(BF16), nt,
                            preferred_element_type=F32)
    o_ref[...] = o

    b_last = b_ref[ROW_TILE - 1:ROW_TILE, :]
    kdl = (k * jnp.exp(b_last - b)).astype(BF16)
    st_new = st * jnp.exp(b_last) + lax.dot_general(v_bf, kdl, (((0,), (0,)), ((), ())),
                                                    preferred_element_type=F32)
    st_ref[...] = st_new
    return st_new


def _hgrn_prompt(h, lb, n_batch, n_chunks):
    m = h.shape[0]
    col = lambda base: (lambda b, c: (b * n_chunks + c, base // A_WIDTH))
    return pl.pallas_call(
        _hgrn_kernel,
        grid=(n_batch, n_chunks),
        in_specs=[pl.BlockSpec((ROW_TILE, A_WIDTH), col(EV_QA)),
                  pl.BlockSpec((ROW_TILE, A_WIDTH), col(EV_FA)),
                  pl.BlockSpec((ROW_TILE, A_WIDTH), col(EV_IA)),
                  pl.BlockSpec((1, A_WIDTH), lambda b, c: (0, 0))],
        out_specs=[pl.BlockSpec((ROW_TILE, A_WIDTH), lambda b, c: (b * n_chunks + c, 0)),
                   pl.BlockSpec((1, A_HEADS, A_DK, A_DV), lambda b, c: (b, 0, 0, 0))],
        out_shape=[jax.ShapeDtypeStruct((m, A_WIDTH), F32),
                   jax.ShapeDtypeStruct((n_batch, A_HEADS, A_DK, A_DV), F32)],
        scratch_shapes=[pltpu.VMEM((A_HEADS, A_DV, A_DK), F32), pltpu.VMEM((A_HEADS, ROW_TILE, LANES), F32)],
        compiler_params=_cparams(("parallel", "arbitrary")),
        name="hgrn_prompt",
    )(h, h, h, lb)


HS_TB = 8


def _hgrn_sample_kernel(qa_ref, fa_ref, ia_ref, lb_ref, s_ref, o_ref, so_ref):
    lb = lb_ref[...]
    z = fa_ref[...]
    la = jnp.log(jnp.maximum(lb, LB_TINY))
    lc = jnp.log1p(-lb) + _log_sigmoid(z)
    f = jnp.exp(jnp.maximum(la, lc) + jnp.log1p(jnp.exp(-jnp.abs(la - lc))))
    k = (1.0 - lb) * _sigmoid(-z)
    qa = qa_ref[...]
    q = qa * _sigmoid(qa)
    v = ia_ref[...]
    pad = jnp.zeros((LANES - 3 * HS_TB, LANES), F32)
    for h in range(A_HEADS):
        sl = slice(h * A_DK, (h + 1) * A_DK)
        cols = jnp.concatenate([q[:, sl], k[:, sl], f[:, sl], pad], axis=0).T
        for b in range(HS_TB):
            qc = cols[:, b:b + 1]
            kc = cols[:, HS_TB + b:HS_TB + b + 1]
            fc = cols[:, 2 * HS_TB + b:2 * HS_TB + b + 1]
            s_new = fc * s_ref[b, 0, h] + kc * v[b:b + 1, sl]
            so_ref[b, h] = s_new
            o_ref[b:b + 1, sl] = jnp.sum(qc * s_new, axis=0, keepdims=True)


def _hgrn_sample(h, lb, state, layer):
    db = h.shape[0]
    row = lambda cb: pl.BlockSpec((HS_TB, A_WIDTH), lambda i: (i, cb))
    return pl.pallas_call(
        _hgrn_sample_kernel,
        grid=(db // HS_TB,),
        in_specs=[row(EV_QA // A_WIDTH), row(EV_FA // A_WIDTH), row(EV_IA // A_WIDTH),
                  pl.BlockSpec((1, A_WIDTH), lambda i: (0, 0)),
                  pl.BlockSpec((HS_TB, 1, A_HEADS, A_DK, A_DV), lambda i: (i, layer, 0, 0, 0))],
        out_specs=[row(0), pl.BlockSpec((HS_TB, A_HEADS, A_DK, A_DV), lambda i: (i, 0, 0, 0))],
        out_shape=[jax.ShapeDtypeStruct((db, A_WIDTH), F32),
                   jax.ShapeDtypeStruct((db, A_HEADS, A_DK, A_DV), F32)],
        compiler_params=_cparams(("parallel",)),
        name="hgrn_sample",
    )(h, h, h, lb, state)


def _pack_even_weight(w):
    parts = (A_HEADS * A_DK, A_HEADS * A_DK, A_WIDTH, A_WIDTH, B_WIDTH, B_KV_WIDTH, B_KV_WIDTH,
             IDX_HEADS * IDX_DIM, IDX_DIM, IDX_HEADS)
    cuts = [int(c) for c in np.cumsum(parts)[:-1]]
    qa, fa, ia, ga, qb, kb, vb, qi, ki, wi = jnp.split(w, cuts, axis=1)
    zeros = lambda n: jnp.zeros((w.shape[0], n), w.dtype)
    out = jnp.concatenate([qa, fa, ia, ga, qb, qi, kb, vb, ki, zeros(LANES - IDX_DIM),
                           wi, zeros(LANES - IDX_HEADS)], axis=1)
    assert out.shape[1] == EV_COLS
    return out.astype(BF16)

def _rope_tables(pos):
    half = B_HEAD_DIM // 2
    lane = np.arange(LANES)
    inv = ROPE_THETA ** (-(lane % half).astype(np.float64) / half)
    ang = np.asarray(pos, np.float64)[:, None] * inv[None, :]
    sign = np.where((lane % B_HEAD_DIM) < half, -1.0, 1.0)
    return np.cos(ang).astype(np.float32), (np.sin(ang) * sign[None, :]).astype(np.float32)


def _rope128(x, cos, sin, first_half):
    rot = jnp.where(first_half, pltpu.roll(x, LANES - 32, axis=1), pltpu.roll(x, 32, axis=1))
    return x * cos + rot * sin


def _even_post_kernel(qb_ref, qi_ref, kv_ref, ki_ref, cos_ref, sin_ref,
                      qbx_ref, qix_ref, kvf_ref, kvb_ref, kif_ref, kib_ref):
    cos = cos_ref[...]
    sin = sin_ref[...]
    lane = lax.broadcasted_iota(I32, (1, LANES), 1)
    first_half = (lane % B_HEAD_DIM) < (B_HEAD_DIM // 2)
    low = lane < B_HEAD_DIM
    scale = B_HEAD_DIM ** -0.5
    for pair in range(B_HEADS // 2):
        sl = slice(pair * LANES, (pair + 1) * LANES)
        qb = _rope128(qb_ref[:, sl], cos, sin, first_half) * scale
        qi = _rope128(qi_ref[:, sl], cos, sin, first_half) * scale
        qb_sw = pltpu.roll(qb, B_HEAD_DIM, axis=1)
        group = (2 * pair) // (B_HEADS // B_KV_HEADS)
        for sub in range(2):
            h = 2 * pair + sub
            src = qb if sub == group else qb_sw
            keep = low if group == 0 else jnp.logical_not(low)
            qbx_ref[:, h * LANES:(h + 1) * LANES] = jnp.where(keep, src, 0.0).astype(BF16)
            keep_i = low if sub == 0 else jnp.logical_not(low)
            qix_ref[:, h * LANES:(h + 1) * LANES] = jnp.where(keep_i, qi, 0.0).astype(BF16)
    k = _rope128(kv_ref[:, :LANES], cos, sin, first_half)
    v = kv_ref[:, LANES:]
    kvf_ref[:, :LANES] = k
    kvf_ref[:, LANES:] = v
    kvb_ref[:, :LANES] = k.astype(BF16)
    kvb_ref[:, LANES:] = v.astype(BF16)
    ki = _rope128(ki_ref[...], cos, sin, first_half)
    kif_ref[...] = ki
    kib_ref[...] = (ki + pltpu.roll(ki, B_HEAD_DIM, axis=1)).astype(BF16)


def _even_post(h, cos, sin, tm):
    m = h.shape[0]
    row = lambda w, cb: pl.BlockSpec((tm, w), lambda i: (i, cb))
    return pl.pallas_call(
        _even_post_kernel,
        grid=(m // tm,),
        in_specs=[row(512, EV_QB // 512), row(512, EV_QI // 512), row(256, EV_KB // 256),
                  row(LANES, EV_KI // LANES), row(LANES, 0), row(LANES, 0)],
        out_specs=[row(1024, 0), row(1024, 0), row(256, 0), row(256, 0), row(LANES, 0), row(LANES, 0)],
        out_shape=[jax.ShapeDtypeStruct((m, 1024), BF16), jax.ShapeDtypeStruct((m, 1024), BF16),
                   jax.ShapeDtypeStruct((m, 256), F32), jax.ShapeDtypeStruct((m, 256), BF16),
                   jax.ShapeDtypeStruct((m, LANES), F32), jax.ShapeDtypeStruct((m, LANES), BF16)],
        compiler_params=_cparams(("parallel",)),
        name="even_post",
    )(h, h, h, h, cos, sin)


def _ordinal_to_f32(k):
    return pltpu.bitcast(jnp.where(k < 0, k ^ 0x7FFFFFFF, k), F32)


def _kth_largest(count_ge, n_sel, shape):
    n_f = float(n_sel)
    base = jnp.where(count_ge(jnp.zeros(shape, F32)) >= n_f, 0, INT_MIN).astype(I32)

    def bit_step(i, base):
        cand = base | (jnp.int32(1) << (30 - i))
        return jnp.where(count_ge(_ordinal_to_f32(cand)) >= n_f, cand, base)

    return _ordinal_to_f32(lax.fori_loop(0, 31, bit_step, base))


def _tie_cut(count_eq_before, need, n_cols_log2):
    def step(i, c):
        cand = c + (jnp.int32(1) << (n_cols_log2 - 1 - i))
        return jnp.where(count_eq_before(cand) < need, cand, c)
    return lax.fori_loop(0, n_cols_log2, step, jnp.zeros(need.shape, I32))


def _topk_select(score, cols, n_sel, n_cols_log2):
    rows = score.shape[0]
    count = lambda m: jnp.sum(m, axis=1, keepdims=True)
    thr = _kth_largest(lambda t: count(jnp.where(score >= t, 1.0, 0.0)), n_sel, (rows, 1))
    gt = jnp.where(score > thr, 1.0, 0.0)
    eq = jnp.where(score == thr, 1.0, 0.0)
    need = float(n_sel) - count(gt)
    c_all = jnp.full((rows, 1), (1 << n_cols_log2) - 1, I32)
    c_star = lax.cond(jnp.max(count(eq) - need) > 0.0,
                      lambda _: _tie_cut(lambda c: count(jnp.where(cols < c, eq, 0.0)), need, n_cols_log2),
                      lambda _: c_all, 0)
    return gt + jnp.where(cols <= c_star, eq, 0.0)


def _merge_head_pair(a, b, group, low):
    if group == 0:
        return jnp.where(low, a, pltpu.roll(b, B_HEAD_DIM, axis=1))
    return jnp.where(low, pltpu.roll(a, B_HEAD_DIM, axis=1), b)


DSA_KEY_BLOCK = 512
DSA_ATT_BLOCK = 512


def _transpose_bf16(x):
    return x.astype(F32).T.astype(BF16)


def _fold_keys(x, reduce):
    keys, nq = x.shape
    return reduce(reduce(x.reshape(keys // 64, 64, nq), axis=0), axis=0, keepdims=True)


def _dsa_prompt_kernel(qb_ref, qi_ref, wi_ref, kv_ref, ki_ref, o_ref, s_ref, acc_ref, qit_ref, qbt_ref,
                       *, n_seq, n_sel, n_cols_log2):
    kb_w = DSA_KEY_BLOCK
    nq = n_seq * ROW_TILE
    j = pl.program_id(1)
    nk = ((j + 1) * ROW_TILE + kb_w - 1) // kb_w
    per = B_HEADS // B_KV_HEADS
    key_in_block = lax.broadcasted_iota(I32, (kb_w, 1), 0)
    qpos1 = j * ROW_TILE + lax.broadcasted_iota(I32, (1, ROW_TILE), 1) - PAD_ROWS
    qpos = jnp.concatenate([qpos1] * n_seq, axis=1)
    key_sum = lambda a: jnp.sum(a, axis=0, keepdims=True)

    def key_rows(kb):
        return pl.ds(pl.multiple_of(kb * kb_w, kb_w), kb_w)

    w_rows = []
    for b in range(n_seq):
        w_rows.append(wi_ref[b].T * (IDX_HEADS ** -0.5))
        for h in range(IDX_HEADS):
            qit_ref[b, h // 2, :, (h % 2) * ROW_TILE:(h % 2 + 1) * ROW_TILE] = _transpose_bf16(
                qi_ref[b, :, h * LANES:(h + 1) * LANES])
        for h in range(B_HEADS):
            a, i = b * B_KV_HEADS + h // per, h % per
            qbt_ref[a, :, i * ROW_TILE:(i + 1) * ROW_TILE] = _transpose_bf16(qb_ref[b, :, h * LANES:(h + 1) * LANES])

    def score_block(kb, carry):
        kpos = kb * kb_w + key_in_block - PAD_ROWS
        for b in range(n_seq):
            ki = ki_ref[b, key_rows(kb), :]
            acc = jnp.zeros((kb_w, ROW_TILE), F32)
            for pair in range(IDX_HEADS // 2):
                d = jnp.maximum(jnp.dot(ki, qit_ref[b, pair], preferred_element_type=F32), 0.0)
                for sub in range(2):
                    h = 2 * pair + sub
                    acc = acc + w_rows[b][h:h + 1, :] * d[:, sub * ROW_TILE:(sub + 1) * ROW_TILE]
            s = jnp.where(kpos <= qpos1, acc, NEG_BIG)
            s_ref[kb, :, b * ROW_TILE:(b + 1) * ROW_TILE] = jnp.where(kpos >= 0, s, -jnp.inf)
        return carry
    lax.fori_loop(0, nk, score_block, 0)

    n_chain = 4

    def over_blocks(fn):
        def body(kb, a):
            f = fn(kb, s_ref[kb]).reshape(kb_w // (n_chain * SUBLANES), n_chain * SUBLANES, nq)
            return a + jnp.sum(f, axis=0)
        return key_sum(lax.fori_loop(0, nk, body, jnp.zeros((n_chain * SUBLANES, nq), F32)))

    takes_all = qpos < n_sel
    c_all = jnp.full((1, nq), (1 << n_cols_log2) - 1, I32)
    eq_unit = float(1 << n_cols_log2)

    def search(_):
        thr = _kth_largest(lambda t: over_blocks(lambda kb, s: jnp.where(s >= t, 1.0, 0.0)), n_sel, (1, nq))
        both = over_blocks(lambda kb, s: jnp.where(s > thr, 1.0, jnp.where(s == thr, eq_unit, 0.0)))
        n_eq = jnp.floor(both * (1.0 / eq_unit))
        need = float(n_sel) - (both - n_eq * eq_unit)

        def eq_before(c):
            return over_blocks(
                lambda kb, s: jnp.where(s == thr, jnp.where(kb * kb_w + key_in_block < c, 1.0, 0.0), 0.0))

        c_cut = lax.cond(jnp.max(jnp.where(takes_all, 0.0, n_eq - need)) > 0.0,
                         lambda _: _tie_cut(eq_before, need, n_cols_log2), lambda _: c_all, 0)
        return thr, c_cut

    last_qpos = j * ROW_TILE + (ROW_TILE - 1) - PAD_ROWS
    thr, c_star = lax.cond(last_qpos < n_sel, lambda _: (jnp.zeros((1, nq), F32), c_all), search, 0)

    n_att = n_seq * B_KV_HEADS
    acc_ref[...] = jnp.zeros_like(acc_ref)
    sub_w = DSA_ATT_BLOCK
    key_in_sub = key_in_block[:sub_w]

    def attend_block(kb, carry):
        carry = list(carry)
        for sub in range(kb_w // sub_w):
            lo = sub * sub_w
            s = s_ref[kb, lo:lo + sub_w, :]
            kidx = kb * kb_w + lo + key_in_sub
            kpos = kidx - PAD_ROWS
            picked = jnp.where(s > thr, 1.0, jnp.where(s == thr, jnp.where(kidx <= c_star, 1.0, 0.0), 0.0))
            picked = jnp.where(takes_all, 1.0, picked)
            valid = jnp.where(kpos >= 0, jnp.where(kpos <= qpos, picked, 0.0), 0.0)
            sub_rows = pl.ds(pl.multiple_of(kb * kb_w + lo, sub_w), sub_w)
            for b in range(n_seq):
                ok = jnp.concatenate([valid[:, b * ROW_TILE:(b + 1) * ROW_TILE]] * per, axis=1) > 0.5
                v_t = _transpose_bf16(kv_ref[b, sub_rows, LANES:])
                k_blk = kv_ref[b, sub_rows, :LANES]
                for g in range(B_KV_HEADS):
                    a = b * B_KV_HEADS + g
                    m_old, l_old = carry[2 * a], carry[2 * a + 1]
                    logits = jnp.dot(k_blk, qbt_ref[a], preferred_element_type=F32)
                    logits = jnp.where(ok, logits, NEG_BIG)
                    m_new = jnp.maximum(m_old, _fold_keys(logits, jnp.max))
                    alpha = jnp.exp(m_old - m_new)
                    p = jnp.exp(logits - m_new)
                    carry[2 * a] = m_new
                    carry[2 * a + 1] = alpha * l_old + _fold_keys(p, jnp.sum)
                    acc_ref[a] = acc_ref[a] * alpha + jnp.dot(v_t, p.astype(BF16), preferred_element_type=F32)
        return tuple(carry)

    init = []
    for a in range(n_att):
        init += [jnp.full((1, per * ROW_TILE), -jnp.inf, F32), jnp.zeros((1, per * ROW_TILE), F32)]
    stats = lax.fori_loop(0, nk, attend_block, tuple(init))

    low = lax.broadcasted_iota(I32, (1, LANES), 1) < B_HEAD_DIM
    for a in range(n_att):
        b, g = a // B_KV_HEADS, a % B_KV_HEADS
        o_t = acc_ref[a] / stats[2 * a + 1]
        o = [o_t[:, i * ROW_TILE:(i + 1) * ROW_TILE].T for i in range(per)]
        for pair in range(per // 2):
            c = (g * per) // 2 + pair
            o_ref[b, :, c * LANES:(c + 1) * LANES] = _merge_head_pair(o[2 * pair], o[2 * pair + 1], g, low)


def _dsa_prompt(qbx, qix, h, kvb, kib, n_batch, n_chunks):
    t = n_chunks * ROW_TILE
    n_seq = 2 if n_batch % 2 == 0 else 1
    n_kb = -(-t // DSA_KEY_BLOCK)
    tk = n_kb * DSA_KEY_BLOCK
    seq3 = lambda a: a.reshape(n_batch, t, a.shape[-1])
    pad_keys = lambda a: jnp.pad(seq3(a), ((0, 0), (0, tk - t), (0, 0)))
    rowblk = lambda w, cb: pl.BlockSpec((n_seq, ROW_TILE, w), lambda b, j: (b, j, cb))
    out = pl.pallas_call(
        functools.partial(_dsa_prompt_kernel, n_seq=n_seq, n_sel=min(TOPK_MAX, (t - PAD_ROWS) // 4),
                          n_cols_log2=math.ceil(math.log2(tk))),
        grid=(n_batch // n_seq, n_chunks),
        in_specs=[rowblk(1024, 0), rowblk(1024, 0), rowblk(LANES, EV_WI // LANES),
                  pl.BlockSpec((n_seq, tk, 256), lambda b, j: (b, 0, 0)),
                  pl.BlockSpec((n_seq, tk, LANES), lambda b, j: (b, 0, 0))],
        out_specs=rowblk(512, 0),
        out_shape=jax.ShapeDtypeStruct((n_batch, t, B_WIDTH), F32),
        scratch_shapes=[pltpu.VMEM((n_kb, DSA_KEY_BLOCK, n_seq * ROW_TILE), F32),
                        pltpu.VMEM((n_seq * B_KV_HEADS, LANES, (B_HEADS // B_KV_HEADS) * ROW_TILE), F32),
                        pltpu.VMEM((n_seq, IDX_HEADS // 2, LANES, 2 * ROW_TILE), BF16),
                        pltpu.VMEM((n_seq * B_KV_HEADS, LANES, (B_HEADS // B_KV_HEADS) * ROW_TILE), BF16)],
        compiler_params=_cparams(("parallel", "arbitrary")),
        name="dsa_prompt",
    )(seq3(qbx), seq3(qix), seq3(h), pad_keys(kvb), pad_keys(kib))
    return out.reshape(n_batch * t, B_WIDTH)


DS_SEQ = 2


def _page_specs(n_seq, n_pages, width, col_block):
    return [pl.BlockSpec((1, PAGE_SIZE, width),
                         functools.partial(lambda i, pt, s, p: (pt[i * n_seq + s, p], 0, col_block), s=s, p=p))
            for s in range(n_seq) for p in range(n_pages)]


def _dsa_sample_score_kernel(pt_ref, q_ref, w_ref, qx_ref, kx_ref, *refs, n_seq, n_pages, n_cols):
    del pt_ref
    page_refs, o_ref = refs[:n_seq * n_pages], refs[n_seq * n_pages]
    nt = (((1,), (1,)), ((), ()))
    lane = lax.broadcasted_iota(I32, (1, LANES), 1)
    for s in range(n_seq):
        q = q_ref[s]
        w = w_ref[s] * (IDX_HEADS ** -0.5)
        pieces = []
        for p in range(n_pages):
            d = lax.dot_general(q, page_refs[s * n_pages + p][0].astype(BF16), nt,
                                preferred_element_type=F32)
            pieces.append(jnp.sum(w * jnp.maximum(d, 0.0), axis=0, keepdims=True))
        d_self = jnp.sum(qx_ref[s].astype(F32) * kx_ref[s].astype(F32), axis=1, keepdims=True)
        s_self = jnp.sum(w[:, :1] * jnp.maximum(d_self, 0.0), axis=0, keepdims=True)
        pieces.append(jnp.where(lane == 0, s_self, 0.0))
        pad = n_cols - (n_pages + 1) * LANES
        if pad:
            pieces.append(jnp.zeros((1, pad), F32))
        o_ref[s] = jnp.concatenate(pieces, axis=1)


def _dsa_sample_scores(page_table, q_is, w_ib, q_ix, k_ib, cache_kidx2, n_cols):
    db, n_pages = page_table.shape
    n_seq = DS_SEQ if db % DS_SEQ == 0 else 1
    per_seq = lambda shape: pl.BlockSpec((n_seq,) + shape, lambda i, pt: (i, 0, 0))
    grid_spec = pltpu.PrefetchScalarGridSpec(
        num_scalar_prefetch=1, grid=(db // n_seq,),
        in_specs=[per_seq((IDX_HEADS, LANES)), per_seq((IDX_HEADS, LANES)), per_seq((IDX_HEADS, LANES)),
                  per_seq((1, LANES))] + _page_specs(n_seq, n_pages, LANES, 0),
        out_specs=per_seq((1, n_cols)))
    return pl.pallas_call(
        functools.partial(_dsa_sample_score_kernel, n_seq=n_seq, n_pages=n_pages, n_cols=n_cols),
        grid_spec=grid_spec,
        out_shape=jax.ShapeDtypeStruct((db, 1, n_cols), F32),
        compiler_params=_cparams(("arbitrary",)),
        name="dsa_sample_scores",
    )(page_table, q_is, w_ib, q_ix, k_ib, *([cache_kidx2] * (n_seq * n_pages)))


def _dsa_sample_select_kernel(s_ref, o_ref, *, n_keys, n_sel, n_cols_log2):
    cols = lax.broadcasted_iota(I32, (1, s_ref.shape[1]), 1)
    score = jnp.where(cols < n_keys, s_ref[...], -jnp.inf)
    o_ref[...] = _topk_select(score, cols, n_sel, n_cols_log2)


def _dsa_sample_select(scores, n_keys):
    db, n_cols = scores.shape
    return pl.pallas_call(
        functools.partial(_dsa_sample_select_kernel, n_keys=n_keys, n_sel=min(TOPK_MAX, n_keys // 4),
                          n_cols_log2=math.ceil(math.log2(n_cols))),
        out_shape=jax.ShapeDtypeStruct((db, n_cols), F32),
        compiler_params=pltpu.CompilerParams(vmem_limit_bytes=VMEM_LIMIT_BYTES),
        name="dsa_sample_select",
    )(scores)


def _dsa_sample_attend_kernel(pt_ref, q_ref, m_ref, kn_ref, *refs, n_seq, n_pages):
    del pt_ref
    n_blocks = n_seq * n_pages
    k_refs, v_refs, o_ref = refs[:n_blocks], refs[n_blocks:2 * n_blocks], refs[2 * n_blocks]
    nt = (((1,), (1,)), ((), ()))
    for s in range(n_seq):
        q = q_ref[s]
        mask = m_ref[s]
        logits = []
        for p in range(n_pages):
            l_p = lax.dot_general(q, k_refs[s * n_pages + p][0].astype(BF16), nt,
                                  preferred_element_type=F32)
            logits.append(jnp.where(mask[:, p * LANES:(p + 1) * LANES] > 0.5, l_p, NEG_BIG))
        kn = kn_ref[s]
        k_new = kn[:, :LANES].astype(BF16).astype(F32)
        v_new = kn[:, LANES:].astype(BF16).astype(F32)
        l_self = jnp.sum(q.astype(F32) * k_new, axis=1, keepdims=True)
        l_self = jnp.where(mask[:, n_pages * LANES:n_pages * LANES + 1] > 0.5, l_self, NEG_BIG)
        mx = l_self
        for l_p in logits:
            mx = jnp.maximum(mx, jnp.max(l_p, axis=1, keepdims=True))
        p_self = jnp.exp(l_self - mx)
        den = p_self
        acc = p_self * v_new
        for p in range(n_pages):
            w_p = jnp.exp(logits[p] - mx)
            den = den + jnp.sum(w_p, axis=1, keepdims=True)
            acc = acc + jnp.dot(w_p.astype(BF16), v_refs[s * n_pages + p][0].astype(BF16),
                                preferred_element_type=F32)
        o_ref[s] = acc / den


def _dsa_sample_attend(page_table, q8, mask, kv_new, cache_k2, cache_v2, layer):
    db, n_pages = page_table.shape
    n_seq = DS_SEQ if db % DS_SEQ == 0 else 1
    n_cols = mask.shape[-1]
    per_seq = lambda shape: pl.BlockSpec((n_seq,) + shape, lambda i, pt: (i, 0, 0))
    grid_spec = pltpu.PrefetchScalarGridSpec(
        num_scalar_prefetch=1, grid=(db // n_seq,),
        in_specs=[per_seq((B_HEADS, LANES)), per_seq((1, n_cols)), per_seq((1, 2 * LANES))]
        + _page_specs(n_seq, n_pages, LANES, layer) + _page_specs(n_seq, n_pages, LANES, layer),
        out_specs=per_seq((B_HEADS, LANES)))
    return pl.pallas_call(
        functools.partial(_dsa_sample_attend_kernel, n_seq=n_seq, n_pages=n_pages),
        grid_spec=grid_spec,
        out_shape=jax.ShapeDtypeStruct((db, B_HEADS, LANES), F32),
        compiler_params=_cparams(("arbitrary",)),
        name="dsa_sample_attend",
    )(page_table, q8, mask, kv_new, *([cache_k2] * (n_seq * n_pages)), *([cache_v2] * (n_seq * n_pages)))


def _even_out_kernel(oa_ref, ga_ref, ob_ref, x_ref, w_ref, ng_ref, lg_ref, lb_ref, o_ref, *, tm, seq_pad):
    ng = ng_ref[...]
    acc = jnp.dot(ob_ref[...].astype(BF16), w_ref[A_WIDTH:, :], preferred_element_type=F32)
    for h in range(A_HEADS):
        sl = slice(h * A_DV, (h + 1) * A_DV)
        oa = oa_ref[:, sl]
        oa = oa * lax.rsqrt(jnp.mean(oa * oa, axis=-1, keepdims=True) + RMS_EPS) * ng
        ga = ga_ref[:, sl]
        oa = oa * (ga * _sigmoid(ga))
        acc = acc + jnp.dot(oa.astype(BF16), w_ref[sl, :], preferred_element_type=F32)
    y = DN_ALPHA * x_ref[...] + acc
    o_ref[...] = _finish_rows(y, lg_ref[...], lb_ref[...], pl.program_id(0), tm, seq_pad)


def _even_out(oa, h, ob, x, w, ng, lg, lb, tm, seq_pad):
    m = x.shape[0]
    row = lambda w_, cb: pl.BlockSpec((tm, w_), lambda i: (i, cb))
    full = lambda a: pl.BlockSpec(a.shape, lambda i: (0,) * a.ndim)
    return pl.pallas_call(
        functools.partial(_even_out_kernel, tm=tm, seq_pad=seq_pad),
        grid=(m // tm,),
        in_specs=[row(A_WIDTH, 0), row(A_WIDTH, EV_GA // A_WIDTH), row(B_WIDTH, 0), row(D_MODEL, 0),
                  full(w), full(ng), full(lg), full(lb)],
        out_specs=row(D_MODEL, 0),
        out_shape=jax.ShapeDtypeStruct((m, D_MODEL), F32),
        compiler_params=_cparams(("parallel",)),
        name="even_out",
    )(oa, h, ob, x, w, ng, lg, lb)


RT_EXPERT0 = N_GROUPS


def _lane_argmax(v, lane):
    mx = jnp.max(v, axis=-1, keepdims=True)
    idx = jnp.min(jnp.where(v == mx, lane, float(LANES)), axis=-1, keepdims=True)
    return mx, idx


def _router_gates(x, wrh_ref, wrl_ref, rb_ref):
    xh = x.astype(BF16)
    xl = (x - xh.astype(F32)).astype(BF16)
    logits = (jnp.dot(xh, wrh_ref[...], preferred_element_type=F32)
              + jnp.dot(xl, wrh_ref[...], preferred_element_type=F32)
              + jnp.dot(xh, wrl_ref[...], preferred_element_type=F32)) + rb_ref[...]
    lane_i = lax.broadcasted_iota(I32, logits.shape, 1)
    lane = lane_i.astype(F32)
    neg_inf = -jnp.inf
    g_logits = jnp.where(lane_i < N_GROUPS, logits, neg_inf)
    g_max, g_idx = _lane_argmax(g_logits, lane)
    g_val = 1.0 / jnp.sum(jnp.exp(g_logits - g_max), axis=-1, keepdims=True)
    e_lane = lane_i - RT_EXPERT0
    lane_group = jnp.where(e_lane >= 0, e_lane >> 2, -1)
    lane_group = jnp.where(lane_i < RT_EXPERT0 + N_EXPERTS, lane_group, -1).astype(F32)
    e_logits = jnp.where(lane_group == g_idx, logits, neg_inf)
    e_max, first = _lane_argmax(e_logits, lane)
    p = jnp.exp(e_logits - e_max)
    p = p / jnp.sum(p, axis=-1, keepdims=True)
    p1 = jnp.sum(jnp.where(lane == first, p, 0.0), axis=-1, keepdims=True)
    rest = jnp.where(lane == first, neg_inf, jnp.where(lane_group == g_idx, p, neg_inf))
    p2, second = _lane_argmax(rest, lane)
    scale = g_val / (p1 + p2)
    return jnp.where(lane == first, p1 * scale, jnp.where(lane == second, p2 * scale, 0.0))


MOE_EXPERTS_PER_STEP = EXPERTS_PER_GROUP


def _moe_kernel(x_ref, wrh_ref, wrl_ref, rb_ref, wgu_ref, wd_ref, lg_ref, lb_ref, o_ref, acc_ref, gate_ref, xb_ref,
                *, tm, seq_pad):
    step = pl.program_id(1)

    @pl.when(step == 0)
    def _():
        gate_ref[...] = _router_gates(x_ref[...], wrh_ref, wrl_ref, rb_ref)
        acc_ref[...] = jnp.zeros_like(acc_ref)
        xb_ref[...] = x_ref[...].astype(BF16)

    lane = lax.broadcasted_iota(I32, (1, LANES), 1)
    for j in range(MOE_EXPERTS_PER_STEP):
        e = step * MOE_EXPERTS_PER_STEP + j
        gate = jnp.sum(jnp.where(lane == e + RT_EXPERT0, gate_ref[...], 0.0), axis=-1, keepdims=True)
        hgu = jnp.dot(xb_ref[...], wgu_ref[j], preferred_element_type=F32)
        hg = hgu[:, :EXPERT_FF]
        act = (hg * _sigmoid(hg)) * hgu[:, EXPERT_FF:] * gate
        acc_ref[...] += jnp.dot(act.astype(BF16), wd_ref[j], preferred_element_type=F32)

    @pl.when(step == pl.num_programs(1) - 1)
    def _():
        y = DN_ALPHA * x_ref[...] + acc_ref[...]
        o_ref[...] = _finish_rows(y, lg_ref[...], lb_ref[...], pl.program_id(0), tm, seq_pad)


def _moe(x, p, lg, lb, tm, seq_pad):
    m = x.shape[0]
    eps = MOE_EXPERTS_PER_STEP
    full = lambda a: pl.BlockSpec(a.shape, lambda i, e: (0,) * a.ndim)
    return pl.pallas_call(
        functools.partial(_moe_kernel, tm=tm, seq_pad=seq_pad),
        grid=(m // tm, N_EXPERTS // eps),
        in_specs=[pl.BlockSpec((tm, D_MODEL), lambda i, e: (i, 0)),
                  full(p["wrh"]), full(p["wrl"]), full(p["rb"]),
                  pl.BlockSpec((eps, D_MODEL, 2 * EXPERT_FF), lambda i, e: (e, 0, 0)),
                  pl.BlockSpec((eps, EXPERT_FF, D_MODEL), lambda i, e: (e, 0, 0)),
                  full(lg), full(lb)],
        out_specs=pl.BlockSpec((tm, D_MODEL), lambda i, e: (i, 0)),
        out_shape=jax.ShapeDtypeStruct((m, D_MODEL), F32),
        scratch_shapes=[pltpu.VMEM((tm, D_MODEL), F32), pltpu.VMEM((tm, LANES), F32),
                        pltpu.VMEM((tm, D_MODEL), BF16)],
        compiler_params=_cparams(("parallel", "arbitrary")),
        name="moe",
    )(x, p["wrh"], p["wrl"], p["rb"], p["wgu"], p["wd"], lg, lb)


def _pack_moe_params(rg_w, rg_b, re_w, re_b, w_gate, w_up, w_down):
    d = rg_w.shape[0]
    wr = jnp.concatenate([rg_w, re_w, jnp.zeros((d, LANES - N_GROUPS - N_EXPERTS), F32)], axis=1).astype(F32)
    wrh = wr.astype(BF16)
    wrl = (wr - wrh.astype(F32)).astype(BF16)
    rb = jnp.concatenate([rg_b, re_b, jnp.zeros((LANES - N_GROUPS - N_EXPERTS,), F32)]).reshape(1, LANES)
    return dict(wrh=wrh, wrl=wrl, rb=rb.astype(F32),
                wgu=jnp.concatenate([w_gate, w_up], axis=2).astype(BF16), wd=w_down.astype(BF16))


C_HIST = 32
D_HIST = 8


def _softplus(x):
    return jnp.maximum(x, 0.0) + jnp.log1p(jnp.exp(-jnp.abs(x)))


def _gelu_tanh(x):
    return 0.5 * x * (1.0 + jnp.tanh(math.sqrt(2.0 / math.pi) * (x + 0.044715 * (x * x * x))))


def _lru_gates(xc, wab_ref, ba, bx, lam):
    proj = jnp.dot(xc.astype(BF16), wab_ref[...], preferred_element_type=F32)
    r = _sigmoid(proj[:, :D_WIDTH] + ba)
    ig = _sigmoid(proj[:, D_WIDTH:] + bx)
    log_a = -LRU_C * r * _softplus(-lam)
    a = jnp.exp(log_a)
    th = jnp.tanh(log_a)
    drive = jnp.sqrt(jnp.maximum(-2.0 * th / (1.0 - th), 0.0)) * ig * xc
    return a, drive


def _odd_seq_kernel(ca_ref, cg_ref, dx_ref, dg_ref, cw_ref, cb_ref, lng_ref, lnb_ref, dw_ref, db_ref,
                    wab_ref, ba_ref, bx_ref, lam_ref,
                    y_ref, cst_ref, dst_ref, hst_ref, uext, dext, hc):
    c = pl.program_id(1)

    @pl.when(c == 0)
    def _():
        uext[:C_HIST, :] = jnp.zeros((C_HIST, C_WIDTH), F32)
        dext[:D_HIST, :] = jnp.zeros((D_HIST, D_WIDTH), F32)
        hc[...] = jnp.zeros_like(hc)

    uext[C_HIST:, :] = ca_ref[...] * _sigmoid(cg_ref[...])
    acc = jnp.zeros((ROW_TILE, C_WIDTH), F32)
    for j in range(C_CONV):
        off = C_HIST - (C_CONV - 1) + j
        acc = acc + cw_ref[j:j + 1, :] * uext[off:off + ROW_TILE, :]
    yc = _layer_norm(acc + cb_ref[...], lng_ref[...], lnb_ref[...])
    y_ref[:, :C_WIDTH] = yc * _sigmoid(yc)

    dext[D_HIST:, :] = dx_ref[...]
    xc = jnp.zeros((ROW_TILE, D_WIDTH), F32)
    for j in range(D_CONV):
        off = D_HIST - (D_CONV - 1) + j
        xc = xc + dw_ref[j:j + 1, :] * dext[off:off + ROW_TILE, :]
    xc = xc + db_ref[...]
    a, u = _lru_gates(xc, wab_ref, ba_ref[...], bx_ref[...], lam_ref[...])
    rows = lax.broadcasted_iota(I32, (ROW_TILE, 1), 0)
    u = jnp.where(jnp.logical_and(c == 0, rows < PAD_ROWS), 0.0, u)
    d = 1
    while d < ROW_TILE:
        head = rows < d
        a_prev = jnp.where(head, 1.0, pltpu.roll(a, d, axis=0))
        u_prev = jnp.where(head, 0.0, pltpu.roll(u, d, axis=0))
        u = u + a * u_prev
        a = a * a_prev
        d *= 2
    hs = a * hc[...] + u
    y_ref[:, C_WIDTH:] = hs * _gelu_tanh(dg_ref[...])

    hc[...] = hs[ROW_TILE - 1:ROW_TILE, :]
    uext[:C_HIST, :] = uext[ROW_TILE:ROW_TILE + C_HIST, :]
    dext[:D_HIST, :] = dext[ROW_TILE:ROW_TILE + D_HIST, :]

    @pl.when(c == pl.num_programs(1) - 1)
    def _():
        cst_ref[0] = uext[C_HIST + ROW_TILE - (C_CONV - 1):C_HIST + ROW_TILE, :]
        dst_ref[0] = dext[D_HIST + ROW_TILE - (D_CONV - 1):D_HIST + ROW_TILE, :]
        hst_ref[0] = hs[ROW_TILE - 1:ROW_TILE, :]


def _odd_seq(h, p, n_batch, n_chunks):
    m = h.shape[0]
    blk = lambda cb: pl.BlockSpec((ROW_TILE, 512), lambda b, c: (b * n_chunks + c, cb))
    full = lambda a: pl.BlockSpec(a.shape, lambda b, c: (0,) * a.ndim)
    params = [p["cw"], p["cb"], p["lng"], p["lnb"], p["dw"], p["db"], p["wab"], p["ba"], p["bx"], p["lam"]]
    state = lambda r: pl.BlockSpec((1, r, 512), lambda b, c: (b, 0, 0))
    return pl.pallas_call(
        _odd_seq_kernel,
        grid=(n_batch, n_chunks),
        in_specs=[blk(0), blk(1), blk(2), blk(3)] + [full(a) for a in params],
        out_specs=[pl.BlockSpec((ROW_TILE, 1024), lambda b, c: (b * n_chunks + c, 0)),
                   state(C_CONV - 1), state(D_CONV - 1), state(1)],
        out_shape=[jax.ShapeDtypeStruct((m, C_WIDTH + D_WIDTH), F32),
                   jax.ShapeDtypeStruct((n_batch, C_CONV - 1, C_WIDTH), F32),
                   jax.ShapeDtypeStruct((n_batch, D_CONV - 1, D_WIDTH), F32),
                   jax.ShapeDtypeStruct((n_batch, 1, D_WIDTH), F32)],
        scratch_shapes=[pltpu.VMEM((C_HIST + ROW_TILE, C_WIDTH), F32),
                        pltpu.VMEM((D_HIST + ROW_TILE, D_WIDTH), F32),
                        pltpu.VMEM((1, D_WIDTH), F32)],
        compiler_params=_cparams(("parallel", "arbitrary")),
        name="odd_seq",
    )(h, h, h, h, *params)


def _pack_odd_params(cw, cb, lng, lnb, dw, db, wa, ba, wx, bx, lam):
    def block_diag(w):
        out = jnp.zeros((D_WIDTH, D_WIDTH), w.dtype)
        for n in range(D_BLOCKS):
            out = out.at[n * D_BLOCK_W:(n + 1) * D_BLOCK_W, n * D_BLOCK_W:(n + 1) * D_BLOCK_W].set(w[n])
        return out
    row = lambda v: v.reshape(1, -1).astype(F32)
    return dict(cw=cw.astype(F32), cb=row(cb), lng=row(lng), lnb=row(lnb), dw=dw.astype(F32), db=row(db),
                wab=jnp.concatenate([block_diag(wa), block_diag(wx)], axis=1).astype(BF16),
                ba=row(ba), bx=row(bx), lam=row(lam))


def _odd_sample_kernel(ca_ref, cg_ref, dx_ref, dg_ref, cs_ref, ds_ref, h0_ref,
                       cw_ref, cb_ref, lng_ref, lnb_ref, dw_ref, db_ref, wab_ref, ba_ref, bx_ref, lam_ref,
                       y_ref, cso_ref, dso_ref, ho_ref):
    u = ca_ref[...] * _sigmoid(cg_ref[...])
    acc = cw_ref[C_CONV - 1:C_CONV, :] * u
    for j in range(C_CONV - 1):
        acc = acc + cw_ref[j:j + 1, :] * cs_ref[j]
        if j > 0:
            cso_ref[j - 1] = cs_ref[j]
    cso_ref[C_CONV - 2] = u
    yc = _layer_norm(acc + cb_ref[...], lng_ref[...], lnb_ref[...])
    y_ref[:, :C_WIDTH] = yc * _sigmoid(yc)

    dx = dx_ref[...]
    xc = dw_ref[D_CONV - 1:D_CONV, :] * dx
    for j in range(D_CONV - 1):
        xc = xc + dw_ref[j:j + 1, :] * ds_ref[j]
        if j > 0:
            dso_ref[j - 1] = ds_ref[j]
    dso_ref[D_CONV - 2] = dx
    xc = xc + db_ref[...]
    a, drive = _lru_gates(xc, wab_ref, ba_ref[...], bx_ref[...], lam_ref[...])
    h = a * h0_ref[...] + drive
    ho_ref[...] = h
    y_ref[:, C_WIDTH:] = h * _gelu_tanh(dg_ref[...])


def _odd_sample(h, cs_t, ds_t, h0, p):
    db = h.shape[0]
    params = [p["cw"], p["cb"], p["lng"], p["lnb"], p["dw"], p["db"], p["wab"], p["ba"], p["bx"], p["lam"]]
    full = lambda a: pl.BlockSpec(a.shape, lambda i: (0,) * a.ndim)
    blk = lambda cb: pl.BlockSpec((db, 512), lambda i: (0, cb))
    return pl.pallas_call(
        _odd_sample_kernel,
        grid=(1,),
        in_specs=[blk(0), blk(1), blk(2), blk(3), full(cs_t), full(ds_t), full(h0)] + [full(a) for a in params],
        out_specs=[pl.BlockSpec((db, 1024), lambda i: (0, 0)), full(cs_t), full(ds_t), full(h0)],
        out_shape=[jax.ShapeDtypeStruct((db, C_WIDTH + D_WIDTH), F32),
                   jax.ShapeDtypeStruct(cs_t.shape, F32), jax.ShapeDtypeStruct(ds_t.shape, F32),
                   jax.ShapeDtypeStruct(h0.shape, F32)],
        compiler_params=_cparams(("arbitrary",)),
        name="odd_sample",
    )(h, h, h, h, cs_t, ds_t, h0, *params)


def _mm_postnorm_kernel(a_ref, x_ref, w_ref, lg_ref, lb_ref, o_ref, *, tm, seq_pad):
    acc = jnp.dot(a_ref[...].astype(BF16), w_ref[...], preferred_element_type=F32)
    y = DN_ALPHA * x_ref[...] + acc
    o_ref[...] = _finish_rows(y, lg_ref[...], lb_ref[...], pl.program_id(0), tm, seq_pad)


def _mm_postnorm(a, x, w, lg, lb, tm, seq_pad):
    m = x.shape[0]
    row = lambda w_: pl.BlockSpec((tm, w_), lambda i: (i, 0))
    full = lambda arr: pl.BlockSpec(arr.shape, lambda i: (0,) * arr.ndim)
    return pl.pallas_call(
        functools.partial(_mm_postnorm_kernel, tm=tm, seq_pad=seq_pad),
        grid=(m // tm,),
        in_specs=[row(a.shape[1]), row(D_MODEL), full(w), full(lg), full(lb)],
        out_specs=row(D_MODEL),
        out_shape=jax.ShapeDtypeStruct((m, D_MODEL), F32),
        compiler_params=_cparams(("parallel",)),
        name="mm_postnorm",
    )(a, x, w, lg, lb)


def _row_tile(m):
    for tm in (512, 256, ROW_TILE):
        if m % tm == 0:
            return tm
    return m


def _moe_tile(m):
    for tm in (1024, 512, 256, ROW_TILE):
        if m % tm == 0:
            return tm
    return m


def _even_sample_attention(hs, qbx, qix, kvf, kib, cache_k2, cache_v2, cache_kidx2, page_table, layer):
    db, n_pages = page_table.shape
    past = n_pages * PAGE_SIZE
    n_keys = past + 1
    n_cols = -(-(n_keys) // LANES) * LANES
    qi = qix.reshape(db, IDX_HEADS, 2, IDX_DIM)
    qi = jnp.stack([qi[:, h, h % 2] for h in range(IDX_HEADS)], axis=1)
    zeros = jnp.zeros_like(qi)
    q_is = jnp.concatenate([qi, zeros] if layer == 0 else [zeros, qi], axis=-1)
    w_ib = jnp.broadcast_to(hs[:, EV_WI:EV_WI + IDX_HEADS, None], (db, IDX_HEADS, LANES))
    scores = _dsa_sample_scores(page_table, q_is, w_ib, qix.reshape(db, IDX_HEADS, LANES),
                                kib.reshape(db, 1, LANES), cache_kidx2, n_cols)
    mask = _dsa_sample_select(scores.reshape(db, n_cols), n_keys)
    o8 = _dsa_sample_attend(page_table, qbx.reshape(db, B_HEADS, LANES), mask.reshape(db, 1, n_cols),
                            kvf.reshape(db, 1, 2 * LANES), cache_k2, cache_v2, layer)
    per = B_HEADS // B_KV_HEADS
    halves = [o8[:, h, (h // per) * B_HEAD_DIM:(h // per + 1) * B_HEAD_DIM] for h in range(B_HEADS)]
    return jnp.concatenate(halves, axis=-1)


def kernel(x_prompt, x_sample, cache_k, cache_v, cache_kidx, state_hgrn, state_conv_c, state_conv_d, state_lru,
           page_table, meta_tokens, w_in_even, w_out_even, hgrn_lb_logits, hgrn_norm_g, w_in_odd, w_out_odd,
           conv_c_w, conv_c_b, conv_c_ln_g, conv_c_ln_b, conv_d_w, conv_d_b, lru_wa, lru_ba, lru_wx, lru_bx,
           lru_lambda, ln1_g, ln1_b, ln2_g, ln2_b, router_g_w, router_g_b, router_e_w, router_e_b,
           w_gate, w_up, w_down):
    bsz, seq, _ = x_prompt.shape
    dbsz, dseq, _ = x_sample.shape
    assert dseq == 1 and seq % ROW_TILE == 0 and dbsz % HS_TB == 0
    t_real = N_META + seq
    n_chunks = (PAD_ROWS + t_real) // ROW_TILE
    t_pad = n_chunks * ROW_TILE
    n_phys = cache_k.shape[0]
    past_len = page_table.shape[1] * PAGE_SIZE
    n_even = cache_k.shape[2]

    meta = jnp.broadcast_to(meta_tokens.astype(F32)[None], (bsz, N_META, D_MODEL))
    xp = jnp.concatenate([jnp.zeros((bsz, PAD_ROWS, D_MODEL), F32), meta, x_prompt.astype(F32)], axis=1)
    xp = xp.reshape(bsz * t_pad, D_MODEL)
    xs = x_sample.reshape(dbsz, D_MODEL).astype(F32)
    tm_p, tm_s = _row_tile(bsz * t_pad), _row_tile(dbsz)

    cos_p, sin_p = _rope_tables(np.maximum(np.arange(t_pad) - PAD_ROWS, 0))
    cos_p, sin_p = jnp.tile(jnp.asarray(cos_p), (bsz, 1)), jnp.tile(jnp.asarray(sin_p), (bsz, 1))
    cos_s, sin_s = _rope_tables(np.full((dbsz,), past_len))
    cos_s, sin_s = jnp.asarray(cos_s), jnp.asarray(sin_s)

    sm = jax.nn.softmax(hgrn_lb_logits.astype(F32), axis=0)
    lower_bounds = jnp.cumsum(sm, axis=0) - sm[0]

    cache_k2 = cache_k.reshape(n_phys, PAGE_SIZE, n_even * B_KV_WIDTH)
    cache_v2 = cache_v.reshape(n_phys, PAGE_SIZE, n_even * B_KV_WIDTH)
    cache_kidx2 = cache_kidx.reshape(n_phys, PAGE_SIZE, n_even * IDX_DIM)
    assert n_even * IDX_DIM == LANES and B_KV_WIDTH == LANES

    row2 = lambda v: v.reshape(1, -1).astype(F32)
    unpad = lambda a, w: a.reshape(bsz, t_pad, w)[:, PAD_ROWS:]
    kp, vp, ip, hp, cp, dp, lp = [], [], [], [], [], [], []
    ks, vs, iks, hsm, csm, dsm, lsm = [], [], [], [], [], [], []
    for layer in range(DEPTH):
        li = layer // 2
        lg1, lb1 = row2(ln1_g[layer]), row2(ln1_b[layer])
        if layer % 2 == 0:
            w_in = _pack_even_weight(w_in_even[li])
            w_out = w_out_even[li].astype(BF16)
            lb = lower_bounds[li].reshape(1, A_WIDTH)
            ng = row2(hgrn_norm_g[li])
            h = _matmul(xp, w_in, tm_p, 512)
            qbx, qix, kvf, kvb, kif, kib = _even_post(h, cos_p, sin_p, tm_p)
            oa, s_p = _hgrn_prompt(h, lb, bsz, n_chunks)
            ob = _dsa_prompt(qbx, qix, h, kvb, kib, bsz, n_chunks)
            xp = _even_out(oa, h, ob, xp, w_out, ng, lg1, lb1, tm_p, t_pad)
            kp.append(unpad(kvf[:, :LANES], LANES).reshape(bsz, t_real, B_KV_HEADS, B_HEAD_DIM))
            vp.append(unpad(kvf[:, LANES:], LANES).reshape(bsz, t_real, B_KV_HEADS, B_HEAD_DIM))
            ip.append(unpad(kif[:, :IDX_DIM], IDX_DIM))
            hp.append(s_p)
            h = _matmul(xs, w_in, tm_s, 512)
            qbx, qix, kvf, kvb, kif, kib = _even_post(h, cos_s, sin_s, tm_s)
            oa, s_s = _hgrn_sample(h, lb, state_hgrn, li)
            ob = _even_sample_attention(h, qbx, qix, kvf, kib, cache_k2, cache_v2, cache_kidx2, page_table, li)
            xs = _even_out(oa, h, ob, xs, w_out, ng, lg1, lb1, tm_s, None)
            ks.append(kvf[:, :LANES].reshape(dbsz, 1, B_KV_HEADS, B_HEAD_DIM))
            vs.append(kvf[:, LANES:].reshape(dbsz, 1, B_KV_HEADS, B_HEAD_DIM))
            iks.append(kif[:, :IDX_DIM].reshape(dbsz, 1, IDX_DIM))
            hsm.append(s_s)
        else:
            w_in = w_in_odd[li].astype(BF16)
            w_out = w_out_odd[li].astype(BF16)
            p = _pack_odd_params(conv_c_w[li], conv_c_b[li], conv_c_ln_g[li], conv_c_ln_b[li], conv_d_w[li],
                                 conv_d_b[li], lru_wa[li], lru_ba[li], lru_wx[li], lru_bx[li], lru_lambda[li])
            h = _matmul(xp, w_in, tm_p, 512)
            y, c_p, d_p, h_p = _odd_seq(h, p, bsz, n_chunks)
            xp = _mm_postnorm(y, xp, w_out, lg1, lb1, tm_p, t_pad)
            cp.append(c_p); dp.append(d_p); lp.append(h_p[:, 0])
            h = _matmul(xs, w_in, tm_s, 512)
            y, c_s, d_s, h_s = _odd_sample(h, jnp.swapaxes(state_conv_c[:, li], 0, 1).astype(F32),
                                           jnp.swapaxes(state_conv_d[:, li], 0, 1).astype(F32),
                                           state_lru[:, li].astype(F32), p)
            xs = _mm_postnorm(y, xs, w_out, lg1, lb1, tm_s, None)
            csm.append(jnp.swapaxes(c_s, 0, 1)); dsm.append(jnp.swapaxes(d_s, 0, 1)); lsm.append(h_s)
        mp = _pack_moe_params(router_g_w[layer], router_g_b[layer], router_e_w[layer], router_e_b[layer],
                              w_gate[layer], w_up[layer], w_down[layer])
        lg2, lb2 = row2(ln2_g[layer]), row2(ln2_b[layer])
        xp = _moe(xp, mp, lg2, lb2, _moe_tile(bsz * t_pad), t_pad)
        xs = _moe(xs, mp, lg2, lb2, _moe_tile(dbsz), None)

    y_prompt = xp.reshape(bsz, t_pad, D_MODEL)[:, PAD_ROWS + N_META:]
    y_sample = xs.reshape(dbsz, 1, D_MODEL)
    return (y_prompt, y_sample, jnp.stack(kp, axis=2), jnp.stack(vp, axis=2), jnp.stack(ip, axis=2),
            jnp.stack(hp, axis=1), jnp.stack(cp, axis=1), jnp.stack(dp, axis=1), jnp.stack(lp, axis=1),
            jnp.stack(ks, axis=2), jnp.stack(vs, axis=2), jnp.stack(iks, axis=2), jnp.stack(hsm, axis=1),
            jnp.stack(csm, axis=1), jnp.stack(dsm, axis=1), jnp.stack(lsm, axis=1))
```
